```python
import math
import jax, jax.numpy as jnp
from jax import lax
import numpy as np

D_MODEL = 1024
BATCH = 2
SEQ = 8192
DEPTH = 1

CHUNK = 64
Q_BLOCK = 128
FOX_HEADS = 8
FOX_HEAD_DIM = 64
FOX_WIDTH = FOX_HEADS * FOX_HEAD_DIM
DIFF_HEADS = 4
DIFF_HEAD_DIM = 64
DIFF_V_DIM = 2 * DIFF_HEAD_DIM
DIFF_WIDTH = DIFF_HEADS * DIFF_V_DIM
N_BRANCH = 2
IN_COLS = 3 * FOX_WIDTH + FOX_HEADS + 3 * DIFF_WIDTH + N_BRANCH * D_MODEL
N_GROUPS = 4
EXPERTS_PER_GROUP = 8
N_EXPERTS = N_GROUPS * EXPERTS_PER_GROUP
TOP_K_EXPERT = 2
D_EXPERT = D_MODEL // 2
MOE_BLOCK = 128
EPS = 1e-6

kernel_name = "fox_diffattn_hiermoe_hybrid_block"


def rms_norm(x, g):
    xf = x.astype(jnp.float32)
    y = xf * lax.rsqrt(jnp.mean(xf * xf, axis=-1, keepdims=True) + EPS)
    return (y * g.astype(jnp.float32)).astype(x.dtype)


def to_heads(t, n_heads):
    b, s, _ = t.shape
    return t.reshape(b, s, n_heads, -1).transpose(0, 2, 1, 3)


def from_heads(t):
    b, h, s, d = t.shape
    return t.transpose(0, 2, 1, 3).reshape(b, s, h * d)


def fox_attention(q, k, v, log_f):
    B, H, S, d = q.shape
    F = jnp.cumsum(log_f, axis=-1)
    pos = jnp.arange(S)
    scale = d ** -0.5

    def block(i):
        start = i * Q_BLOCK
        qb = lax.dynamic_slice_in_dim(q, start, Q_BLOCK, axis=2)
        Fq = lax.dynamic_slice_in_dim(F, start, Q_BLOCK, axis=2)
        tq = start + jnp.arange(Q_BLOCK)
        s = jnp.einsum('bhqd,bhkd->bhqk', qb, k, preferred_element_type=jnp.float32) * scale
        s = s + Fq[..., :, None] - F[..., None, :]
        mask = pos[None, :] <= tq[:, None]
        s = jnp.where(mask, s, -jnp.inf)
        p = jax.nn.softmax(s, axis=-1).astype(v.dtype)
        return jnp.einsum('bhqk,bhkd->bhqd', p, v)

    out = lax.map(block, jnp.arange(S // Q_BLOCK))
    return out.transpose(1, 2, 0, 3, 4).reshape(B, H, S, d)


def diff_attention(q1, q2, k1, k2, v, lam, slopes):
    B, H, S, d = q1.shape
    pos = jnp.arange(S)
    chunk_k = pos // CHUNK
    scale = d ** -0.5

    def block(i):
        start = i * Q_BLOCK
        q1b = lax.dynamic_slice_in_dim(q1, start, Q_BLOCK, axis=2)
        q2b = lax.dynamic_slice_in_dim(q2, start, Q_BLOCK, axis=2)
        tq = start + jnp.arange(Q_BLOCK)
        dist = jnp.abs(tq[:, None] - pos[None, :]).astype(jnp.float32)
        bias = -slopes[:, None, None] * dist
        mask = chunk_k[None, :] <= (tq // CHUNK)[:, None]
        s1 = jnp.einsum('bhqd,bhkd->bhqk', q1b, k1, preferred_element_type=jnp.float32) * scale + bias
        s2 = jnp.einsum('bhqd,bhkd->bhqk', q2b, k2, preferred_element_type=jnp.float32) * scale + bias
        p1 = jax.nn.softmax(jnp.where(mask, s1, -jnp.inf), axis=-1)
        p2 = jax.nn.softmax(jnp.where(mask, s2, -jnp.inf), axis=-1)
        a = (p1 - lam * p2).astype(v.dtype)
        return jnp.einsum('bhqk,bhkd->bhqd', a, v)

    out = lax.map(block, jnp.arange(S // Q_BLOCK))
    return out.transpose(1, 2, 0, 3, 4).reshape(B, H, S, v.shape[-1])


def hybrid_mixer(h, w_in, b_fgate, b_gate, lq1, lk1, lq2, lk2, g_subln, w_pa, w_pb, w_o, lam_init):
    z = h @ w_in
    widths = [FOX_WIDTH] * 3 + [FOX_HEADS] + [DIFF_WIDTH] * 3 + [D_MODEL] * N_BRANCH
    splits = [int(c) for c in np.cumsum(widths)[:-1]]
    fq, fk, fv, ff, dq, dk, dv, ga, gb = jnp.split(z, splits, axis=-1)

    log_f = jax.nn.log_sigmoid((ff + b_fgate).astype(jnp.float32)).transpose(0, 2, 1)
    o_a = from_heads(fox_attention(to_heads(fq, FOX_HEADS), to_heads(fk, FOX_HEADS),
                                   to_heads(fv, FOX_HEADS), log_f))

    qh = to_heads(dq, DIFF_HEADS)
    kh = to_heads(dk, DIFF_HEADS)
    vh = to_heads(dv, DIFF_HEADS)
    q1, q2 = qh[..., :DIFF_HEAD_DIM], qh[..., DIFF_HEAD_DIM:]
    k1, k2 = kh[..., :DIFF_HEAD_DIM], kh[..., DIFF_HEAD_DIM:]
    f32 = jnp.float32
    lam = (jnp.exp(jnp.sum(lq1.astype(f32) * lk1.astype(f32)))
           - jnp.exp(jnp.sum(lq2.astype(f32) * lk2.astype(f32))) + lam_init)
    slopes = jnp.exp2(-8.0 / DIFF_HEADS * jnp.arange(1, DIFF_HEADS + 1, dtype=f32))
    ob = diff_attention(q1, q2, k1, k2, vh, lam, slopes)
    ob = rms_norm(ob, g_subln) * (1.0 - lam_init)
    o_b = from_heads(ob)

    gate_a = jax.nn.sigmoid(ga + b_gate[:D_MODEL])
    gate_b = jax.nn.sigmoid(gb + b_gate[D_MODEL:])
    y = gate_a * (o_a @ w_pa) + gate_b * (o_b @ w_pb)
    return y @ w_o


def hier_moe(h, w_group, b_group, w_expert, b_expert, w1, w3, w2):
    B, S, D = h.shape
    N = B * S
    hf = h.reshape(N, D)
    f32 = jnp.float32
    gl = (hf @ w_group).astype(f32) + b_group.astype(f32)
    gp = jax.nn.softmax(gl, axis=-1)
    g_idx = jnp.argmax(gl, axis=-1)
    g_w = jnp.take_along_axis(gp, g_idx[:, None], axis=-1)[:, 0]
    el = (hf @ w_expert).astype(f32).reshape(N, N_GROUPS, EXPERTS_PER_GROUP)
    el = jnp.take_along_axis(el, g_idx[:, None, None], axis=1)[:, 0] + b_expert.astype(f32)[g_idx]
    ep = jax.nn.softmax(el, axis=-1)
    vals, idx = lax.top_k(ep, TOP_K_EXPERT)
    vals = vals / jnp.sum(vals, axis=-1, keepdims=True)
    weights = g_w[:, None] * vals
    expert_ids = (g_idx[:, None] * EXPERTS_PER_GROUP + idx).astype(jnp.int32)

    A = N * TOP_K_EXPERT
    e_flat = expert_ids.reshape(A)
    w_flat = weights.reshape(A)
    tok = jnp.repeat(jnp.arange(N, dtype=jnp.int32), TOP_K_EXPERT)
    order = jnp.argsort(e_flat)
    e_s, tok_s, w_s = e_flat[order], tok[order], w_flat[order]
    counts = jnp.bincount(e_flat, length=N_EXPERTS)
    padded = (counts + MOE_BLOCK - 1) // MOE_BLOCK * MOE_BLOCK
    start = jnp.cumsum(counts) - counts
    pstart = jnp.cumsum(padded) - padded
    dest = pstart[e_s] + jnp.arange(A) - start[e_s]
    P = A + N_EXPERTS * MOE_BLOCK
    n_blk = P // MOE_BLOCK
    row_tok = jnp.zeros((P,), jnp.int32).at[dest].set(tok_s)
    row_w = jnp.zeros((P,), f32).at[dest].set(w_s)
    blk_e = jnp.searchsorted(jnp.cumsum(padded), jnp.arange(n_blk) * MOE_BLOCK, side='right')
    blk_e = jnp.minimum(blk_e, N_EXPERTS - 1)
    xb = hf[row_tok].reshape(n_blk, MOE_BLOCK, D)

    def expert_block(args):
        xblk, e = args
        return (jax.nn.silu(xblk @ w1[e]) * (xblk @ w3[e])) @ w2[e]

    yb = lax.map(expert_block, (xb, blk_e)).reshape(P, D)
    out = jnp.zeros_like(hf).at[row_tok].add(yb * row_w[:, None].astype(yb.dtype))
    return out.reshape(B, S, D)


def setup_inputs(seed: int = 0) -> dict:
    key = jax.random.key(seed)
    ks = jax.random.split(key, 24)
    n = jax.random.normal
    f = jnp.float32
    L = DEPTH
    return {
        "x": n(ks[0], (BATCH, SEQ, D_MODEL), f),
        "g_mix": 1.0 + 0.01 * n(ks[1], (L, D_MODEL), f),
        "w_in": n(ks[2], (L, D_MODEL, IN_COLS), f) * D_MODEL ** -0.5,
        "b_fgate": 1.0 + 0.5 * n(ks[3], (L, FOX_HEADS), f),
        "b_gate": 0.1 * n(ks[4], (L, N_BRANCH * D_MODEL), f),
        "lam_q1": 0.1 * n(ks[5], (L, DIFF_HEAD_DIM), f),
        "lam_k1": 0.1 * n(ks[6], (L, DIFF_HEAD_DIM), f),
        "lam_q2": 0.1 * n(ks[7], (L, DIFF_HEAD_DIM), f),
        "lam_k2": 0.1 * n(ks[8], (L, DIFF_HEAD_DIM), f),
        "g_subln": 1.0 + 0.01 * n(ks[9], (L, DIFF_V_DIM), f),
        "w_pa": n(ks[10], (L, FOX_WIDTH, D_MODEL), f) * FOX_WIDTH ** -0.5,
        "w_pb": n(ks[11], (L, DIFF_WIDTH, D_MODEL), f) * DIFF_WIDTH ** -0.5,
        "w_o": n(ks[12], (L, D_MODEL, D_MODEL), f) * D_MODEL ** -0.5,
        "g_moe": 1.0 + 0.01 * n(ks[13], (L, D_MODEL), f),
        "w_group": n(ks[14], (L, D_MODEL, N_GROUPS), f) * D_MODEL ** -0.5,
        "b_group": 0.01 * n(ks[15], (L, N_GROUPS), f),
        "w_expert": n(ks[16], (L, D_MODEL, N_EXPERTS), f) * D_MODEL ** -0.5,
        "b_expert": 0.01 * n(ks[17], (L, N_GROUPS, EXPERTS_PER_GROUP), f),
        "w1": n(ks[18], (L, N_EXPERTS, D_MODEL, D_EXPERT), f) * D_MODEL ** -0.5,
        "w3": n(ks[19], (L, N_EXPERTS, D_MODEL, D_EXPERT), f) * D_MODEL ** -0.5,
        "w2": n(ks[20], (L, N_EXPERTS, D_EXPERT, D_MODEL), f) * D_EXPERT ** -0.5,
        "g_final": 1.0 + 0.01 * n(ks[21], (D_MODEL,), f),
    }


def reference(x, g_mix, w_in, b_fgate, b_gate, lam_q1, lam_k1, lam_q2, lam_k2, g_subln,
              w_pa, w_pb, w_o, g_moe, w_group, b_group, w_expert, b_expert, w1, w3, w2, g_final):
    for l in range(DEPTH):
        lam_init = 0.8 - 0.6 * math.exp(-0.3 * l)
        h = rms_norm(x, g_mix[l])
        x = x + hybrid_mixer(h, w_in[l], b_fgate[l], b_gate[l], lam_q1[l], lam_k1[l],
                             lam_q2[l], lam_k2[l], g_subln[l], w_pa[l], w_pb[l], w_o[l], lam_init)
        h = rms_norm(x, g_moe[l])
        x = x + hier_moe(h, w_group[l], b_group[l], w_expert[l], b_expert[l], w1[l], w3[l], w2[l])
    return rms_norm(x, g_final)
```

```python
import functools

import jax
import jax.numpy as jnp
from jax import lax
from jax.experimental import pallas as pl
from jax.experimental.pallas import tpu as pltpu

F32 = jnp.float32
BF16 = jnp.bfloat16

D_MODEL = 1024
FOX_HEADS = 8
FOX_HEAD_DIM = 64
DIFF_HEADS = 4
DIFF_HEAD_DIM = 64
DIFF_V_DIM = 128
CHUNK = 64
N_GROUPS = 4
EXPERTS_PER_GROUP = 8
N_EXPERTS = 32
D_EXPERT = 512
EPS = 1e-6
LAM_INIT = 0.8 - 0.6 * 1.0

LANE = 128
NEG = -1e30
ATT_TILE = 512
PROJ_ROWS = 256
MOE_ROWS = 256
MOVE_ROWS = 256
VMEM_LIMIT = 56 * 1024 * 1024

_QF, _KF, _VF, _FF = 0, 1024, 2048, 3072
_Q1, _Q2, _K1, _K2, _VD, _PROJ_COLS = 3200, 3712, 4224, 4736, 5248, 6272


def _rms(x, g):
    return x * lax.rsqrt(jnp.mean(x * x, axis=-1, keepdims=True) + EPS) * g


def _split3(r):
    r0 = r.astype(BF16).astype(F32)
    r1 = (r - r0).astype(BF16).astype(F32)
    r2 = (r - r0 - r1).astype(BF16).astype(F32)
    return r0, r1, r2


def _proj_kernel(x_ref, g_ref, w_ref, b_ref, bf_ref,
                 qf_ref, kf_ref, vf_ref, q1_ref, q2_ref, k1_ref, k2_ref, vd_ref,
                 fcol_ref, cblk_ref, carry_ref, c_ref, *, bm, steps_per_seq, tile):
    i = pl.program_id(0)
    h = _rms(x_ref[...], g_ref[...]).astype(BF16)

    def proj(lo, hi):
        return jnp.dot(h, w_ref[:, lo:hi], preferred_element_type=F32) + b_ref[:, lo:hi]

    qf_ref[...] = proj(_QF, _KF).astype(BF16)
    vf_ref[...] = proj(_VF, _FF).astype(BF16)
    q1_ref[...] = proj(_Q1, _Q2).astype(BF16)
    q2_ref[...] = proj(_Q2, _K1).astype(BF16)
    vd_ref[...] = proj(_VD, _PROJ_COLS).astype(BF16)

    z = proj(_FF, _Q1) + bf_ref[...]
    logf = jnp.minimum(z, 0.0) - jnp.log1p(jnp.exp(-jnp.abs(z)))

    @pl.when(i % steps_per_seq == 0)
    def _():
        carry_ref[...] = jnp.zeros_like(carry_ref)

    @pl.when(i % (tile // bm) == 0)
    def _():
        c_ref[...] = carry_ref[...]

    row = lax.broadcasted_iota(jnp.int32, (bm, bm), 0)
    col = lax.broadcasted_iota(jnp.int32, (bm, bm), 1)
    tri = (col <= row).astype(BF16)
    l0, l1, l2 = _split3(logf)
    fcum = (jnp.dot(tri, l0.astype(BF16), preferred_element_type=F32)
            + jnp.dot(tri, l1.astype(BF16), preferred_element_type=F32)
            + jnp.dot(tri, l2.astype(BF16), preferred_element_type=F32)) + carry_ref[...]
    carry_ref[...] = fcum[bm - 1:bm, :]
    fcol_ref[...] = fcum
    cblk_ref[...] = jnp.broadcast_to(c_ref[...], cblk_ref.shape)

    kz = proj(_KF, _VF)
    rel = fcum - c_ref[...]
    lane = lax.broadcasted_iota(jnp.int32, (bm, LANE), 1)
    for hh in range(FOX_HEADS):
        r0, r1, r2 = _split3(rel[:, hh:hh + 1])
        aug = jnp.where(lane == FOX_HEAD_DIM, r0,
                        jnp.where(lane == FOX_HEAD_DIM + 1, r1,
                                  jnp.where(lane == FOX_HEAD_DIM + 2, r2, 0.0)))
        kf_ref[:, hh * LANE:(hh + 1) * LANE] = (kz[:, hh * LANE:(hh + 1) * LANE] + aug).astype(BF16)

    width = DIFF_HEADS * LANE
    jrel = (i * bm) % tile + lax.broadcasted_iota(jnp.int32, (bm, width), 0)
    lane4 = lax.broadcasted_iota(jnp.int32, (bm, width), 1) & (LANE - 1)
    jhi = ((jrel >> 7) << 7).astype(F32)
    jlo = (jrel & (LANE - 1)).astype(F32)
    augk = jnp.where(lane4 == DIFF_HEAD_DIM, jhi, jnp.where(lane4 == DIFF_HEAD_DIM + 1, jlo, 0.0))
    k1_ref[...] = (proj(_K1, _K2) + augk).astype(BF16)
    k2_ref[...] = (proj(_K2, _VD) + augk).astype(BF16)


def _proj_call(x2, g_mix, w_a, b_a, bf_pad, seq, tile):
    n = x2.shape[0]
    bm = PROJ_ROWS
    steps = n // bm
    row = lambda w: pl.BlockSpec((bm, w), lambda i: (i, 0))
    full = lambda a: pl.BlockSpec(a.shape, lambda i: (0,) * a.ndim)
    out_shape = (
        jax.ShapeDtypeStruct((n, 1024), BF16),
        jax.ShapeDtypeStruct((n, 1024), BF16),
        jax.ShapeDtypeStruct((n, 1024), BF16),
        jax.ShapeDtypeStruct((n, 512), BF16),
        jax.ShapeDtypeStruct((n, 512), BF16),
        jax.ShapeDtypeStruct((n, 512), BF16),
        jax.ShapeDtypeStruct((n, 512), BF16),
        jax.ShapeDtypeStruct((n, 1024), BF16),
        jax.ShapeDtypeStruct((n, LANE), F32),
        jax.ShapeDtypeStruct((steps, 8, LANE), F32),
    )
    out_specs = (row(1024), row(1024), row(1024), row(512), row(512), row(512), row(512),
                 row(1024), row(LANE), pl.BlockSpec((None, 8, LANE), lambda i: (i, 0, 0)))
    return pl.pallas_call(
        functools.partial(_proj_kernel, bm=bm, steps_per_seq=seq // bm, tile=tile),
        grid=(steps,),
        in_specs=[row(D_MODEL), full(g_mix), full(w_a), full(b_a), full(bf_pad)],
        out_specs=out_specs,
        out_shape=out_shape,
        scratch_shapes=[pltpu.VMEM((1, LANE), F32), pltpu.VMEM((1, LANE), F32)],
        compiler_params=pltpu.CompilerParams(dimension_semantics=("arbitrary",),
                                             vmem_limit_bytes=VMEM_LIMIT),
        name="proj",
    )(x2, g_mix, w_a, b_a, bf_pad)


_NT = (((1,), (1,)), ((), ()))


def _flash_update(s, d, v, m_ref, acc_ref):
    m_old = m_ref[...]
    m_new = jnp.maximum(m_old, jnp.max(s, axis=1, keepdims=True) + d)
    alpha = jnp.exp(m_old - m_new)
    p = jnp.exp(s - (m_new - d))
    acc_ref[...] = alpha * acc_ref[...] + jnp.dot(p.astype(BF16), v, preferred_element_type=F32)
    m_ref[...] = m_new


def _fox_kernel(c_ref, q_ref, k_ref, v_ref, f_ref, o_ref, m_ref, acc_ref, *, tile, nk):
    b, hh, i = pl.program_id(0), pl.program_id(1), pl.program_id(2)
    q = q_ref[...]
    fi = f_ref[...]
    m_ref[...] = jnp.full_like(m_ref, NEG)
    acc_ref[...] = jnp.zeros_like(acc_ref)
    base = (b * FOX_HEADS + hh) * nk

    def scores(j):
        off = pl.multiple_of(j * tile, tile)
        k = k_ref[pl.ds(off, tile), :]
        v = v_ref[pl.ds(off, tile), :]
        return lax.dot_general(q, k, _NT, preferred_element_type=F32), v

    def body(j, carry):
        s, v = scores(j)
        _flash_update(s, fi - c_ref[base + j], v, m_ref, acc_ref)
        return carry

    lax.fori_loop(0, i, body, 0)

    s, v = scores(i)
    row = lax.broadcasted_iota(jnp.int32, (tile, tile), 0)
    col = lax.broadcasted_iota(jnp.int32, (tile, tile), 1)
    s = jnp.where(col <= row, s, NEG)
    _flash_update(s, fi - c_ref[base + i], v, m_ref, acc_ref)

    acc = acc_ref[...]
    lane = lax.broadcasted_iota(jnp.int32, acc.shape, 1)
    o_ref[...] = jnp.where(lane < FOX_HEAD_DIM, acc / acc[:, FOX_HEAD_DIM:FOX_HEAD_DIM + 1], 0.0).astype(BF16)


def _fox_call(c_flat, qf, kf, vf, fcol, tile):
    bsz, seq, _ = qf.shape
    nq = seq // tile
    grid_spec = pltpu.PrefetchScalarGridSpec(
        num_scalar_prefetch=1,
        grid=(bsz, FOX_HEADS, nq),
        in_specs=[
            pl.BlockSpec((None, tile, LANE), lambda b, h, i, c: (b, i, h)),
            pl.BlockSpec((None, seq, LANE), lambda b, h, i, c: (b, 0, h)),
            pl.BlockSpec((None, seq, LANE), lambda b, h, i, c: (b, 0, h)),
            pl.BlockSpec((None, None, tile, 1), lambda b, h, i, c: (b, h, i, 0)),
        ],
        out_specs=pl.BlockSpec((None, tile, LANE), lambda b, h, i, c: (b, i, h)),
        scratch_shapes=[pltpu.VMEM((tile, 1), F32), pltpu.VMEM((tile, LANE), F32)],
    )
    return pl.pallas_call(
        functools.partial(_fox_kernel, tile=tile, nk=nq),
        grid_spec=grid_spec,
        out_shape=jax.ShapeDtypeStruct((bsz, seq, FOX_HEADS * LANE), BF16),
        compiler_params=pltpu.CompilerParams(
            dimension_semantics=("parallel", "parallel", "arbitrary"),
            vmem_limit_bytes=VMEM_LIMIT),
        name="fox_attn",
    )(c_flat, qf, kf, vf, fcol)


def _diff_kernel(sc_ref, q1_ref, q2_ref, k1_ref, k2_ref, v_ref, g_ref, o_ref,
                 m1_ref, a1_ref, m2_ref, a2_ref, *, tile):
    hh, i = pl.program_id(1), pl.program_id(2)
    slope = sc_ref[hh]
    lam = sc_ref[DIFF_HEADS]
    q1 = q1_ref[...]
    q2 = q2_ref[...]
    for m_ref, a_ref in ((m1_ref, a1_ref), (m2_ref, a2_ref)):
        m_ref[...] = jnp.full_like(m_ref, NEG)
        a_ref[...] = jnp.zeros_like(a_ref)
    irel = lax.broadcasted_iota(jnp.int32, (tile, 1), 0)

    def tiles(j):
        off = pl.multiple_of(j * tile, tile)
        k1 = k1_ref[pl.ds(off, tile), :]
        k2 = k2_ref[pl.ds(off, tile), :]
        v = v_ref[pl.ds(off, tile), :]
        s1 = lax.dot_general(q1, k1, _NT, preferred_element_type=F32)
        s2 = lax.dot_general(q2, k2, _NT, preferred_element_type=F32)
        return s1, s2, v

    def body(j, carry):
        s1, s2, v = tiles(j)
        d = slope * ((j - i) * tile - irel).astype(F32)
        _flash_update(s1, d, v, m1_ref, a1_ref)
        _flash_update(s2, d, v, m2_ref, a2_ref)
        return carry

    lax.fori_loop(0, i, body, 0)

    s1, s2, v = tiles(i)
    row = lax.broadcasted_iota(jnp.int32, (tile, tile), 0)
    col = lax.broadcasted_iota(jnp.int32, (tile, tile), 1)
    fix = -slope * (col + jnp.abs(row - col)).astype(F32)
    allowed = (col // CHUNK) <= (row // CHUNK)
    zero = jnp.zeros((tile, 1), F32)
    _flash_update(jnp.where(allowed, s1 + fix, NEG), zero, v, m1_ref, a1_ref)
    _flash_update(jnp.where(allowed, s2 + fix, NEG), zero, v, m2_ref, a2_ref)

    a1 = a1_ref[...]
    a2 = a2_ref[...]
    o1 = a1[:, :DIFF_V_DIM] / a1[:, DIFF_V_DIM:DIFF_V_DIM + 1]
    o2 = a2[:, :DIFF_V_DIM] / a2[:, DIFF_V_DIM:DIFF_V_DIM + 1]
    ob = o1 - lam * o2
    o_ref[...] = (_rms(ob, g_ref[...]) * (1.0 - LAM_INIT)).astype(BF16)


def _diff_call(scal, q1, q2, k1, k2, vd, g_subln, tile):
    bsz, seq, _ = q1.shape
    nq = seq // tile
    qspec = pl.BlockSpec((None, tile, LANE), lambda b, h, i, c: (b, i, h))
    kspec = pl.BlockSpec((None, seq, LANE), lambda b, h, i, c: (b, 0, h))
    grid_spec = pltpu.PrefetchScalarGridSpec(
        num_scalar_prefetch=1,
        grid=(bsz, DIFF_HEADS, nq),
        in_specs=[qspec, qspec, kspec, kspec,
                  pl.BlockSpec((None, seq, 2 * LANE), lambda b, h, i, c: (b, 0, h)),
                  pl.BlockSpec((1, DIFF_V_DIM), lambda b, h, i, c: (0, 0))],
        out_specs=qspec,
        scratch_shapes=[pltpu.VMEM((tile, 1), F32), pltpu.VMEM((tile, 2 * LANE), F32),
                        pltpu.VMEM((tile, 1), F32), pltpu.VMEM((tile, 2 * LANE), F32)],
    )
    return pl.pallas_call(
        functools.partial(_diff_kernel, tile=tile),
        grid_spec=grid_spec,
        out_shape=jax.ShapeDtypeStruct((bsz, seq, DIFF_HEADS * LANE), BF16),
        compiler_params=pltpu.CompilerParams(
            dimension_semantics=("parallel", "parallel", "arbitrary"),
            vmem_limit_bytes=VMEM_LIMIT),
        name="diff_attn",
    )(scal, q1, q2, k1, k2, vd, g_subln)


def _merge_kernel(x_ref, oa_ref, ob_ref, gmix_ref, wg_ref, bg_ref, wpa_ref, wpb_ref, wo_ref,
                  gmoe_ref, wrh_ref, wrl_ref, br_ref,
                  x1_ref, h2_ref, mi_ref, mf_ref, cnt_ref, run_ref, *, bm):
    i = pl.program_id(0)

    @pl.when(i == 0)
    def _():
        run_ref[...] = jnp.zeros_like(run_ref)

    x = x_ref[...]
    h = _rms(x, gmix_ref[...]).astype(BF16)
    gates = jnp.dot(h, wg_ref[...], preferred_element_type=F32) + bg_ref[...]
    gates = 1.0 / (1.0 + jnp.exp(-gates))
    ya = jnp.dot(oa_ref[...], wpa_ref[...], preferred_element_type=F32)
    yb = jnp.dot(ob_ref[...], wpb_ref[...], preferred_element_type=F32)
    y = gates[:, :D_MODEL] * ya + gates[:, D_MODEL:] * yb
    x1 = x + jnp.dot(y.astype(BF16), wo_ref[...], preferred_element_type=F32)
    x1_ref[...] = x1
    h2 = _rms(x1, gmoe_ref[...])
    h2_ref[...] = h2

    h2h = h2.astype(BF16)
    h2l = (h2 - h2h.astype(F32)).astype(BF16)
    r = (jnp.dot(h2h, wrh_ref[...], preferred_element_type=F32)
         + jnp.dot(h2l, wrh_ref[...], preferred_element_type=F32)
         + jnp.dot(h2h, wrl_ref[...], preferred_element_type=F32)) + br_ref[...]
    el_all = r[:, :LANE]
    gl = r[:, LANE:]

    lane_i = lax.broadcasted_iota(jnp.int32, (bm, LANE), 1)
    lane = lane_i.astype(F32)
    big = float(LANE)

    def first_argmax(vals):
        vmax = jnp.max(vals, axis=1, keepdims=True)
        idx = jnp.min(jnp.where(vals == vmax, lane, big), axis=1, keepdims=True)
        return vmax, idx

    glm = jnp.where(lane_i < N_GROUPS, gl, NEG)
    gmax, g_idx = first_argmax(glm)
    g_w = 1.0 / jnp.sum(jnp.exp(glm - gmax), axis=1, keepdims=True)

    in_group = ((lane_i >> 3).astype(F32) == g_idx) & (lane_i < N_EXPERTS)
    elm = jnp.where(in_group, el_all, NEG)
    e1, idx1 = first_argmax(elm)
    elm2 = jnp.where(lane == idx1, NEG, elm)
    e2, idx2 = first_argmax(elm2)
    t = jnp.exp(e2 - e1)
    w1 = g_w / (1.0 + t)
    w2 = g_w * t / (1.0 + t)

    oh1 = (lane == idx1).astype(F32)
    oh2 = (lane == idx2).astype(F32)
    oh = oh1 + oh2
    row = lax.broadcasted_iota(jnp.int32, (bm, bm), 0)
    col = lax.broadcasted_iota(jnp.int32, (bm, bm), 1)
    strict = (col < row).astype(BF16)
    before = jnp.dot(strict, oh.astype(BF16), preferred_element_type=F32) + run_ref[...]
    rank1 = jnp.sum(oh1 * before, axis=1, keepdims=True)
    rank2 = jnp.sum(oh2 * before, axis=1, keepdims=True)
    run_ref[...] = run_ref[...] + jnp.sum(oh, axis=0, keepdims=True)
    cnt_ref[...] = jnp.broadcast_to(run_ref[...], cnt_ref.shape)

    meta = jnp.where(lane_i == 0, idx1, jnp.where(lane_i == 1, idx2,
                     jnp.where(lane_i == 2, rank1, jnp.where(lane_i == 3, rank2, 0.0))))
    mi_ref[...] = meta.astype(jnp.int32)
    mf_ref[...] = jnp.where(lane_i == 0, w1, jnp.where(lane_i == 1, w2, 0.0))


def _merge_call(x2, oa, ob, g_mix, w_g, b_g, w_pa, w_pb, w_o, g_moe, wr_hi, wr_lo, b_r):
    n = x2.shape[0]
    bm = PROJ_ROWS
    row = lambda w: pl.BlockSpec((bm, w), lambda i: (i, 0))
    full = lambda a: pl.BlockSpec(a.shape, lambda i: (0,) * a.ndim)
    consts = (g_mix, w_g, b_g, w_pa, w_pb, w_o, g_moe, wr_hi, wr_lo, b_r)
    return pl.pallas_call(
        functools.partial(_merge_kernel, bm=bm),
        grid=(n // bm,),
        in_specs=[row(D_MODEL), row(oa.shape[1]), row(ob.shape[1])] + [full(a) for a in consts],
        out_specs=(row(D_MODEL), row(D_MODEL), row(LANE), row(LANE),
                   pl.BlockSpec((8, LANE), lambda i: (0, 0))),
        out_shape=(jax.ShapeDtypeStruct((n, D_MODEL), F32),
                   jax.ShapeDtypeStruct((n, D_MODEL), F32),
                   jax.ShapeDtypeStruct((n, LANE), jnp.int32),
                   jax.ShapeDtypeStruct((n, LANE), F32),
                   jax.ShapeDtypeStruct((8, LANE), F32)),
        scratch_shapes=[pltpu.VMEM((1, LANE), F32)],
        compiler_params=pltpu.CompilerParams(dimension_semantics=("arbitrary",),
                                             vmem_limit_bytes=VMEM_LIMIT),
        name="merge_router",
    )(x2, oa, ob, *consts)


def _row_copy(src_ref, src_row, dst_ref, dst_row, sem):
    return pltpu.make_async_copy(src_ref.at[pl.ds(src_row, 1), :],
                                 dst_ref.at[pl.ds(dst_row, 1), :], sem)


def _dispatch_kernel(dest_ref, h_ref, xs_in_ref, xs_ref, sem, *, bt):
    del xs_in_ref
    i = pl.program_id(0)

    def issue(r, carry):
        a = (i * bt + r) * 2
        _row_copy(h_ref, r, xs_ref, dest_ref[a], sem).start()
        _row_copy(h_ref, r, xs_ref, dest_ref[a + 1], sem).start()
        return carry

    lax.fori_loop(0, bt, issue, 0)

    def drain(r, carry):
        _row_copy(h_ref, 0, xs_ref, 0, sem).wait()
        _row_copy(h_ref, 0, xs_ref, 0, sem).wait()
        return carry

    lax.fori_loop(0, bt, drain, 0)


def _dispatch_call(dest_flat, h2, p_rows):
    n, d = h2.shape
    bt = MOVE_ROWS
    grid_spec = pltpu.PrefetchScalarGridSpec(
        num_scalar_prefetch=1,
        grid=(n // bt,),
        in_specs=[pl.BlockSpec((bt, d), lambda i, dest: (i, 0)),
                  pl.BlockSpec(memory_space=pl.ANY)],
        out_specs=pl.BlockSpec(memory_space=pl.ANY),
        scratch_shapes=[pltpu.SemaphoreType.DMA(())],
    )
    return pl.pallas_call(
        functools.partial(_dispatch_kernel, bt=bt),
        grid_spec=grid_spec,
        out_shape=jax.ShapeDtypeStruct((p_rows, d), h2.dtype),
        input_output_aliases={2: 0},
        compiler_params=pltpu.CompilerParams(dimension_semantics=("arbitrary",)),
        name="dispatch",
    )(dest_flat, h2, jnp.zeros((p_rows, d), h2.dtype))


def _expert_kernel(blk_e_ref, used_ref, x_ref, w1_ref, w3_ref, w2_ref, y_ref):
    del blk_e_ref
    i = pl.program_id(0)

    @pl.when(i < used_ref[0])
    def _():
        xb = x_ref[...].astype(BF16)
        a = jnp.dot(xb, w1_ref[...], preferred_element_type=F32)
        g = jnp.dot(xb, w3_ref[...], preferred_element_type=F32)
        mid = (a / (1.0 + jnp.exp(-a))) * g
        y_ref[...] = jnp.dot(mid.astype(BF16), w2_ref[...], preferred_element_type=F32)

    @pl.when(i >= used_ref[0])
    def _():
        y_ref[...] = jnp.zeros_like(y_ref)


def _expert_call(blk_e, n_used, xs, w1, w3, w2):
    p_rows, d = xs.shape
    blk = MOE_ROWS
    x_idx = lambda i, be, nu: (jnp.minimum(i, nu[0] - 1), 0)
    grid_spec = pltpu.PrefetchScalarGridSpec(
        num_scalar_prefetch=2,
        grid=(p_rows // blk,),
        in_specs=[pl.BlockSpec((blk, d), x_idx),
                  pl.BlockSpec((None, d, D_EXPERT), lambda i, be, nu: (be[i], 0, 0)),
                  pl.BlockSpec((None, d, D_EXPERT), lambda i, be, nu: (be[i], 0, 0)),
                  pl.BlockSpec((None, D_EXPERT, d), lambda i, be, nu: (be[i], 0, 0))],
        out_specs=pl.BlockSpec((blk, d), lambda i, be, nu: (i, 0)),
    )
    return pl.pallas_call(
        _expert_kernel,
        grid_spec=grid_spec,
        out_shape=jax.ShapeDtypeStruct((p_rows, d), F32),
        compiler_params=pltpu.CompilerParams(dimension_semantics=("arbitrary",),
                                             vmem_limit_bytes=VMEM_LIMIT),
        name="experts",
    )(blk_e, n_used, xs, w1, w3, w2)


def _combine_kernel(dest_ref, x1_ref, mf_ref, g_ref, yb_ref, o_ref, buf0, buf1, sem, *, bt):
    i = pl.program_id(0)

    def issue(r, carry):
        a = (i * bt + r) * 2
        _row_copy(yb_ref, dest_ref[a], buf0, r, sem).start()
        _row_copy(yb_ref, dest_ref[a + 1], buf1, r, sem).start()
        return carry

    lax.fori_loop(0, bt, issue, 0)

    def drain(r, carry):
        _row_copy(yb_ref, 0, buf0, 0, sem).wait()
        _row_copy(yb_ref, 0, buf1, 0, sem).wait()
        return carry

    lax.fori_loop(0, bt, drain, 0)

    mf = mf_ref[...]
    x2 = x1_ref[...] + mf[:, 0:1] * buf0[...] + mf[:, 1:2] * buf1[...]
    o_ref[...] = _rms(x2, g_ref[...])


def _combine_call(dest_flat, x1, mf, g_final, yb):
    n, d = x1.shape
    bt = MOVE_ROWS
    grid_spec = pltpu.PrefetchScalarGridSpec(
        num_scalar_prefetch=1,
        grid=(n // bt,),
        in_specs=[pl.BlockSpec((bt, d), lambda i, dest: (i, 0)),
                  pl.BlockSpec((bt, LANE), lambda i, dest: (i, 0)),
                  pl.BlockSpec((1, d), lambda i, dest: (0, 0)),
                  pl.BlockSpec(memory_space=pl.ANY)],
        out_specs=pl.BlockSpec((bt, d), lambda i, dest: (i, 0)),
        scratch_shapes=[pltpu.VMEM((bt, d), F32), pltpu.VMEM((bt, d), F32),
                        pltpu.SemaphoreType.DMA(())],
    )
    return pl.pallas_call(
        functools.partial(_combine_kernel, bt=bt),
        grid_spec=grid_spec,
        out_shape=jax.ShapeDtypeStruct((n, d), F32),
        compiler_params=pltpu.CompilerParams(dimension_semantics=("arbitrary",)),
        name="combine",
    )(dest_flat, x1, mf, g_final, yb)


def _pad_heads(w, heads, width, padded, scale=1.0):
    d = w.shape[0]
    w = (w * scale).reshape(d, heads, width)
    return jnp.pad(w, ((0, 0), (0, 0), (0, padded - width))).reshape(d, heads * padded)


def _build_proj_params(w_in, b_fgate):
    fw, dw = FOX_HEADS * FOX_HEAD_DIM, DIFF_HEADS * DIFF_V_DIM
    o = 0
    fq, fk, fv = (w_in[:, o + k * fw:o + (k + 1) * fw] for k in range(3))
    o += 3 * fw
    ff = w_in[:, o:o + FOX_HEADS]
    o += FOX_HEADS
    dq, dk, dv = (w_in[:, o + k * dw:o + (k + 1) * dw] for k in range(3))
    d = w_in.shape[0]
    dq = dq.reshape(d, DIFF_HEADS, 2, DIFF_HEAD_DIM)
    dk = dk.reshape(d, DIFF_HEADS, 2, DIFF_HEAD_DIM)
    fscale = FOX_HEAD_DIM ** -0.5
    dscale = DIFF_HEAD_DIM ** -0.5
    half = lambda t, k: t[:, :, k, :].reshape(d, DIFF_HEADS * DIFF_HEAD_DIM)
    w_a = jnp.concatenate([
        _pad_heads(fq, FOX_HEADS, FOX_HEAD_DIM, LANE, fscale),
        _pad_heads(fk, FOX_HEADS, FOX_HEAD_DIM, LANE),
        _pad_heads(fv, FOX_HEADS, FOX_HEAD_DIM, LANE),
        jnp.pad(ff, ((0, 0), (0, LANE - FOX_HEADS))),
        _pad_heads(half(dq, 0), DIFF_HEADS, DIFF_HEAD_DIM, LANE, dscale),
        _pad_heads(half(dq, 1), DIFF_HEADS, DIFF_HEAD_DIM, LANE, dscale),
        _pad_heads(half(dk, 0), DIFF_HEADS, DIFF_HEAD_DIM, LANE),
        _pad_heads(half(dk, 1), DIFF_HEADS, DIFF_HEAD_DIM, LANE),
        _pad_heads(dv, DIFF_HEADS, DIFF_V_DIM, 2 * LANE),
    ], axis=1).astype(BF16)

    slopes = [2.0 ** (-8.0 / DIFF_HEADS * (hh + 1)) for hh in range(DIFF_HEADS)]
    b = jnp.zeros((_PROJ_COLS,), F32)
    for hh in range(FOX_HEADS):
        b = b.at[_QF + hh * LANE + FOX_HEAD_DIM:_QF + hh * LANE + FOX_HEAD_DIM + 3].set(-1.0)
        b = b.at[_VF + hh * LANE + FOX_HEAD_DIM].set(1.0)
    for hh in range(DIFF_HEADS):
        for base in (_Q1, _Q2):
            b = b.at[base + hh * LANE + DIFF_HEAD_DIM:base + hh * LANE + DIFF_HEAD_DIM + 2].set(slopes[hh])
        b = b.at[_VD + hh * 2 * LANE + DIFF_V_DIM].set(1.0)
    bf_pad = jnp.pad(b_fgate, (0, LANE - FOX_HEADS))[None, :]
    return w_a, b[None, :], bf_pad, slopes


def kernel(x, g_mix, w_in, b_fgate, b_gate, lam_q1, lam_k1, lam_q2, lam_k2, g_subln, w_pa, w_pb, w_o,
           g_moe, w_group, b_group, w_expert, b_expert, w1, w3, w2, g_final):
    bsz, seq, d = x.shape
    n = bsz * seq
    tile = ATT_TILE
    x2 = x.reshape(n, d)

    w_a, b_a, bf_pad, slopes = _build_proj_params(w_in[0], b_fgate[0])
    qf, kf, vf, q1, q2, k1, k2, vd, fcol, cblk = _proj_call(x2, g_mix, w_a, b_a, bf_pad, seq, tile)
    shp = lambda a: a.reshape(bsz, seq, a.shape[1])
    nk = seq // tile
    c_flat = (cblk[::tile // PROJ_ROWS, 0, :FOX_HEADS].reshape(bsz, nk, FOX_HEADS)
              .transpose(0, 2, 1).reshape(-1))
    f_heads = fcol[:, :FOX_HEADS].reshape(bsz, seq, FOX_HEADS).transpose(0, 2, 1)[..., None]

    oa = _fox_call(c_flat, shp(qf), shp(kf), shp(vf), f_heads, tile)
    lam = (jnp.exp(jnp.sum(lam_q1[0] * lam_k1[0])) - jnp.exp(jnp.sum(lam_q2[0] * lam_k2[0])) + LAM_INIT)
    scal = jnp.concatenate([jnp.asarray(slopes, F32), lam[None].astype(F32)])
    ob = _diff_call(scal, shp(q1), shp(q2), shp(k1), shp(k2), shp(vd), g_subln, tile)

    o = 3 * FOX_HEADS * FOX_HEAD_DIM + FOX_HEADS + 3 * DIFF_HEADS * DIFF_V_DIM
    w_g = w_in[0][:, o:].astype(BF16)
    w_pa_pad = jnp.pad(w_pa[0].reshape(FOX_HEADS, FOX_HEAD_DIM, d),
                       ((0, 0), (0, LANE - FOX_HEAD_DIM), (0, 0))).reshape(FOX_HEADS * LANE, d).astype(BF16)
    w_r = jnp.zeros((d, 2 * LANE), F32)
    w_r = w_r.at[:, :N_EXPERTS].set(w_expert[0]).at[:, LANE:LANE + N_GROUPS].set(w_group[0])
    wr_hi = w_r.astype(BF16)
    wr_lo = (w_r - wr_hi.astype(F32)).astype(BF16)
    b_r = jnp.zeros((1, 2 * LANE), F32)
    b_r = b_r.at[0, :N_EXPERTS].set(b_expert[0].reshape(-1)).at[0, LANE:LANE + N_GROUPS].set(b_group[0])
    x1, h2, mi, mf, cnt = _merge_call(
        x2, oa.reshape(n, -1), ob.reshape(n, -1), g_mix, w_g, b_gate, w_pa_pad,
        w_pb[0].astype(BF16), w_o[0].astype(BF16), g_moe, wr_hi, wr_lo, b_r)

    blk = MOE_ROWS
    counts = cnt[0, :N_EXPERTS].astype(jnp.int32)
    padded = (counts + blk - 1) // blk * blk
    pend = jnp.cumsum(padded)
    pstart = pend - padded
    dest = (pstart[mi[:, 0:2]] + mi[:, 2:4]).reshape(-1)
    p_rows = n * 2 + N_EXPERTS * blk
    n_blk = p_rows // blk
    blk_e = jnp.minimum(jnp.searchsorted(pend, jnp.arange(n_blk, dtype=jnp.int32) * blk, side='right'),
                        N_EXPERTS - 1).astype(jnp.int32)
    n_used = (pend[-1:] // blk).astype(jnp.int32)

    xs = _dispatch_call(dest, h2, p_rows)
    yb = _expert_call(blk_e, n_used, xs, w1[0].astype(BF16), w3[0].astype(BF16), w2[0].astype(BF16))
    out = _combine_call(dest, x1, mf, g_final[None, :], yb)
    return out.reshape(bsz, seq, d)
```

```python
import functools

import jax
import jax.numpy as jnp
from jax import lax
from jax.experimental import pallas as pl
from jax.experimental.pallas import tpu as pltpu

F32 = jnp.float32
BF16 = jnp.bfloat16

D_MODEL = 1024
FOX_HEADS = 8
FOX_HEAD_DIM = 64
DIFF_HEADS = 4
DIFF_HEAD_DIM = 64
DIFF_V_DIM = 128
CHUNK = 64
N_GROUPS = 4
EXPERTS_PER_GROUP = 8
N_EXPERTS = 32
D_EXPERT = 512
EPS = 1e-6
LAM_INIT = 0.8 - 0.6 * 1.0

LANE = 128
NEG = -1e30
ATT_TILE = 512
MERGE_ROWS = 256
MOE_ROWS = 256
MOVE_ROWS = 256
VMEM_LIMIT = 56 * 1024 * 1024

FOX_V_ROWS = 80
DIFF_V_ROWS = 144

_TQF, _TVF, _TQ1 = 0, 1024, 1024 + FOX_HEADS * FOX_V_ROWS
_TQ2, _TVD = _TQ1 + 512, _TQ1 + 1024
_T_ROWS = _TVD + DIFF_HEADS * DIFF_V_ROWS
_KF, _FF, _K1, _K2, _ROW_COLS = 0, 1024, 1152, 1664, 2176


def _rms(x, g):
    return x * lax.rsqrt(jnp.mean(x * x, axis=-1, keepdims=True) + EPS) * g


def _split3(r):
    r0 = r.astype(BF16).astype(F32)
    r1 = (r - r0).astype(BF16).astype(F32)
    r2 = (r - r0 - r1).astype(BF16).astype(F32)
    return r0, r1, r2


_NT = (((1,), (1,)), ((), ()))


def _proj_kernel(x_ref, g_ref, wt_ref, bt_ref, w_ref, bf_ref,
                 qft_ref, vft_ref, q1t_ref, q2t_ref, vdt_ref, kf_ref, k1_ref, k2_ref,
                 fcol_ref, cblk_ref, carry_ref, *, bm, steps_per_seq):
    i = pl.program_id(0)
    h = _rms(x_ref[...], g_ref[...]).astype(BF16)

    zt = lax.dot_general(wt_ref[...], h, _NT, preferred_element_type=F32) + bt_ref[...]

    def heads(lo, n_heads, rows):
        return zt[lo:lo + n_heads * rows].reshape(n_heads, rows, bm).astype(BF16)

    qft_ref[...] = heads(_TQF, FOX_HEADS, LANE)
    vft_ref[...] = heads(_TVF, FOX_HEADS, FOX_V_ROWS)
    q1t_ref[...] = heads(_TQ1, DIFF_HEADS, LANE)
    q2t_ref[...] = heads(_TQ2, DIFF_HEADS, LANE)
    vdt_ref[...] = heads(_TVD, DIFF_HEADS, DIFF_V_ROWS)

    z = jnp.dot(h, w_ref[...], preferred_element_type=F32)

    zf = z[:, _FF:_K1] + bf_ref[...]
    logf = jnp.minimum(zf, 0.0) - jnp.log1p(jnp.exp(-jnp.abs(zf)))

    @pl.when(i % steps_per_seq == 0)
    def _():
        carry_ref[...] = jnp.zeros_like(carry_ref)

    c = carry_ref[...]
    row = lax.broadcasted_iota(jnp.int32, (bm, bm), 0)
    col = lax.broadcasted_iota(jnp.int32, (bm, bm), 1)
    tri = (col <= row).astype(BF16)
    l0, l1, l2 = _split3(logf)
    rel = (jnp.dot(tri, l0.astype(BF16), preferred_element_type=F32)
           + jnp.dot(tri, l1.astype(BF16), preferred_element_type=F32)
           + jnp.dot(tri, l2.astype(BF16), preferred_element_type=F32))
    fcum = rel + c
    carry_ref[...] = fcum[bm - 1:bm, :]
    fcol_ref[...] = fcum
    cblk_ref[...] = jnp.broadcast_to(c, cblk_ref.shape)

    lane = lax.broadcasted_iota(jnp.int32, (bm, LANE), 1)
    for hh in range(FOX_HEADS):
        r0, r1, r2 = _split3(rel[:, hh:hh + 1])
        aug = jnp.where(lane == FOX_HEAD_DIM, r0,
                        jnp.where(lane == FOX_HEAD_DIM + 1, r1,
                                  jnp.where(lane == FOX_HEAD_DIM + 2, r2, 0.0)))
        lo = _KF + hh * LANE
        kf_ref[:, hh * LANE:(hh + 1) * LANE] = (z[:, lo:lo + LANE] + aug).astype(BF16)

    width = DIFF_HEADS * LANE
    jrel = lax.broadcasted_iota(jnp.int32, (bm, width), 0)
    lane4 = lax.broadcasted_iota(jnp.int32, (bm, width), 1) & (LANE - 1)
    jhi = ((jrel >> 7) << 7).astype(F32)
    jlo = (jrel & (LANE - 1)).astype(F32)
    augk = jnp.where(lane4 == DIFF_HEAD_DIM, jhi, jnp.where(lane4 == DIFF_HEAD_DIM + 1, jlo, 0.0))
    k1_ref[...] = (z[:, _K1:_K2] + augk).astype(BF16)
    k2_ref[...] = (z[:, _K2:_ROW_COLS] + augk).astype(BF16)


def _proj_call(x2, g_mix, w_t, b_t, w_row, bf_pad, bsz, seq):
    n = x2.shape[0]
    bm = ATT_TILE
    steps = n // bm
    spp = seq // bm
    row = lambda w: pl.BlockSpec((bm, w), lambda i: (i, 0))
    once = lambda a: pl.BlockSpec(a.shape, lambda i: (0,) * a.ndim, pipeline_mode=pl.Buffered(1))
    tshape = lambda heads, rows: jax.ShapeDtypeStruct((bsz, heads, spp, rows, bm), BF16)
    tspec = lambda heads, rows: pl.BlockSpec((None, heads, None, rows, bm),
                                             lambda i: (i // spp, 0, i % spp, 0, 0))
    out_shape = (
        tshape(FOX_HEADS, LANE), tshape(FOX_HEADS, FOX_V_ROWS),
        tshape(DIFF_HEADS, LANE), tshape(DIFF_HEADS, LANE), tshape(DIFF_HEADS, DIFF_V_ROWS),
        jax.ShapeDtypeStruct((n, FOX_HEADS * LANE), BF16),
        jax.ShapeDtypeStruct((n, DIFF_HEADS * LANE), BF16),
        jax.ShapeDtypeStruct((n, DIFF_HEADS * LANE), BF16),
        jax.ShapeDtypeStruct((n, LANE), F32),
        jax.ShapeDtypeStruct((steps, 8, LANE), F32),
    )
    out_specs = (
        tspec(FOX_HEADS, LANE), tspec(FOX_HEADS, FOX_V_ROWS),
        tspec(DIFF_HEADS, LANE), tspec(DIFF_HEADS, LANE), tspec(DIFF_HEADS, DIFF_V_ROWS),
        row(FOX_HEADS * LANE), row(DIFF_HEADS * LANE), row(DIFF_HEADS * LANE), row(LANE),
        pl.BlockSpec((None, 8, LANE), lambda i: (i, 0, 0)),
    )
    return pl.pallas_call(
        functools.partial(_proj_kernel, bm=bm, steps_per_seq=spp),
        grid=(steps,),
        in_specs=[row(D_MODEL), once(g_mix), once(w_t), once(b_t), once(w_row), once(bf_pad)],
        out_specs=out_specs,
        out_shape=out_shape,
        scratch_shapes=[pltpu.VMEM((1, LANE), F32)],
        compiler_params=pltpu.CompilerParams(dimension_semantics=("arbitrary",),
                                             vmem_limit_bytes=VMEM_LIMIT),
        name="proj",
    )(x2, g_mix, w_t, b_t, w_row, bf_pad)


def _softmax_step(s_ref, m_ref, acc_ref, vt, d, tile, fix=None):
    half = tile // 2
    for c0 in (0, half):
        cols = slice(c0, c0 + half)
        s = s_ref[:, cols]
        if fix is not None:
            s = fix(s, c0)
        dc = d[:, cols]
        m_old = m_ref[:, cols]
        m_new = jnp.maximum(m_old, jnp.max(s, axis=0, keepdims=True) + dc)
        p = jnp.exp(s - (m_new - dc))
        acc_ref[:, cols] = (jnp.exp(m_old - m_new) * acc_ref[:, cols]
                            + jnp.dot(vt, p.astype(BF16), preferred_element_type=F32))
        m_ref[:, cols] = m_new


def _tile_loop(i, fill, step, last):
    fill(0, 0)

    def pair(jj, carry):
        j = 2 * jj
        fill(1, j + 1)
        step(0, j)
        fill(0, j + 2)
        step(1, j + 1)
        return carry

    lax.fori_loop(0, i // 2, pair, 0)

    @pl.when(i % 2 == 1)
    def _():
        fill(1, i)
        step(0, i - 1)
        last(1)

    @pl.when(i % 2 == 0)
    def _():
        last(0)


def _fox_kernel(c_ref, qt_ref, k_ref, vt_ref, f_ref, o_ref, sa_ref, sb_ref, m_ref, acc_ref, *, tile, nk):
    b, hh, i = pl.program_id(0), pl.program_id(1), pl.program_id(2)
    qt = qt_ref[...]
    fi = f_ref[...]
    m_ref[...] = jnp.full_like(m_ref, NEG)
    acc_ref[...] = jnp.zeros_like(acc_ref)
    base = (b * FOX_HEADS + hh) * nk
    bufs = (sa_ref, sb_ref)

    def fill(slot, j):
        off = pl.multiple_of(j * tile, tile)
        bufs[slot][...] = jnp.dot(k_ref[pl.ds(off, tile), :], qt, preferred_element_type=F32)

    def step(slot, j):
        _softmax_step(bufs[slot], m_ref, acc_ref, vt_ref[j], fi - c_ref[base + j], tile)

    def last(slot):
        key = lax.broadcasted_iota(jnp.int32, (tile, tile // 2), 0)
        qry = lax.broadcasted_iota(jnp.int32, (tile, tile // 2), 1)
        causal = lambda s, c0: jnp.where(key <= qry + c0, s, NEG)
        _softmax_step(bufs[slot], m_ref, acc_ref, vt_ref[i], fi - c_ref[base + i], tile, causal)

    _tile_loop(i, fill, step, last)

    acc = acc_ref[...]
    o = acc[:FOX_HEAD_DIM] / acc[FOX_HEAD_DIM:FOX_HEAD_DIM + 1]
    o = jnp.concatenate([o, jnp.zeros((LANE - FOX_HEAD_DIM, tile), F32)], axis=0)
    o_ref[...] = o.T.astype(BF16)


def _fox_call(c_flat, qft, kf, vft, f_rows, tile):
    bsz, _, nq, _, _ = qft.shape
    seq = nq * tile
    grid_spec = pltpu.PrefetchScalarGridSpec(
        num_scalar_prefetch=1,
        grid=(bsz, FOX_HEADS, nq),
        in_specs=[
            pl.BlockSpec((None, None, None, LANE, tile), lambda b, h, i, c: (b, h, i, 0, 0)),
            pl.BlockSpec((None, seq, LANE), lambda b, h, i, c: (b, 0, h)),
            pl.BlockSpec((None, None, nq, FOX_V_ROWS, tile), lambda b, h, i, c: (b, h, 0, 0, 0)),
            pl.BlockSpec((None, None, None, 1, tile), lambda b, h, i, c: (b, h, i, 0, 0)),
        ],
        out_specs=pl.BlockSpec((None, tile, LANE), lambda b, h, i, c: (b, i, h)),
        scratch_shapes=[pltpu.VMEM((tile, tile), F32), pltpu.VMEM((tile, tile), F32),
                        pltpu.VMEM((1, tile), F32), pltpu.VMEM((FOX_V_ROWS, tile), F32)],
    )
    return pl.pallas_call(
        functools.partial(_fox_kernel, tile=tile, nk=nq),
        grid_spec=grid_spec,
        out_shape=jax.ShapeDtypeStruct((bsz, seq, FOX_HEADS * LANE), BF16),
        compiler_params=pltpu.CompilerParams(
            dimension_semantics=("parallel", "parallel", "arbitrary"),
            vmem_limit_bytes=VMEM_LIMIT),
        name="fox_attn",
    )(c_flat, qft, kf, vft, f_rows)


def _diff_kernel(sc_ref, q1t_ref, q2t_ref, k1_ref, k2_ref, vt_ref, g_ref, o_ref,
                 s1a_ref, s1b_ref, s2a_ref, s2b_ref, m1_ref, a1_ref, m2_ref, a2_ref, *, tile):
    hh, i = pl.program_id(1), pl.program_id(2)
    slope = sc_ref[hh]
    lam = sc_ref[DIFF_HEADS]
    q1t = q1t_ref[...]
    q2t = q2t_ref[...]
    for m_ref, a_ref in ((m1_ref, a1_ref), (m2_ref, a2_ref)):
        m_ref[...] = jnp.full_like(m_ref, NEG)
        a_ref[...] = jnp.zeros_like(a_ref)
    bufs1 = (s1a_ref, s1b_ref)
    bufs2 = (s2a_ref, s2b_ref)
    qrel = lax.broadcasted_iota(jnp.int32, (1, tile), 1)

    def fill(slot, j):
        off = pl.multiple_of(j * tile, tile)
        bufs1[slot][...] = jnp.dot(k1_ref[pl.ds(off, tile), :], q1t, preferred_element_type=F32)
        bufs2[slot][...] = jnp.dot(k2_ref[pl.ds(off, tile), :], q2t, preferred_element_type=F32)

    def step(slot, j):
        d = slope * ((j - i) * tile - qrel).astype(F32)
        vt = vt_ref[j]
        _softmax_step(bufs1[slot], m1_ref, a1_ref, vt, d, tile)
        _softmax_step(bufs2[slot], m2_ref, a2_ref, vt, d, tile)

    def last(slot):
        key = lax.broadcasted_iota(jnp.int32, (tile, tile // 2), 0)
        qry = lax.broadcasted_iota(jnp.int32, (tile, tile // 2), 1)

        def chunk_bias(s, c0):
            q = qry + c0
            bias = -slope * (key + jnp.abs(q - key)).astype(F32)
            return jnp.where((key // CHUNK) <= (q // CHUNK), s + bias, NEG)

        zero = jnp.zeros((1, tile), F32)
        vt = vt_ref[i]
        _softmax_step(bufs1[slot], m1_ref, a1_ref, vt, zero, tile, chunk_bias)
        _softmax_step(bufs2[slot], m2_ref, a2_ref, vt, zero, tile, chunk_bias)

    _tile_loop(i, fill, step, last)

    a1 = a1_ref[...]
    a2 = a2_ref[...]
    o1 = a1[:DIFF_V_DIM] / a1[DIFF_V_DIM:DIFF_V_DIM + 1]
    o2 = a2[:DIFF_V_DIM] / a2[DIFF_V_DIM:DIFF_V_DIM + 1]
    ob = o1 - lam * o2
    ob = ob * lax.rsqrt(jnp.mean(ob * ob, axis=0, keepdims=True) + EPS) * g_ref[...] * (1.0 - LAM_INIT)
    o_ref[...] = ob.T.astype(BF16)


def _diff_call(scal, q1t, q2t, k1, k2, vdt, g_col, tile):
    bsz, _, nq, _, _ = q1t.shape
    seq = nq * tile
    qspec = pl.BlockSpec((None, None, None, LANE, tile), lambda b, h, i, c: (b, h, i, 0, 0))
    kspec = pl.BlockSpec((None, seq, LANE), lambda b, h, i, c: (b, 0, h))
    score = pltpu.VMEM((tile, tile), F32)
    grid_spec = pltpu.PrefetchScalarGridSpec(
        num_scalar_prefetch=1,
        grid=(bsz, DIFF_HEADS, nq),
        in_specs=[qspec, qspec, kspec, kspec,
                  pl.BlockSpec((None, None, nq, DIFF_V_ROWS, tile), lambda b, h, i, c: (b, h, 0, 0, 0)),
                  pl.BlockSpec((DIFF_V_DIM, 1), lambda b, h, i, c: (0, 0))],
        out_specs=pl.BlockSpec((None, tile, LANE), lambda b, h, i, c: (b, i, h)),
        scratch_shapes=[score, score, score, score,
                        pltpu.VMEM((1, tile), F32), pltpu.VMEM((DIFF_V_ROWS, tile), F32),
                        pltpu.VMEM((1, tile), F32), pltpu.VMEM((DIFF_V_ROWS, tile), F32)],
    )
    return pl.pallas_call(
        functools.partial(_diff_kernel, tile=tile),
        grid_spec=grid_spec,
        out_shape=jax.ShapeDtypeStruct((bsz, seq, DIFF_HEADS * LANE), BF16),
        compiler_params=pltpu.CompilerParams(
            dimension_semantics=("parallel", "parallel", "arbitrary"),
            vmem_limit_bytes=VMEM_LIMIT),
        name="diff_attn",
    )(scal, q1t, q2t, k1, k2, vdt, g_col)


def _merge_kernel(x_ref, oa_ref, ob_ref, gmix_ref, wg_ref, bg_ref, wpa_ref, wpb_ref, wo_ref,
                  gmoe_ref, wrh_ref, wrl_ref, br_ref,
                  x1_ref, h2_ref, mi_ref, mf_ref, cnt_ref, run_ref, *, bm):
    i = pl.program_id(0)

    @pl.when(i == 0)
    def _():
        run_ref[...] = jnp.zeros_like(run_ref)

    x = x_ref[...]
    h = _rms(x, gmix_ref[...]).astype(BF16)
    gates = jnp.dot(h, wg_ref[...], preferred_element_type=F32) + bg_ref[...]
    gates = 1.0 / (1.0 + jnp.exp(-gates))
    ya = jnp.dot(oa_ref[...], wpa_ref[...], preferred_element_type=F32)
    yb = jnp.dot(ob_ref[...], wpb_ref[...], preferred_element_type=F32)
    y = gates[:, :D_MODEL] * ya + gates[:, D_MODEL:] * yb
    x1 = x + jnp.dot(y.astype(BF16), wo_ref[...], preferred_element_type=F32)
    x1_ref[...] = x1
    h2 = _rms(x1, gmoe_ref[...])
    h2_ref[...] = h2

    h2h = h2.astype(BF16)
    h2l = (h2 - h2h.astype(F32)).astype(BF16)
    r = (jnp.dot(h2h, wrh_ref[...], preferred_element_type=F32)
         + jnp.dot(h2l, wrh_ref[...], preferred_element_type=F32)
         + jnp.dot(h2h, wrl_ref[...], preferred_element_type=F32)) + br_ref[...]
    el_all = r[:, :LANE]
    gl = r[:, LANE:]

    lane_i = lax.broadcasted_iota(jnp.int32, (bm, LANE), 1)
    lane = lane_i.astype(F32)
    big = float(LANE)

    def first_argmax(vals):
        vmax = jnp.max(vals, axis=1, keepdims=True)
        idx = jnp.min(jnp.where(vals == vmax, lane, big), axis=1, keepdims=True)
        return vmax, idx

    glm = jnp.where(lane_i < N_GROUPS, gl, NEG)
    gmax, g_idx = first_argmax(glm)
    g_w = 1.0 / jnp.sum(jnp.exp(glm - gmax), axis=1, keepdims=True)

    in_group = ((lane_i >> 3).astype(F32) == g_idx) & (lane_i < N_EXPERTS)
    elm = jnp.where(in_group, el_all, NEG)
    e1, idx1 = first_argmax(elm)
    elm2 = jnp.where(lane == idx1, NEG, elm)
    e2, idx2 = first_argmax(elm2)
    t = jnp.exp(e2 - e1)
    w1 = g_w / (1.0 + t)
    w2 = g_w * t / (1.0 + t)

    oh1 = (lane == idx1).astype(F32)
    oh2 = (lane == idx2).astype(F32)
    oh = oh1 + oh2
    row = lax.broadcasted_iota(jnp.int32, (bm, bm), 0)
    col = lax.broadcasted_iota(jnp.int32, (bm, bm), 1)
    strict = (col < row).astype(BF16)
    before = jnp.dot(strict, oh.astype(BF16), preferred_element_type=F32) + run_ref[...]
    rank1 = jnp.sum(oh1 * before, axis=1, keepdims=True)
    rank2 = jnp.sum(oh2 * before, axis=1, keepdims=True)
    run_ref[...] = run_ref[...] + jnp.sum(oh, axis=0, keepdims=True)
    cnt_ref[...] = jnp.broadcast_to(run_ref[...], cnt_ref.shape)

    meta = jnp.where(lane_i == 0, idx1, jnp.where(lane_i == 1, idx2,
                     jnp.where(lane_i == 2, rank1, jnp.where(lane_i == 3, rank2, 0.0))))
    mi_ref[...] = meta.astype(jnp.int32)
    mf_ref[...] = jnp.where(lane_i == 0, w1, jnp.where(lane_i == 1, w2, 0.0))


def _merge_call(x2, oa, ob, g_mix, w_g, b_g, w_pa, w_pb, w_o, g_moe, wr_hi, wr_lo, b_r):
    n = x2.shape[0]
    bm = MERGE_ROWS
    row = lambda w: pl.BlockSpec((bm, w), lambda i: (i, 0))
    full = lambda a: pl.BlockSpec(a.shape, lambda i: (0,) * a.ndim)
    consts = (g_mix, w_g, b_g, w_pa, w_pb, w_o, g_moe, wr_hi, wr_lo, b_r)
    return pl.pallas_call(
        functools.partial(_merge_kernel, bm=bm),
        grid=(n // bm,),
        in_specs=[row(D_MODEL), row(oa.shape[1]), row(ob.shape[1])] + [full(a) for a in consts],
        out_specs=(row(D_MODEL), row(D_MODEL), row(LANE), row(LANE),
                   pl.BlockSpec((8, LANE), lambda i: (0, 0))),
        out_shape=(jax.ShapeDtypeStruct((n, D_MODEL), F32),
                   jax.ShapeDtypeStruct((n, D_MODEL), F32),
                   jax.ShapeDtypeStruct((n, LANE), jnp.int32),
                   jax.ShapeDtypeStruct((n, LANE), F32),
                   jax.ShapeDtypeStruct((8, LANE), F32)),
        scratch_shapes=[pltpu.VMEM((1, LANE), F32)],
        compiler_params=pltpu.CompilerParams(dimension_semantics=("arbitrary",),
                                             vmem_limit_bytes=VMEM_LIMIT),
        name="merge_router",
    )(x2, oa, ob, *consts)


def _row_copy(src_ref, src_row, dst_ref, dst_row, sem):
    return pltpu.make_async_copy(src_ref.at[pl.ds(src_row, 1), :],
                                 dst_ref.at[pl.ds(dst_row, 1), :], sem)


def _dispatch_kernel(dest_ref, h_ref, xs_in_ref, xs_ref, sem, *, bt):
    del xs_in_ref
    i = pl.program_id(0)

    def issue(r, carry):
        a = (i * bt + r) * 2
        _row_copy(h_ref, r, xs_ref, dest_ref[a], sem).start()
        _row_copy(h_ref, r, xs_ref, dest_ref[a + 1], sem).start()
        return carry

    lax.fori_loop(0, bt, issue, 0)

    def drain(r, carry):
        _row_copy(h_ref, 0, xs_ref, 0, sem).wait()
        _row_copy(h_ref, 0, xs_ref, 0, sem).wait()
        return carry

    lax.fori_loop(0, bt, drain, 0)


def _dispatch_call(dest_flat, h2, p_rows):
    n, d = h2.shape
    bt = MOVE_ROWS
    grid_spec = pltpu.PrefetchScalarGridSpec(
        num_scalar_prefetch=1,
        grid=(n // bt,),
        in_specs=[pl.BlockSpec((bt, d), lambda i, dest: (i, 0)),
                  pl.BlockSpec(memory_space=pl.ANY)],
        out_specs=pl.BlockSpec(memory_space=pl.ANY),
        scratch_shapes=[pltpu.SemaphoreType.DMA(())],
    )
    return pl.pallas_call(
        functools.partial(_dispatch_kernel, bt=bt),
        grid_spec=grid_spec,
        out_shape=jax.ShapeDtypeStruct((p_rows, d), h2.dtype),
        input_output_aliases={2: 0},
        compiler_params=pltpu.CompilerParams(dimension_semantics=("arbitrary",)),
        name="dispatch",
    )(dest_flat, h2, jnp.zeros((p_rows, d), h2.dtype))


def _expert_kernel(blk_e_ref, used_ref, x_ref, w1_ref, w3_ref, w2_ref, y_ref):
    del blk_e_ref
    i = pl.program_id(0)

    @pl.when(i < used_ref[0])
    def _():
        xb = x_ref[...].astype(BF16)
        a = jnp.dot(xb, w1_ref[...], preferred_element_type=F32)
        g = jnp.dot(xb, w3_ref[...], preferred_element_type=F32)
        mid = (a / (1.0 + jnp.exp(-a))) * g
        y_ref[...] = jnp.dot(mid.astype(BF16), w2_ref[...], preferred_element_type=F32)

    @pl.when(i >= used_ref[0])
    def _():
        y_ref[...] = jnp.zeros_like(y_ref)


def _expert_call(blk_e, n_used, xs, w1, w3, w2):
    p_rows, d = xs.shape
    blk = MOE_ROWS
    x_idx = lambda i, be, nu: (jnp.minimum(i, nu[0] - 1), 0)
    grid_spec = pltpu.PrefetchScalarGridSpec(
        num_scalar_prefetch=2,
        grid=(p_rows // blk,),
        in_specs=[pl.BlockSpec((blk, d), x_idx),
                  pl.BlockSpec((None, d, D_EXPERT), lambda i, be, nu: (be[i], 0, 0)),
                  pl.BlockSpec((None, d, D_EXPERT), lambda i, be, nu: (be[i], 0, 0)),
                  pl.BlockSpec((None, D_EXPERT, d), lambda i, be, nu: (be[i], 0, 0))],
        out_specs=pl.BlockSpec((blk, d), lambda i, be, nu: (i, 0)),
    )
    return pl.pallas_call(
        _expert_kernel,
        grid_spec=grid_spec,
        out_shape=jax.ShapeDtypeStruct((p_rows, d), F32),
        compiler_params=pltpu.CompilerParams(dimension_semantics=("arbitrary",),
                                             vmem_limit_bytes=VMEM_LIMIT),
        name="experts",
    )(blk_e, n_used, xs, w1, w3, w2)


def _combine_kernel(dest_ref, x1_ref, mf_ref, g_ref, yb_ref, o_ref, buf0, buf1, sem, *, bt):
    i = pl.program_id(0)

    def issue(r, carry):
        a = (i * bt + r) * 2
        _row_copy(yb_ref, dest_ref[a], buf0, r, sem).start()
        _row_copy(yb_ref, dest_ref[a + 1], buf1, r, sem).start()
        return carry

    lax.fori_loop(0, bt, issue, 0)

    def drain(r, carry):
        _row_copy(yb_ref, 0, buf0, 0, sem).wait()
        _row_copy(yb_ref, 0, buf1, 0, sem).wait()
        return carry

    lax.fori_loop(0, bt, drain, 0)

    mf = mf_ref[...]
    x2 = x1_ref[...] + mf[:, 0:1] * buf0[...] + mf[:, 1:2] * buf1[...]
    o_ref[...] = _rms(x2, g_ref[...])


def _combine_call(dest_flat, x1, mf, g_final, yb):
    n, d = x1.shape
    bt = MOVE_ROWS
    grid_spec = pltpu.PrefetchScalarGridSpec(
        num_scalar_prefetch=1,
        grid=(n // bt,),
        in_specs=[pl.BlockSpec((bt, d), lambda i, dest: (i, 0)),
                  pl.BlockSpec((bt, LANE), lambda i, dest: (i, 0)),
                  pl.BlockSpec((1, d), lambda i, dest: (0, 0)),
                  pl.BlockSpec(memory_space=pl.ANY)],
        out_specs=pl.BlockSpec((bt, d), lambda i, dest: (i, 0)),
        scratch_shapes=[pltpu.VMEM((bt, d), F32), pltpu.VMEM((bt, d), F32),
                        pltpu.SemaphoreType.DMA(())],
    )
    return pl.pallas_call(
        functools.partial(_combine_kernel, bt=bt),
        grid_spec=grid_spec,
        out_shape=jax.ShapeDtypeStruct((n, d), F32),
        compiler_params=pltpu.CompilerParams(dimension_semantics=("arbitrary",)),
        name="combine",
    )(dest_flat, x1, mf, g_final, yb)


def _pad_heads(w, heads, width, padded, scale=1.0):
    d = w.shape[0]
    w = (w * scale).reshape(d, heads, width)
    return jnp.pad(w, ((0, 0), (0, 0), (0, padded - width))).reshape(d, heads * padded)


def _build_proj_params(w_in, b_fgate):
    fw, dw = FOX_HEADS * FOX_HEAD_DIM, DIFF_HEADS * DIFF_V_DIM
    o = 0
    fq, fk, fv = (w_in[:, o + k * fw:o + (k + 1) * fw] for k in range(3))
    o += 3 * fw
    ff = w_in[:, o:o + FOX_HEADS]
    o += FOX_HEADS
    dq, dk, dv = (w_in[:, o + k * dw:o + (k + 1) * dw] for k in range(3))
    d = w_in.shape[0]
    dq = dq.reshape(d, DIFF_HEADS, 2, DIFF_HEAD_DIM)
    dk = dk.reshape(d, DIFF_HEADS, 2, DIFF_HEAD_DIM)
    fscale = FOX_HEAD_DIM ** -0.5
    dscale = DIFF_HEAD_DIM ** -0.5
    half = lambda t, k: t[:, :, k, :].reshape(d, DIFF_HEADS * DIFF_HEAD_DIM)
    w_t = jnp.concatenate([
        _pad_heads(fq, FOX_HEADS, FOX_HEAD_DIM, LANE, fscale),
        _pad_heads(fv, FOX_HEADS, FOX_HEAD_DIM, FOX_V_ROWS),
        _pad_heads(half(dq, 0), DIFF_HEADS, DIFF_HEAD_DIM, LANE, dscale),
        _pad_heads(half(dq, 1), DIFF_HEADS, DIFF_HEAD_DIM, LANE, dscale),
        _pad_heads(dv, DIFF_HEADS, DIFF_V_DIM, DIFF_V_ROWS),
    ], axis=1).T.astype(BF16)
    w_row = jnp.concatenate([
        _pad_heads(fk, FOX_HEADS, FOX_HEAD_DIM, LANE),
        jnp.pad(ff, ((0, 0), (0, LANE - FOX_HEADS))),
        _pad_heads(half(dk, 0), DIFF_HEADS, DIFF_HEAD_DIM, LANE),
        _pad_heads(half(dk, 1), DIFF_HEADS, DIFF_HEAD_DIM, LANE),
    ], axis=1).astype(BF16)

    slopes = [2.0 ** (-8.0 / DIFF_HEADS * (hh + 1)) for hh in range(DIFF_HEADS)]
    b = jnp.zeros((_T_ROWS,), F32)
    for hh in range(FOX_HEADS):
        b = b.at[_TQF + hh * LANE + FOX_HEAD_DIM:_TQF + hh * LANE + FOX_HEAD_DIM + 3].set(-1.0)
        b = b.at[_TVF + hh * FOX_V_ROWS + FOX_HEAD_DIM].set(1.0)
    for hh in range(DIFF_HEADS):
        for base in (_TQ1, _TQ2):
            b = b.at[base + hh * LANE + DIFF_HEAD_DIM:base + hh * LANE + DIFF_HEAD_DIM + 2].set(slopes[hh])
        b = b.at[_TVD + hh * DIFF_V_ROWS + DIFF_V_DIM].set(1.0)
    bf_pad = jnp.pad(b_fgate, (0, LANE - FOX_HEADS))[None, :]
    return w_t, b[:, None], w_row, bf_pad, slopes


def kernel(x, g_mix, w_in, b_fgate, b_gate, lam_q1, lam_k1, lam_q2, lam_k2, g_subln, w_pa, w_pb, w_o,
           g_moe, w_group, b_group, w_expert, b_expert, w1, w3, w2, g_final):
    bsz, seq, d = x.shape
    n = bsz * seq
    tile = ATT_TILE
    nk = seq // tile
    x2 = x.reshape(n, d)

    w_t, b_t, w_row, bf_pad, slopes = _build_proj_params(w_in[0], b_fgate[0])
    qft, vft, q1t, q2t, vdt, kf, k1, k2, fcol, cblk = _proj_call(x2, g_mix, w_t, b_t, w_row, bf_pad, bsz, seq)
    shp = lambda a: a.reshape(bsz, seq, a.shape[1])
    c_flat = cblk[:, 0, :FOX_HEADS].reshape(bsz, nk, FOX_HEADS).transpose(0, 2, 1).reshape(-1)
    f_rows = (fcol[:, :FOX_HEADS].reshape(bsz, nk, tile, FOX_HEADS).transpose(0, 3, 1, 2)
              .reshape(bsz, FOX_HEADS, nk, 1, tile))

    oa = _fox_call(c_flat, qft, shp(kf), vft, f_rows, tile)
    lam = (jnp.exp(jnp.sum(lam_q1[0] * lam_k1[0])) - jnp.exp(jnp.sum(lam_q2[0] * lam_k2[0])) + LAM_INIT)
    scal = jnp.concatenate([jnp.asarray(slopes, F32), lam[None].astype(F32)])
    ob = _diff_call(scal, q1t, q2t, shp(k1), shp(k2), vdt, g_subln[0][:, None], tile)

    o = 3 * FOX_HEADS * FOX_HEAD_DIM + FOX_HEADS + 3 * DIFF_HEADS * DIFF_V_DIM
    w_g = w_in[0][:, o:].astype(BF16)
    w_pa_pad = jnp.pad(w_pa[0].reshape(FOX_HEADS, FOX_HEAD_DIM, d),
                       ((0, 0), (0, LANE - FOX_HEAD_DIM), (0, 0))).reshape(FOX_HEADS * LANE, d).astype(BF16)
    w_r = jnp.zeros((d, 2 * LANE), F32)
    w_r = w_r.at[:, :N_EXPERTS].set(w_expert[0]).at[:, LANE:LANE + N_GROUPS].set(w_group[0])
    wr_hi = w_r.astype(BF16)
    wr_lo = (w_r - wr_hi.astype(F32)).astype(BF16)
    b_r = jnp.zeros((1, 2 * LANE), F32)
    b_r = b_r.at[0, :N_EXPERTS].set(b_expert[0].reshape(-1)).at[0, LANE:LANE + N_GROUPS].set(b_group[0])
    x1, h2, mi, mf, cnt = _merge_call(
        x2, oa.reshape(n, -1), ob.reshape(n, -1), g_mix, w_g, b_gate, w_pa_pad,
        w_pb[0].astype(BF16), w_o[0].astype(BF16), g_moe, wr_hi, wr_lo, b_r)

    blk = MOE_ROWS
    counts = cnt[0, :N_EXPERTS].astype(jnp.int32)
    padded = (counts + blk - 1) // blk * blk
    pend = jnp.cumsum(padded)
    pstart = pend - padded
    dest = (pstart[mi[:, 0:2]] + mi[:, 2:4]).reshape(-1)
    p_rows = n * 2 + N_EXPERTS * blk
    n_blk = p_rows // blk
    blk_pos = jnp.arange(n_blk, dtype=jnp.int32) * blk
    blk_e = jnp.minimum(jnp.sum(pend[None, :] <= blk_pos[:, None], axis=1), N_EXPERTS - 1).astype(jnp.int32)
    n_used = (pend[-1:] // blk).astype(jnp.int32)

    xs = _dispatch_call(dest, h2, p_rows)
    yb = _expert_call(blk_e, n_used, xs, w1[0].astype(BF16), w3[0].astype(BF16), w2[0].astype(BF16))
    out = _combine_call(dest, x1, mf, g_final[None, :], yb)
    return out.reshape(bsz, seq, d)
```

```python
import functools

import jax
import jax.numpy as jnp
from jax import lax
from jax.experimental import pallas as pl
from jax.experimental.pallas import tpu as pltpu

F32 = jnp.float32
BF16 = jnp.bfloat16

D_MODEL = 1024
FOX_HEADS = 8
FOX_HEAD_DIM = 64
DIFF_HEADS = 4
DIFF_HEAD_DIM = 64
DIFF_V_DIM = 128
CHUNK = 64
N_GROUPS = 4
EXPERTS_PER_GROUP = 8
N_EXPERTS = 32
D_EXPERT = 512
EPS = 1e-6
LAM_INIT = 0.8 - 0.6 * 1.0

LANE = 128
NEG = -1e30
LOG2E = 1.4426950408889634
ATT_TILE = 512
MERGE_ROWS = 256
MOE_ROWS = 256
MOVE_ROWS = 256
VMEM_LIMIT = 56 * 1024 * 1024

FOX_V_ROWS = 80
DIFF_V_ROWS = 144

_TQF, _TVF, _TQ1 = 0, 1024, 1024 + FOX_HEADS * FOX_V_ROWS
_TQ2, _TVD = _TQ1 + 512, _TQ1 + 1024
_T_ROWS = _TVD + DIFF_HEADS * DIFF_V_ROWS
_KF, _FF, _K1, _K2, _ROW_COLS = 0, 1024, 1152, 1664, 2176


def _rms(x, g):
    return x * lax.rsqrt(jnp.mean(x * x, axis=-1, keepdims=True) + EPS) * g


def _split3(r):
    r0 = r.astype(BF16).astype(F32)
    r1 = (r - r0).astype(BF16).astype(F32)
    r2 = (r - r0 - r1).astype(BF16).astype(F32)
    return r0, r1, r2


_NT = (((1,), (1,)), ((), ()))


def _proj_kernel(x_ref, g_ref, wt_ref, bt_ref, w_ref, bf_ref,
                 qft_ref, vft_ref, q1t_ref, q2t_ref, vdt_ref, kf_ref, k1_ref, k2_ref,
                 fcol_ref, cblk_ref, carry_ref, *, bm, steps_per_seq):
    i = pl.program_id(0)
    h = _rms(x_ref[...], g_ref[...]).astype(BF16)

    zt = lax.dot_general(wt_ref[...], h, _NT, preferred_element_type=F32) + bt_ref[...]

    def heads(lo, n_heads, rows):
        return zt[lo:lo + n_heads * rows].reshape(n_heads, rows, bm).astype(BF16)

    qft_ref[...] = heads(_TQF, FOX_HEADS, LANE)
    vft_ref[...] = heads(_TVF, FOX_HEADS, FOX_V_ROWS)
    q1t_ref[...] = heads(_TQ1, DIFF_HEADS, LANE)
    q2t_ref[...] = heads(_TQ2, DIFF_HEADS, LANE)
    vdt_ref[...] = heads(_TVD, DIFF_HEADS, DIFF_V_ROWS)

    z = jnp.dot(h, w_ref[...], preferred_element_type=F32)

    zf = z[:, _FF:_K1] + bf_ref[...]
    logf = jnp.minimum(zf, 0.0) - jnp.log1p(jnp.exp(-jnp.abs(zf)))

    @pl.when(i % steps_per_seq == 0)
    def _():
        carry_ref[...] = jnp.zeros_like(carry_ref)

    c = carry_ref[...]
    row = lax.broadcasted_iota(jnp.int32, (bm, bm), 0)
    col = lax.broadcasted_iota(jnp.int32, (bm, bm), 1)
    tri = (col <= row).astype(BF16)
    l0, l1, l2 = _split3(logf)
    rel = (jnp.dot(tri, l0.astype(BF16), preferred_element_type=F32)
           + jnp.dot(tri, l1.astype(BF16), preferred_element_type=F32)
           + jnp.dot(tri, l2.astype(BF16), preferred_element_type=F32))
    fcum = rel + c
    carry_ref[...] = fcum[bm - 1:bm, :]
    fcol_ref[...] = fcum
    cblk_ref[...] = jnp.broadcast_to(c, cblk_ref.shape)

    lane = lax.broadcasted_iota(jnp.int32, (bm, LANE), 1)
    rel2 = rel * LOG2E
    for hh in range(FOX_HEADS):
        r0, r1, r2 = _split3(rel2[:, hh:hh + 1])
        aug = jnp.where(lane == FOX_HEAD_DIM, r0,
                        jnp.where(lane == FOX_HEAD_DIM + 1, r1,
                                  jnp.where(lane == FOX_HEAD_DIM + 2, r2, 0.0)))
        lo = _KF + hh * LANE
        kf_ref[:, hh * LANE:(hh + 1) * LANE] = (z[:, lo:lo + LANE] + aug).astype(BF16)

    width = DIFF_HEADS * LANE
    jrel = lax.broadcasted_iota(jnp.int32, (bm, width), 0)
    lane4 = lax.broadcasted_iota(jnp.int32, (bm, width), 1) & (LANE - 1)
    j0, j1, j2 = _split3(jrel.astype(F32) * LOG2E)
    augk = jnp.where(lane4 == DIFF_HEAD_DIM, j0,
                     jnp.where(lane4 == DIFF_HEAD_DIM + 1, j1,
                               jnp.where(lane4 == DIFF_HEAD_DIM + 2, j2, 0.0)))
    k1_ref[...] = (z[:, _K1:_K2] + augk).astype(BF16)
    k2_ref[...] = (z[:, _K2:_ROW_COLS] + augk).astype(BF16)


def _proj_call(x2, g_mix, w_t, b_t, w_row, bf_pad, bsz, seq):
    n = x2.shape[0]
    bm = ATT_TILE
    steps = n // bm
    spp = seq // bm
    row = lambda w: pl.BlockSpec((bm, w), lambda i: (i, 0))
    once = lambda a: pl.BlockSpec(a.shape, lambda i: (0,) * a.ndim, pipeline_mode=pl.Buffered(1))
    tshape = lambda heads, rows: jax.ShapeDtypeStruct((bsz, heads, spp, rows, bm), BF16)
    tspec = lambda heads, rows: pl.BlockSpec((None, heads, None, rows, bm),
                                             lambda i: (i // spp, 0, i % spp, 0, 0))
    out_shape = (
        tshape(FOX_HEADS, LANE), tshape(FOX_HEADS, FOX_V_ROWS),
        tshape(DIFF_HEADS, LANE), tshape(DIFF_HEADS, LANE), tshape(DIFF_HEADS, DIFF_V_ROWS),
        jax.ShapeDtypeStruct((n, FOX_HEADS * LANE), BF16),
        jax.ShapeDtypeStruct((n, DIFF_HEADS * LANE), BF16),
        jax.ShapeDtypeStruct((n, DIFF_HEADS * LANE), BF16),
        jax.ShapeDtypeStruct((n, LANE), F32),
        jax.ShapeDtypeStruct((steps, 8, LANE), F32),
    )
    out_specs = (
        tspec(FOX_HEADS, LANE), tspec(FOX_HEADS, FOX_V_ROWS),
        tspec(DIFF_HEADS, LANE), tspec(DIFF_HEADS, LANE), tspec(DIFF_HEADS, DIFF_V_ROWS),
        row(FOX_HEADS * LANE), row(DIFF_HEADS * LANE), row(DIFF_HEADS * LANE), row(LANE),
        pl.BlockSpec((None, 8, LANE), lambda i: (i, 0, 0)),
    )
    return pl.pallas_call(
        functools.partial(_proj_kernel, bm=bm, steps_per_seq=spp),
        grid=(steps,),
        in_specs=[row(D_MODEL), once(g_mix), once(w_t), once(b_t), once(w_row), once(bf_pad)],
        out_specs=out_specs,
        out_shape=out_shape,
        scratch_shapes=[pltpu.VMEM((1, LANE), F32)],
        compiler_params=pltpu.CompilerParams(dimension_semantics=("arbitrary",),
                                             vmem_limit_bytes=VMEM_LIMIT),
        name="proj",
    )(x2, g_mix, w_t, b_t, w_row, bf_pad)


def _softmax_step(s_ref, m_ref, acc_ref, vt, d, tile, fix=None):
    half = tile // 2
    for c0 in (0, half):
        cols = slice(c0, c0 + half)
        s = s_ref[:, cols]
        if fix is not None:
            s = fix(s, c0)
        dc = d[:, cols]
        m_old = m_ref[:, cols]
        m_new = jnp.maximum(m_old, jnp.max(s, axis=0, keepdims=True) + dc)
        p = jnp.exp2(s - (m_new - dc))
        acc_ref[:, cols] = (jnp.exp2(m_old - m_new) * acc_ref[:, cols]
                            + jnp.dot(vt, p.astype(BF16), preferred_element_type=F32))
        m_ref[:, cols] = m_new


def _tile_loop(i, fill, step, last):
    fill(0, 0)

    def pair(jj, carry):
        j = 2 * jj
        fill(1, j + 1)
        step(0, j)
        fill(0, j + 2)
        step(1, j + 1)
        return carry

    lax.fori_loop(0, i // 2, pair, 0)

    @pl.when(i % 2 == 1)
    def _():
        fill(1, i)
        step(0, i - 1)
        last(1)

    @pl.when(i % 2 == 0)
    def _():
        last(0)


def _fox_kernel(c_ref, qt_ref, k_ref, vt_ref, f_ref, o_ref, sa_ref, sb_ref, m_ref, acc_ref, *, tile, nk):
    b, hh, i = pl.program_id(0), pl.program_id(1), pl.program_id(2)
    qt = qt_ref[...]
    fi = f_ref[...] * LOG2E
    m_ref[...] = jnp.full_like(m_ref, NEG)
    acc_ref[...] = jnp.zeros_like(acc_ref)
    base = (b * FOX_HEADS + hh) * nk
    bufs = (sa_ref, sb_ref)

    def fill(slot, j):
        off = pl.multiple_of(j * tile, tile)
        bufs[slot][...] = jnp.dot(k_ref[pl.ds(off, tile), :], qt, preferred_element_type=F32)

    def step(slot, j):
        _softmax_step(bufs[slot], m_ref, acc_ref, vt_ref[j], fi - c_ref[base + j] * LOG2E, tile)

    def last(slot):
        key = lax.broadcasted_iota(jnp.int32, (tile, tile // 2), 0)
        qry = lax.broadcasted_iota(jnp.int32, (tile, tile // 2), 1)
        causal = lambda s, c0: jnp.where(key <= qry + c0, s, NEG)
        _softmax_step(bufs[slot], m_ref, acc_ref, vt_ref[i], fi - c_ref[base + i] * LOG2E, tile, causal)

    _tile_loop(i, fill, step, last)

    acc = acc_ref[...]
    o = acc[:FOX_HEAD_DIM] / acc[FOX_HEAD_DIM:FOX_HEAD_DIM + 1]
    o = jnp.concatenate([o, jnp.zeros((LANE - FOX_HEAD_DIM, tile), F32)], axis=0)
    o_ref[...] = o.T.astype(BF16)


def _fox_call(c_flat, qft, kf, vft, f_rows, tile):
    bsz, _, nq, _, _ = qft.shape
    seq = nq * tile
    grid_spec = pltpu.PrefetchScalarGridSpec(
        num_scalar_prefetch=1,
        grid=(bsz, FOX_HEADS, nq),
        in_specs=[
            pl.BlockSpec((None, None, None, LANE, tile), lambda b, h, i, c: (b, h, i, 0, 0)),
            pl.BlockSpec((None, seq, LANE), lambda b, h, i, c: (b, 0, h)),
            pl.BlockSpec((None, None, nq, FOX_V_ROWS, tile), lambda b, h, i, c: (b, h, 0, 0, 0)),
            pl.BlockSpec((None, None, None, 1, tile), lambda b, h, i, c: (b, h, i, 0, 0)),
        ],
        out_specs=pl.BlockSpec((None, tile, LANE), lambda b, h, i, c: (b, i, h)),
        scratch_shapes=[pltpu.VMEM((tile, tile), F32), pltpu.VMEM((tile, tile), F32),
                        pltpu.VMEM((1, tile), F32), pltpu.VMEM((FOX_V_ROWS, tile), F32)],
    )
    return pl.pallas_call(
        functools.partial(_fox_kernel, tile=tile, nk=nq),
        grid_spec=grid_spec,
        out_shape=jax.ShapeDtypeStruct((bsz, seq, FOX_HEADS * LANE), BF16),
        compiler_params=pltpu.CompilerParams(
            dimension_semantics=("parallel", "parallel", "arbitrary"),
            vmem_limit_bytes=VMEM_LIMIT),
        name="fox_attn",
    )(c_flat, qft, kf, vft, f_rows)


def _diff_kernel(sc_ref, q1t_ref, q2t_ref, k1_ref, k2_ref, vt_ref, g_ref, o_ref,
                 s1a_ref, s1b_ref, s2a_ref, s2b_ref, m1_ref, a1_ref, m2_ref, a2_ref, *, tile):
    hh, i = pl.program_id(1), pl.program_id(2)
    slope = sc_ref[hh] * LOG2E
    lam = sc_ref[DIFF_HEADS]
    q1t = q1t_ref[...]
    q2t = q2t_ref[...]
    for m_ref, a_ref in ((m1_ref, a1_ref), (m2_ref, a2_ref)):
        m_ref[...] = jnp.full_like(m_ref, NEG)
        a_ref[...] = jnp.zeros_like(a_ref)
    bufs1 = (s1a_ref, s1b_ref)
    bufs2 = (s2a_ref, s2b_ref)
    qrel = lax.broadcasted_iota(jnp.int32, (1, tile), 1)

    def fill(slot, j):
        off = pl.multiple_of(j * tile, tile)
        bufs1[slot][...] = jnp.dot(k1_ref[pl.ds(off, tile), :], q1t, preferred_element_type=F32)
        bufs2[slot][...] = jnp.dot(k2_ref[pl.ds(off, tile), :], q2t, preferred_element_type=F32)

    def step(slot, j):
        d = slope * ((j - i) * tile - qrel).astype(F32)
        vt = vt_ref[j]
        _softmax_step(bufs1[slot], m1_ref, a1_ref, vt, d, tile)
        _softmax_step(bufs2[slot], m2_ref, a2_ref, vt, d, tile)

    def last(slot):
        key = lax.broadcasted_iota(jnp.int32, (tile, tile // 2), 0)
        qry = lax.broadcasted_iota(jnp.int32, (tile, tile // 2), 1)

        def chunk_bias(s, c0):
            q = qry + c0
            bias = -slope * (key + jnp.abs(q - key)).astype(F32)
            return jnp.where((key // CHUNK) <= (q // CHUNK), s + bias, NEG)

        zero = jnp.zeros((1, tile), F32)
        vt = vt_ref[i]
        _softmax_step(bufs1[slot], m1_ref, a1_ref, vt, zero, tile, chunk_bias)
        _softmax_step(bufs2[slot], m2_ref, a2_ref, vt, zero, tile, chunk_bias)

    _tile_loop(i, fill, step, last)

    a1 = a1_ref[...]
    a2 = a2_ref[...]
    o1 = a1[:DIFF_V_DIM] / a1[DIFF_V_DIM:DIFF_V_DIM + 1]
    o2 = a2[:DIFF_V_DIM] / a2[DIFF_V_DIM:DIFF_V_DIM + 1]
    ob = o1 - lam * o2
    ob = ob * lax.rsqrt(jnp.mean(ob * ob, axis=0, keepdims=True) + EPS) * g_ref[...] * (1.0 - LAM_INIT)
    o_ref[...] = ob.T.astype(BF16)


def _diff_call(scal, q1t, q2t, k1, k2, vdt, g_col, tile):
    bsz, _, nq, _, _ = q1t.shape
    seq = nq * tile
    qspec = pl.BlockSpec((None, None, None, LANE, tile), lambda b, h, i, c: (b, h, i, 0, 0))
    kspec = pl.BlockSpec((None, seq, LANE), lambda b, h, i, c: (b, 0, h))
    score = pltpu.VMEM((tile, tile), F32)
    grid_spec = pltpu.PrefetchScalarGridSpec(
        num_scalar_prefetch=1,
        grid=(bsz, DIFF_HEADS, nq),
        in_specs=[qspec, qspec, kspec, kspec,
                  pl.BlockSpec((None, None, nq, DIFF_V_ROWS, tile), lambda b, h, i, c: (b, h, 0, 0, 0)),
                  pl.BlockSpec((DIFF_V_DIM, 1), lambda b, h, i, c: (0, 0))],
        out_specs=pl.BlockSpec((None, tile, LANE), lambda b, h, i, c: (b, i, h)),
        scratch_shapes=[score, score, score, score,
                        pltpu.VMEM((1, tile), F32), pltpu.VMEM((DIFF_V_ROWS, tile), F32),
                        pltpu.VMEM((1, tile), F32), pltpu.VMEM((DIFF_V_ROWS, tile), F32)],
    )
    return pl.pallas_call(
        functools.partial(_diff_kernel, tile=tile),
        grid_spec=grid_spec,
        out_shape=jax.ShapeDtypeStruct((bsz, seq, DIFF_HEADS * LANE), BF16),
        compiler_params=pltpu.CompilerParams(
            dimension_semantics=("parallel", "parallel", "arbitrary"),
            vmem_limit_bytes=VMEM_LIMIT),
        name="diff_attn",
    )(scal, q1t, q2t, k1, k2, vdt, g_col)


def _merge_kernel(x_ref, oa_ref, ob_ref, gmix_ref, wg_ref, bg_ref, wpa_ref, wpb_ref, wo_ref,
                  gmoe_ref, wrh_ref, wrl_ref, br_ref,
                  x1_ref, h2_ref, mi_ref, mf_ref, cnt_ref, run_ref, *, bm):
    i = pl.program_id(0)

    @pl.when(i == 0)
    def _():
        run_ref[...] = jnp.zeros_like(run_ref)

    x = x_ref[...]
    h = _rms(x, gmix_ref[...]).astype(BF16)
    gates = jnp.dot(h, wg_ref[...], preferred_element_type=F32) + bg_ref[...]
    gates = 1.0 / (1.0 + jnp.exp(-gates))
    ya = jnp.dot(oa_ref[...], wpa_ref[...], preferred_element_type=F32)
    yb = jnp.dot(ob_ref[...], wpb_ref[...], preferred_element_type=F32)
    y = gates[:, :D_MODEL] * ya + gates[:, D_MODEL:] * yb
    x1 = x + jnp.dot(y.astype(BF16), wo_ref[...], preferred_element_type=F32)
    x1_ref[...] = x1
    h2 = _rms(x1, gmoe_ref[...])
    h2_ref[...] = h2

    h2h = h2.astype(BF16)
    h2l = (h2 - h2h.astype(F32)).astype(BF16)
    r = (jnp.dot(h2h, wrh_ref[...], preferred_element_type=F32)
         + jnp.dot(h2l, wrh_ref[...], preferred_element_type=F32)
         + jnp.dot(h2h, wrl_ref[...], preferred_element_type=F32)) + br_ref[...]
    el_all = r[:, :LANE]
    gl = r[:, LANE:]

    lane_i = lax.broadcasted_iota(jnp.int32, (bm, LANE), 1)
    lane = lane_i.astype(F32)
    big = float(LANE)

    def first_argmax(vals):
        vmax = jnp.max(vals, axis=1, keepdims=True)
        idx = jnp.min(jnp.where(vals == vmax, lane, big), axis=1, keepdims=True)
        return vmax, idx

    glm = jnp.where(lane_i < N_GROUPS, gl, NEG)
    gmax, g_idx = first_argmax(glm)
    g_w = 1.0 / jnp.sum(jnp.exp(glm - gmax), axis=1, keepdims=True)

    in_group = ((lane_i >> 3).astype(F32) == g_idx) & (lane_i < N_EXPERTS)
    elm = jnp.where(in_group, el_all, NEG)
    e1, idx1 = first_argmax(elm)
    elm2 = jnp.where(lane == idx1, NEG, elm)
    e2, idx2 = first_argmax(elm2)
    t = jnp.exp(e2 - e1)
    w1 = g_w / (1.0 + t)
    w2 = g_w * t / (1.0 + t)

    oh1 = (lane == idx1).astype(F32)
    oh2 = (lane == idx2).astype(F32)
    oh = oh1 + oh2
    row = lax.broadcasted_iota(jnp.int32, (bm, bm), 0)
    col = lax.broadcasted_iota(jnp.int32, (bm, bm), 1)
    strict = (col < row).astype(BF16)
    before = jnp.dot(strict, oh.astype(BF16), preferred_element_type=F32) + run_ref[...]
    rank1 = jnp.sum(oh1 * before, axis=1, keepdims=True)
    rank2 = jnp.sum(oh2 * before, axis=1, keepdims=True)
    run_ref[...] = run_ref[...] + jnp.sum(oh, axis=0, keepdims=True)
    cnt_ref[...] = jnp.broadcast_to(run_ref[...], cnt_ref.shape)

    meta = jnp.where(lane_i == 0, idx1, jnp.where(lane_i == 1, idx2,
                     jnp.where(lane_i == 2, rank1, jnp.where(lane_i == 3, rank2, 0.0))))
    mi_ref[...] = meta.astype(jnp.int32)
    mf_ref[...] = jnp.where(lane_i == 0, w1, jnp.where(lane_i == 1, w2, 0.0))


def _merge_call(x2, oa, ob, g_mix, w_g, b_g, w_pa, w_pb, w_o, g_moe, wr_hi, wr_lo, b_r):
    n = x2.shape[0]
    bm = MERGE_ROWS
    row = lambda w: pl.BlockSpec((bm, w), lambda i: (i, 0))
    full = lambda a: pl.BlockSpec(a.shape, lambda i: (0,) * a.ndim)
    consts = (g_mix, w_g, b_g, w_pa, w_pb, w_o, g_moe, wr_hi, wr_lo, b_r)
    return pl.pallas_call(
        functools.partial(_merge_kernel, bm=bm),
        grid=(n // bm,),
        in_specs=[row(D_MODEL), row(oa.shape[1]), row(ob.shape[1])] + [full(a) for a in consts],
        out_specs=(row(D_MODEL), row(D_MODEL), row(LANE), row(LANE),
                   pl.BlockSpec((8, LANE), lambda i: (0, 0))),
        out_shape=(jax.ShapeDtypeStruct((n, D_MODEL), F32),
                   jax.ShapeDtypeStruct((n, D_MODEL), F32),
                   jax.ShapeDtypeStruct((n, LANE), jnp.int32),
                   jax.ShapeDtypeStruct((n, LANE), F32),
                   jax.ShapeDtypeStruct((8, LANE), F32)),
        scratch_shapes=[pltpu.VMEM((1, LANE), F32)],
        compiler_params=pltpu.CompilerParams(dimension_semantics=("arbitrary",),
                                             vmem_limit_bytes=VMEM_LIMIT),
        name="merge_router",
    )(x2, oa, ob, *consts)


def _row_copy(src_ref, src_row, dst_ref, dst_row, sem):
    return pltpu.make_async_copy(src_ref.at[pl.ds(src_row, 1), :],
                                 dst_ref.at[pl.ds(dst_row, 1), :], sem)


def _dispatch_kernel(d0_ref, d1_ref, h_ref, xs_in_ref, xs_ref, sem, *, bt):
    del xs_in_ref
    i = pl.program_id(0)

    def issue(r, carry):
        t = i * bt + r
        _row_copy(h_ref, r, xs_ref, d0_ref[t], sem).start()
        _row_copy(h_ref, r, xs_ref, d1_ref[t], sem).start()
        return carry

    lax.fori_loop(0, bt, issue, 0, unroll=8)

    for _ in range(2):
        pltpu.make_async_copy(h_ref, xs_ref.at[pl.ds(0, bt), :], sem).wait()


def _dispatch_call(dest0, dest1, h2, p_rows):
    n, d = h2.shape
    bt = MOVE_ROWS
    grid_spec = pltpu.PrefetchScalarGridSpec(
        num_scalar_prefetch=2,
        grid=(n // bt,),
        in_specs=[pl.BlockSpec((bt, d), lambda i, d0, d1: (i, 0)),
                  pl.BlockSpec(memory_space=pl.ANY)],
        out_specs=pl.BlockSpec(memory_space=pl.ANY),
        scratch_shapes=[pltpu.SemaphoreType.DMA(())],
    )
    return pl.pallas_call(
        functools.partial(_dispatch_kernel, bt=bt),
        grid_spec=grid_spec,
        out_shape=jax.ShapeDtypeStruct((p_rows, d), h2.dtype),
        input_output_aliases={3: 0},
        compiler_params=pltpu.CompilerParams(dimension_semantics=("arbitrary",)),
        name="dispatch",
    )(dest0, dest1, h2, jnp.zeros((p_rows, d), h2.dtype))


def _expert_kernel(blk_e_ref, used_ref, x_ref, w1_ref, w3_ref, w2_ref, y_ref, w1b, w3b, w2b):
    i = pl.program_id(0)
    live = i < used_ref[0]
    fresh = (i == 0) | (blk_e_ref[i] != blk_e_ref[jnp.maximum(i - 1, 0)])

    @pl.when(live & fresh)
    def _():
        w1b[...] = w1_ref[...].astype(BF16)
        w3b[...] = w3_ref[...].astype(BF16)
        w2b[...] = w2_ref[...].astype(BF16)

    @pl.when(live)
    def _():
        xb = x_ref[...].astype(BF16)
        a = jnp.dot(xb, w1b[...], preferred_element_type=F32)
        g = jnp.dot(xb, w3b[...], preferred_element_type=F32)
        mid = (a / (1.0 + jnp.exp(-a))) * g
        y_ref[...] = jnp.dot(mid.astype(BF16), w2b[...], preferred_element_type=F32)

    @pl.when(i >= used_ref[0])
    def _():
        y_ref[...] = jnp.zeros_like(y_ref)


def _expert_call(blk_e, n_used, xs, w1, w3, w2):
    p_rows, d = xs.shape
    blk = MOE_ROWS
    x_idx = lambda i, be, nu: (jnp.minimum(i, nu[0] - 1), 0)
    grid_spec = pltpu.PrefetchScalarGridSpec(
        num_scalar_prefetch=2,
        grid=(p_rows // blk,),
        in_specs=[pl.BlockSpec((blk, d), x_idx),
                  pl.BlockSpec((None, d, D_EXPERT), lambda i, be, nu: (be[i], 0, 0)),
                  pl.BlockSpec((None, d, D_EXPERT), lambda i, be, nu: (be[i], 0, 0)),
                  pl.BlockSpec((None, D_EXPERT, d), lambda i, be, nu: (be[i], 0, 0))],
        out_specs=pl.BlockSpec((blk, d), lambda i, be, nu: (i, 0)),
        scratch_shapes=[pltpu.VMEM((d, D_EXPERT), BF16), pltpu.VMEM((d, D_EXPERT), BF16),
                        pltpu.VMEM((D_EXPERT, d), BF16)],
    )
    return pl.pallas_call(
        _expert_kernel,
        grid_spec=grid_spec,
        out_shape=jax.ShapeDtypeStruct((p_rows, d), F32),
        compiler_params=pltpu.CompilerParams(dimension_semantics=("arbitrary",),
                                             vmem_limit_bytes=VMEM_LIMIT),
        name="experts",
    )(blk_e, n_used, xs, w1, w3, w2)


def _combine_kernel(d0_ref, d1_ref, x1_ref, mf_ref, g_ref, yb_ref, o_ref, buf0, buf1, sem, *, bt):
    i = pl.program_id(0)

    def issue(r, carry):
        t = i * bt + r
        _row_copy(yb_ref, d0_ref[t], buf0, r, sem).start()
        _row_copy(yb_ref, d1_ref[t], buf1, r, sem).start()
        return carry

    lax.fori_loop(0, bt, issue, 0, unroll=8)

    for buf in (buf0, buf1):
        pltpu.make_async_copy(yb_ref.at[pl.ds(0, bt), :], buf, sem).wait()

    mf = mf_ref[...]
    x2 = x1_ref[...] + mf[:, 0:1] * buf0[...] + mf[:, 1:2] * buf1[...]
    o_ref[...] = _rms(x2, g_ref[...])


def _combine_call(dest0, dest1, x1, mf, g_final, yb):
    n, d = x1.shape
    bt = MOVE_ROWS
    grid_spec = pltpu.PrefetchScalarGridSpec(
        num_scalar_prefetch=2,
        grid=(n // bt,),
        in_specs=[pl.BlockSpec((bt, d), lambda i, d0, d1: (i, 0)),
                  pl.BlockSpec((bt, LANE), lambda i, d0, d1: (i, 0)),
                  pl.BlockSpec((1, d), lambda i, d0, d1: (0, 0)),
                  pl.BlockSpec(memory_space=pl.ANY)],
        out_specs=pl.BlockSpec((bt, d), lambda i, d0, d1: (i, 0)),
        scratch_shapes=[pltpu.VMEM((bt, d), F32), pltpu.VMEM((bt, d), F32),
                        pltpu.SemaphoreType.DMA(())],
    )
    return pl.pallas_call(
        functools.partial(_combine_kernel, bt=bt),
        grid_spec=grid_spec,
        out_shape=jax.ShapeDtypeStruct((n, d), F32),
        compiler_params=pltpu.CompilerParams(dimension_semantics=("arbitrary",)),
        name="combine",
    )(dest0, dest1, x1, mf, g_final, yb)


def _pad_heads(w, heads, width, padded, scale=1.0):
    d = w.shape[0]
    w = (w * scale).reshape(d, heads, width)
    return jnp.pad(w, ((0, 0), (0, 0), (0, padded - width))).reshape(d, heads * padded)


def _build_proj_params(w_in, b_fgate):
    fw, dw = FOX_HEADS * FOX_HEAD_DIM, DIFF_HEADS * DIFF_V_DIM
    o = 0
    fq, fk, fv = (w_in[:, o + k * fw:o + (k + 1) * fw] for k in range(3))
    o += 3 * fw
    ff = w_in[:, o:o + FOX_HEADS]
    o += FOX_HEADS
    dq, dk, dv = (w_in[:, o + k * dw:o + (k + 1) * dw] for k in range(3))
    d = w_in.shape[0]
    dq = dq.reshape(d, DIFF_HEADS, 2, DIFF_HEAD_DIM)
    dk = dk.reshape(d, DIFF_HEADS, 2, DIFF_HEAD_DIM)
    fscale = FOX_HEAD_DIM ** -0.5 * LOG2E
    dscale = DIFF_HEAD_DIM ** -0.5 * LOG2E
    half = lambda t, k: t[:, :, k, :].reshape(d, DIFF_HEADS * DIFF_HEAD_DIM)
    w_t = jnp.concatenate([
        _pad_heads(fq, FOX_HEADS, FOX_HEAD_DIM, LANE, fscale),
        _pad_heads(fv, FOX_HEADS, FOX_HEAD_DIM, FOX_V_ROWS),
        _pad_heads(half(dq, 0), DIFF_HEADS, DIFF_HEAD_DIM, LANE, dscale),
        _pad_heads(half(dq, 1), DIFF_HEADS, DIFF_HEAD_DIM, LANE, dscale),
        _pad_heads(dv, DIFF_HEADS, DIFF_V_DIM, DIFF_V_ROWS),
    ], axis=1).T.astype(BF16)
    w_row = jnp.concatenate([
        _pad_heads(fk, FOX_HEADS, FOX_HEAD_DIM, LANE),
        jnp.pad(ff, ((0, 0), (0, LANE - FOX_HEADS))),
        _pad_heads(half(dk, 0), DIFF_HEADS, DIFF_HEAD_DIM, LANE),
        _pad_heads(half(dk, 1), DIFF_HEADS, DIFF_HEAD_DIM, LANE),
    ], axis=1).astype(BF16)

    slopes = [2.0 ** (-8.0 / DIFF_HEADS * (hh + 1)) for hh in range(DIFF_HEADS)]
    b = jnp.zeros((_T_ROWS,), F32)
    for hh in range(FOX_HEADS):
        b = b.at[_TQF + hh * LANE + FOX_HEAD_DIM:_TQF + hh * LANE + FOX_HEAD_DIM + 3].set(-1.0)
        b = b.at[_TVF + hh * FOX_V_ROWS + FOX_HEAD_DIM].set(1.0)
    for hh in range(DIFF_HEADS):
        for base in (_TQ1, _TQ2):
            b = b.at[base + hh * LANE + DIFF_HEAD_DIM:base + hh * LANE + DIFF_HEAD_DIM + 3].set(slopes[hh])
        b = b.at[_TVD + hh * DIFF_V_ROWS + DIFF_V_DIM].set(1.0)
    bf_pad = jnp.pad(b_fgate, (0, LANE - FOX_HEADS))[None, :]
    return w_t, b[:, None], w_row, bf_pad, slopes


def kernel(x, g_mix, w_in, b_fgate, b_gate, lam_q1, lam_k1, lam_q2, lam_k2, g_subln, w_pa, w_pb, w_o,
           g_moe, w_group, b_group, w_expert, b_expert, w1, w3, w2, g_final):
    bsz, seq, d = x.shape
    n = bsz * seq
    tile = ATT_TILE
    nk = seq // tile
    x2 = x.reshape(n, d)

    w_t, b_t, w_row, bf_pad, slopes = _build_proj_params(w_in[0], b_fgate[0])
    qft, vft, q1t, q2t, vdt, kf, k1, k2, fcol, cblk = _proj_call(x2, g_mix, w_t, b_t, w_row, bf_pad, bsz, seq)
    shp = lambda a: a.reshape(bsz, seq, a.shape[1])
    c_flat = cblk[:, 0, :FOX_HEADS].reshape(bsz, nk, FOX_HEADS).transpose(0, 2, 1).reshape(-1)
    f_rows = (fcol[:, :FOX_HEADS].reshape(bsz, nk, tile, FOX_HEADS).transpose(0, 3, 1, 2)
              .reshape(bsz, FOX_HEADS, nk, 1, tile))

    oa = _fox_call(c_flat, qft, shp(kf), vft, f_rows, tile)
    lam = (jnp.exp(jnp.sum(lam_q1[0] * lam_k1[0])) - jnp.exp(jnp.sum(lam_q2[0] * lam_k2[0])) + LAM_INIT)
    scal = jnp.concatenate([jnp.asarray(slopes, F32), lam[None].astype(F32)])
    ob = _diff_call(scal, q1t, q2t, shp(k1), shp(k2), vdt, g_subln[0][:, None], tile)

    o = 3 * FOX_HEADS * FOX_HEAD_DIM + FOX_HEADS + 3 * DIFF_HEADS * DIFF_V_DIM
    w_g = w_in[0][:, o:].astype(BF16)
    w_pa_pad = jnp.pad(w_pa[0].reshape(FOX_HEADS, FOX_HEAD_DIM, d),
                       ((0, 0), (0, LANE - FOX_HEAD_DIM), (0, 0))).reshape(FOX_HEADS * LANE, d).astype(BF16)
    w_r = jnp.zeros((d, 2 * LANE), F32)
    w_r = w_r.at[:, :N_EXPERTS].set(w_expert[0]).at[:, LANE:LANE + N_GROUPS].set(w_group[0])
    wr_hi = w_r.astype(BF16)
    wr_lo = (w_r - wr_hi.astype(F32)).astype(BF16)
    b_r = jnp.zeros((1, 2 * LANE), F32)
    b_r = b_r.at[0, :N_EXPERTS].set(b_expert[0].reshape(-1)).at[0, LANE:LANE + N_GROUPS].set(b_group[0])
    x1, h2, mi, mf, cnt = _merge_call(
        x2, oa.reshape(n, -1), ob.reshape(n, -1), g_mix, w_g, b_gate, w_pa_pad,
        w_pb[0].astype(BF16), w_o[0].astype(BF16), g_moe, wr_hi, wr_lo, b_r)

    blk = MOE_ROWS
    counts = cnt[0, :N_EXPERTS].astype(jnp.int32)
    padded = (counts + blk - 1) // blk * blk
    pend = jnp.cumsum(padded)
    pstart = pend - padded
    meta = mi[:, :4].T
    dest0 = pstart[meta[0]] + meta[2]
    dest1 = pstart[meta[1]] + meta[3]
    p_rows = n * 2 + N_EXPERTS * blk
    n_blk = p_rows // blk
    blk_pos = jnp.arange(n_blk, dtype=jnp.int32) * blk
    blk_e = jnp.minimum(jnp.sum(pend[None, :] <= blk_pos[:, None], axis=1), N_EXPERTS - 1).astype(jnp.int32)
    n_used = (pend[-1:] // blk).astype(jnp.int32)

    xs = _dispatch_call(dest0, dest1, h2, p_rows)
    yb = _expert_call(blk_e, n_used, xs, w1[0], w3[0], w2[0])
    out = _combine_call(dest0, dest1, x1, mf, g_final[None, :], yb)
    return out.reshape(bsz, seq, d)
```

```python
import functools

import jax
import jax.numpy as jnp
from jax import lax
from jax.experimental import pallas as pl
from jax.experimental.pallas import tpu as pltpu

F32 = jnp.float32
BF16 = jnp.bfloat16

D_MODEL = 1024
FOX_HEADS = 8
FOX_HEAD_DIM = 64
DIFF_HEADS = 4
DIFF_HEAD_DIM = 64
DIFF_V_DIM = 128
CHUNK = 64
N_GROUPS = 4
EXPERTS_PER_GROUP = 8
N_EXPERTS = 32
D_EXPERT = 512
EPS = 1e-6
LAM_INIT = 0.8 - 0.6 * 1.0

LANE = 128
NEG = -1e30
LOG2E = 1.4426950408889634
ATT_TILE = 512
MERGE_ROWS = 512
MOE_ROWS = 256
MOVE_ROWS = 512
VMEM_LIMIT = 56 * 1024 * 1024

FOX_V_ROWS = 80
DIFF_V_ROWS = 144

_TQF, _TVF, _TQ1 = 0, 1024, 1024 + FOX_HEADS * FOX_V_ROWS
_TQ2, _TVD = _TQ1 + 512, _TQ1 + 1024
_T_ROWS = _TVD + DIFF_HEADS * DIFF_V_ROWS
_KF, _FF, _K1, _K2, _ROW_COLS = 0, 1024, 1152, 1664, 2176


def _rms(x, g):
    return x * lax.rsqrt(jnp.mean(x * x, axis=-1, keepdims=True) + EPS) * g


def _split3(r):
    r0 = r.astype(BF16).astype(F32)
    r1 = (r - r0).astype(BF16).astype(F32)
    r2 = (r - r0 - r1).astype(BF16).astype(F32)
    return r0, r1, r2


_NT = (((1,), (1,)), ((), ()))


def _proj_kernel(x_ref, g_ref, wt_ref, bt_ref, w_ref, bf_ref,
                 qft_ref, vft_ref, q1t_ref, q2t_ref, vdt_ref, kf_ref, k1_ref, k2_ref,
                 fcol_ref, cblk_ref, carry_ref, *, bm, steps_per_seq):
    i = pl.program_id(0)
    h = _rms(x_ref[...], g_ref[...]).astype(BF16)

    zt = lax.dot_general(wt_ref[...], h, _NT, preferred_element_type=F32) + bt_ref[...]

    def heads(lo, n_heads, rows):
        return zt[lo:lo + n_heads * rows].reshape(n_heads, rows, bm).astype(BF16)

    qft_ref[...] = heads(_TQF, FOX_HEADS, LANE)
    vft_ref[...] = heads(_TVF, FOX_HEADS, FOX_V_ROWS)
    q1t_ref[...] = heads(_TQ1, DIFF_HEADS, LANE)
    q2t_ref[...] = heads(_TQ2, DIFF_HEADS, LANE)
    vdt_ref[...] = heads(_TVD, DIFF_HEADS, DIFF_V_ROWS)

    z = jnp.dot(h, w_ref[...], preferred_element_type=F32)

    zf = z[:, _FF:_K1] + bf_ref[...]
    logf = jnp.minimum(zf, 0.0) - jnp.log1p(jnp.exp(-jnp.abs(zf)))

    @pl.when(i % steps_per_seq == 0)
    def _():
        carry_ref[...] = jnp.zeros_like(carry_ref)

    c = carry_ref[...]
    row = lax.broadcasted_iota(jnp.int32, (bm, bm), 0)
    col = lax.broadcasted_iota(jnp.int32, (bm, bm), 1)
    tri = (col <= row).astype(BF16)
    l0, l1, l2 = _split3(logf)
    rel = (jnp.dot(tri, l0.astype(BF16), preferred_element_type=F32)
           + jnp.dot(tri, l1.astype(BF16), preferred_element_type=F32)
           + jnp.dot(tri, l2.astype(BF16), preferred_element_type=F32))
    fcum = rel + c
    carry_ref[...] = fcum[bm - 1:bm, :]
    fcol_ref[...] = fcum
    cblk_ref[...] = jnp.broadcast_to(c, cblk_ref.shape)

    lane = lax.broadcasted_iota(jnp.int32, (bm, LANE), 1)
    rel2 = rel * LOG2E
    for hh in range(FOX_HEADS):
        r0, r1, r2 = _split3(rel2[:, hh:hh + 1])
        aug = jnp.where(lane == FOX_HEAD_DIM, r0,
                        jnp.where(lane == FOX_HEAD_DIM + 1, r1,
                                  jnp.where(lane == FOX_HEAD_DIM + 2, r2, 0.0)))
        lo = _KF + hh * LANE
        kf_ref[:, hh * LANE:(hh + 1) * LANE] = (z[:, lo:lo + LANE] + aug).astype(BF16)

    width = DIFF_HEADS * LANE
    jrel = lax.broadcasted_iota(jnp.int32, (bm, width), 0)
    lane4 = lax.broadcasted_iota(jnp.int32, (bm, width), 1) & (LANE - 1)
    j0, j1, j2 = _split3(jrel.astype(F32) * LOG2E)
    augk = jnp.where(lane4 == DIFF_HEAD_DIM, j0,
                     jnp.where(lane4 == DIFF_HEAD_DIM + 1, j1,
                               jnp.where(lane4 == DIFF_HEAD_DIM + 2, j2, 0.0)))
    k1_ref[...] = (z[:, _K1:_K2] + augk).astype(BF16)
    k2_ref[...] = (z[:, _K2:_ROW_COLS] + augk).astype(BF16)


def _proj_call(x2, g_mix, w_t, b_t, w_row, bf_pad, bsz, seq):
    n = x2.shape[0]
    bm = ATT_TILE
    steps = n // bm
    spp = seq // bm
    row = lambda w: pl.BlockSpec((bm, w), lambda i: (i, 0))
    once = lambda a: pl.BlockSpec(a.shape, lambda i: (0,) * a.ndim, pipeline_mode=pl.Buffered(1))
    tshape = lambda heads, rows: jax.ShapeDtypeStruct((bsz, heads, spp, rows, bm), BF16)
    tspec = lambda heads, rows: pl.BlockSpec((None, heads, None, rows, bm),
                                             lambda i: (i // spp, 0, i % spp, 0, 0))
    out_shape = (
        tshape(FOX_HEADS, LANE), tshape(FOX_HEADS, FOX_V_ROWS),
        tshape(DIFF_HEADS, LANE), tshape(DIFF_HEADS, LANE), tshape(DIFF_HEADS, DIFF_V_ROWS),
        jax.ShapeDtypeStruct((n, FOX_HEADS * LANE), BF16),
        jax.ShapeDtypeStruct((n, DIFF_HEADS * LANE), BF16),
        jax.ShapeDtypeStruct((n, DIFF_HEADS * LANE), BF16),
        jax.ShapeDtypeStruct((n, LANE), F32),
        jax.ShapeDtypeStruct((steps, 8, LANE), F32),
    )
    out_specs = (
        tspec(FOX_HEADS, LANE), tspec(FOX_HEADS, FOX_V_ROWS),
        tspec(DIFF_HEADS, LANE), tspec(DIFF_HEADS, LANE), tspec(DIFF_HEADS, DIFF_V_ROWS),
        row(FOX_HEADS * LANE), row(DIFF_HEADS * LANE), row(DIFF_HEADS * LANE), row(LANE),
        pl.BlockSpec((None, 8, LANE), lambda i: (i, 0, 0)),
    )
    return pl.pallas_call(
        functools.partial(_proj_kernel, bm=bm, steps_per_seq=spp),
        grid=(steps,),
        in_specs=[row(D_MODEL), once(g_mix), once(w_t), once(b_t), once(w_row), once(bf_pad)],
        out_specs=out_specs,
        out_shape=out_shape,
        scratch_shapes=[pltpu.VMEM((1, LANE), F32)],
        compiler_params=pltpu.CompilerParams(dimension_semantics=("arbitrary",),
                                             vmem_limit_bytes=VMEM_LIMIT),
        name="proj",
    )(x2, g_mix, w_t, b_t, w_row, bf_pad)


def _softmax_step(s_ref, m_ref, acc_ref, vt, d, tile, fix=None):
    half = tile // 2
    for c0 in (0, half):
        cols = slice(c0, c0 + half)
        s = s_ref[:, cols]
        if fix is not None:
            s = fix(s, c0)
        dc = d[:, cols]
        m_old = m_ref[:, cols]
        m_new = jnp.maximum(m_old, jnp.max(s, axis=0, keepdims=True) + dc)
        p = jnp.exp2(s - (m_new - dc))
        acc_ref[:, cols] = (jnp.exp2(m_old - m_new) * acc_ref[:, cols]
                            + jnp.dot(vt, p.astype(BF16), preferred_element_type=F32))
        m_ref[:, cols] = m_new


def _tile_loop(i, fill, step, last):
    fill(0, 0)

    def pair(jj, carry):
        j = 2 * jj
        fill(1, j + 1)
        step(0, j)
        fill(0, j + 2)
        step(1, j + 1)
        return carry

    lax.fori_loop(0, i // 2, pair, 0)

    @pl.when(i % 2 == 1)
    def _():
        fill(1, i)
        step(0, i - 1)
        last(1)

    @pl.when(i % 2 == 0)
    def _():
        last(0)


def _fox_kernel(c_ref, qt_ref, k_ref, vt_ref, f_ref, o_ref, sa_ref, sb_ref, m_ref, acc_ref, *, tile, nk):
    b, hh, i = pl.program_id(0), pl.program_id(1), pl.program_id(2)
    qt = qt_ref[...]
    fi = f_ref[...] * LOG2E
    m_ref[...] = jnp.full_like(m_ref, NEG)
    acc_ref[...] = jnp.zeros_like(acc_ref)
    base = (b * FOX_HEADS + hh) * nk
    bufs = (sa_ref, sb_ref)

    def fill(slot, j):
        off = pl.multiple_of(j * tile, tile)
        bufs[slot][...] = jnp.dot(k_ref[pl.ds(off, tile), :], qt, preferred_element_type=F32)

    def step(slot, j):
        _softmax_step(bufs[slot], m_ref, acc_ref, vt_ref[j], fi - c_ref[base + j] * LOG2E, tile)

    def last(slot):
        key = lax.broadcasted_iota(jnp.int32, (tile, tile // 2), 0)
        qry = lax.broadcasted_iota(jnp.int32, (tile, tile // 2), 1)
        causal = lambda s, c0: jnp.where(key <= qry + c0, s, NEG)
        _softmax_step(bufs[slot], m_ref, acc_ref, vt_ref[i], fi - c_ref[base + i] * LOG2E, tile, causal)

    _tile_loop(i, fill, step, last)

    acc = acc_ref[...]
    o = acc[:FOX_HEAD_DIM] / acc[FOX_HEAD_DIM:FOX_HEAD_DIM + 1]
    o = jnp.concatenate([o, jnp.zeros((LANE - FOX_HEAD_DIM, tile), F32)], axis=0)
    o_ref[...] = o.T.astype(BF16)


def _fox_call(c_flat, qft, kf, vft, f_rows, tile):
    bsz, _, nq, _, _ = qft.shape
    seq = nq * tile
    grid_spec = pltpu.PrefetchScalarGridSpec(
        num_scalar_prefetch=1,
        grid=(bsz, FOX_HEADS, nq),
        in_specs=[
            pl.BlockSpec((None, None, None, LANE, tile), lambda b, h, i, c: (b, h, i, 0, 0)),
            pl.BlockSpec((None, seq, LANE), lambda b, h, i, c: (b, 0, h)),
            pl.BlockSpec((None, None, nq, FOX_V_ROWS, tile), lambda b, h, i, c: (b, h, 0, 0, 0)),
            pl.BlockSpec((None, None, None, 1, tile), lambda b, h, i, c: (b, h, i, 0, 0)),
        ],
        out_specs=pl.BlockSpec((None, tile, LANE), lambda b, h, i, c: (b, i, h)),
        scratch_shapes=[pltpu.VMEM((tile, tile), F32), pltpu.VMEM((tile, tile), F32),
                        pltpu.VMEM((1, tile), F32), pltpu.VMEM((FOX_V_ROWS, tile), F32)],
    )
    return pl.pallas_call(
        functools.partial(_fox_kernel, tile=tile, nk=nq),
        grid_spec=grid_spec,
        out_shape=jax.ShapeDtypeStruct((bsz, seq, FOX_HEADS * LANE), BF16),
        compiler_params=pltpu.CompilerParams(
            dimension_semantics=("parallel", "parallel", "arbitrary"),
            vmem_limit_bytes=VMEM_LIMIT),
        name="fox_attn",
    )(c_flat, qft, kf, vft, f_rows)


def _diff_kernel(sc_ref, q1t_ref, q2t_ref, k1_ref, k2_ref, vt_ref, g_ref, o_ref,
                 s1a_ref, s1b_ref, s2a_ref, s2b_ref, m1_ref, a1_ref, m2_ref, a2_ref, *, tile):
    hh, i = pl.program_id(1), pl.program_id(2)
    slope = sc_ref[hh] * LOG2E
    lam = sc_ref[DIFF_HEADS]
    q1t = q1t_ref[...]
    q2t = q2t_ref[...]
    for m_ref, a_ref in ((m1_ref, a1_ref), (m2_ref, a2_ref)):
        m_ref[...] = jnp.full_like(m_ref, NEG)
        a_ref[...] = jnp.zeros_like(a_ref)
    bufs1 = (s1a_ref, s1b_ref)
    bufs2 = (s2a_ref, s2b_ref)
    qrel = lax.broadcasted_iota(jnp.int32, (1, tile), 1)

    def fill(slot, j):
        off = pl.multiple_of(j * tile, tile)
        bufs1[slot][...] = jnp.dot(k1_ref[pl.ds(off, tile), :], q1t, preferred_element_type=F32)
        bufs2[slot][...] = jnp.dot(k2_ref[pl.ds(off, tile), :], q2t, preferred_element_type=F32)

    def step(slot, j):
        d = slope * ((j - i) * tile - qrel).astype(F32)
        vt = vt_ref[j]
        _softmax_step(bufs1[slot], m1_ref, a1_ref, vt, d, tile)
        _softmax_step(bufs2[slot], m2_ref, a2_ref, vt, d, tile)

    def last(slot):
        key = lax.broadcasted_iota(jnp.int32, (tile, tile // 2), 0)
        qry = lax.broadcasted_iota(jnp.int32, (tile, tile // 2), 1)

        def chunk_bias(s, c0):
            q = qry + c0
            bias = -slope * (key + jnp.abs(q - key)).astype(F32)
            return jnp.where((key // CHUNK) <= (q // CHUNK), s + bias, NEG)

        zero = jnp.zeros((1, tile), F32)
        vt = vt_ref[i]
        _softmax_step(bufs1[slot], m1_ref, a1_ref, vt, zero, tile, chunk_bias)
        _softmax_step(bufs2[slot], m2_ref, a2_ref, vt, zero, tile, chunk_bias)

    _tile_loop(i, fill, step, last)

    a1 = a1_ref[...]
    a2 = a2_ref[...]
    o1 = a1[:DIFF_V_DIM] / a1[DIFF_V_DIM:DIFF_V_DIM + 1]
    o2 = a2[:DIFF_V_DIM] / a2[DIFF_V_DIM:DIFF_V_DIM + 1]
    ob = o1 - lam * o2
    ob = ob * lax.rsqrt(jnp.mean(ob * ob, axis=0, keepdims=True) + EPS) * g_ref[...] * (1.0 - LAM_INIT)
    o_ref[...] = ob.T.astype(BF16)


def _diff_call(scal, q1t, q2t, k1, k2, vdt, g_col, tile):
    bsz, _, nq, _, _ = q1t.shape
    seq = nq * tile
    qspec = pl.BlockSpec((None, None, None, LANE, tile), lambda b, h, i, c: (b, h, i, 0, 0))
    kspec = pl.BlockSpec((None, seq, LANE), lambda b, h, i, c: (b, 0, h))
    score = pltpu.VMEM((tile, tile), F32)
    grid_spec = pltpu.PrefetchScalarGridSpec(
        num_scalar_prefetch=1,
        grid=(bsz, DIFF_HEADS, nq),
        in_specs=[qspec, qspec, kspec, kspec,
                  pl.BlockSpec((None, None, nq, DIFF_V_ROWS, tile), lambda b, h, i, c: (b, h, 0, 0, 0)),
                  pl.BlockSpec((DIFF_V_DIM, 1), lambda b, h, i, c: (0, 0))],
        out_specs=pl.BlockSpec((None, tile, LANE), lambda b, h, i, c: (b, i, h)),
        scratch_shapes=[score, score, score, score,
                        pltpu.VMEM((1, tile), F32), pltpu.VMEM((DIFF_V_ROWS, tile), F32),
                        pltpu.VMEM((1, tile), F32), pltpu.VMEM((DIFF_V_ROWS, tile), F32)],
    )
    return pl.pallas_call(
        functools.partial(_diff_kernel, tile=tile),
        grid_spec=grid_spec,
        out_shape=jax.ShapeDtypeStruct((bsz, seq, DIFF_HEADS * LANE), BF16),
        compiler_params=pltpu.CompilerParams(
            dimension_semantics=("parallel", "parallel", "arbitrary"),
            vmem_limit_bytes=VMEM_LIMIT),
        name="diff_attn",
    )(scal, q1t, q2t, k1, k2, vdt, g_col)


def _merge_kernel(x_ref, oa_ref, ob_ref, gmix_ref, wg_ref, bg_ref, wpa_ref, wpb_ref, wo_ref,
                  gmoe_ref, wrh_ref, wrl_ref, br_ref,
                  x1_ref, h2_ref, mi_ref, mf_ref, cnt_ref, run_ref, *, bm):
    i = pl.program_id(0)

    @pl.when(i == 0)
    def _():
        run_ref[...] = jnp.zeros_like(run_ref)

    x = x_ref[...]
    h = _rms(x, gmix_ref[...]).astype(BF16)
    gates = jnp.dot(h, wg_ref[...], preferred_element_type=F32) + bg_ref[...]
    gates = 1.0 / (1.0 + jnp.exp(-gates))
    ya = jnp.dot(oa_ref[...], wpa_ref[...], preferred_element_type=F32)
    yb = jnp.dot(ob_ref[...], wpb_ref[...], preferred_element_type=F32)
    y = gates[:, :D_MODEL] * ya + gates[:, D_MODEL:] * yb
    x1 = x + jnp.dot(y.astype(BF16), wo_ref[...], preferred_element_type=F32)
    x1_ref[...] = x1
    h2 = _rms(x1, gmoe_ref[...])
    h2_ref[...] = h2

    h2h = h2.astype(BF16)
    h2l = (h2 - h2h.astype(F32)).astype(BF16)
    r = (jnp.dot(h2h, wrh_ref[...], preferred_element_type=F32)
         + jnp.dot(h2l, wrh_ref[...], preferred_element_type=F32)
         + jnp.dot(h2h, wrl_ref[...], preferred_element_type=F32)) + br_ref[...]
    el_all = r[:, :LANE]
    gl = r[:, LANE:]

    lane_i = lax.broadcasted_iota(jnp.int32, (bm, LANE), 1)
    lane = lane_i.astype(F32)
    big = float(LANE)

    def first_argmax(vals):
        vmax = jnp.max(vals, axis=1, keepdims=True)
        idx = jnp.min(jnp.where(vals == vmax, lane, big), axis=1, keepdims=True)
        return vmax, idx

    glm = jnp.where(lane_i < N_GROUPS, gl, NEG)
    gmax, g_idx = first_argmax(glm)
    g_w = 1.0 / jnp.sum(jnp.exp(glm - gmax), axis=1, keepdims=True)

    in_group = ((lane_i >> 3).astype(F32) == g_idx) & (lane_i < N_EXPERTS)
    elm = jnp.where(in_group, el_all, NEG)
    e1, idx1 = first_argmax(elm)
    elm2 = jnp.where(lane == idx1, NEG, elm)
    e2, idx2 = first_argmax(elm2)
    t = jnp.exp(e2 - e1)
    w1 = g_w / (1.0 + t)
    w2 = g_w * t / (1.0 + t)

    oh1 = (lane == idx1).astype(F32)
    oh2 = (lane == idx2).astype(F32)
    oh = oh1 + oh2
    row = lax.broadcasted_iota(jnp.int32, (bm, bm), 0)
    col = lax.broadcasted_iota(jnp.int32, (bm, bm), 1)
    strict = (col < row).astype(BF16)
    before = jnp.dot(strict, oh.astype(BF16), preferred_element_type=F32) + run_ref[...]
    rank1 = jnp.sum(oh1 * before, axis=1, keepdims=True)
    rank2 = jnp.sum(oh2 * before, axis=1, keepdims=True)
    run_ref[...] = run_ref[...] + jnp.sum(oh, axis=0, keepdims=True)
    cnt_ref[...] = jnp.broadcast_to(run_ref[...], cnt_ref.shape)

    meta = jnp.where(lane_i == 0, idx1, jnp.where(lane_i == 1, idx2,
                     jnp.where(lane_i == 2, rank1, jnp.where(lane_i == 3, rank2, 0.0))))
    mi_ref[...] = meta.astype(jnp.int32)
    mf_ref[...] = jnp.where(lane_i == 0, w1, jnp.where(lane_i == 1, w2, 0.0))


def _merge_call(x2, oa, ob, g_mix, w_g, b_g, w_pa, w_pb, w_o, g_moe, wr_hi, wr_lo, b_r):
    n = x2.shape[0]
    bm = MERGE_ROWS
    row = lambda w: pl.BlockSpec((bm, w), lambda i: (i, 0))
    full = lambda a: pl.BlockSpec(a.shape, lambda i: (0,) * a.ndim)
    consts = (g_mix, w_g, b_g, w_pa, w_pb, w_o, g_moe, wr_hi, wr_lo, b_r)
    return pl.pallas_call(
        functools.partial(_merge_kernel, bm=bm),
        grid=(n // bm,),
        in_specs=[row(D_MODEL), row(oa.shape[1]), row(ob.shape[1])] + [full(a) for a in consts],
        out_specs=(row(D_MODEL), row(D_MODEL), row(LANE), row(LANE),
                   pl.BlockSpec((8, LANE), lambda i: (0, 0))),
        out_shape=(jax.ShapeDtypeStruct((n, D_MODEL), F32),
                   jax.ShapeDtypeStruct((n, D_MODEL), F32),
                   jax.ShapeDtypeStruct((n, LANE), jnp.int32),
                   jax.ShapeDtypeStruct((n, LANE), F32),
                   jax.ShapeDtypeStruct((8, LANE), F32)),
        scratch_shapes=[pltpu.VMEM((1, LANE), F32)],
        compiler_params=pltpu.CompilerParams(dimension_semantics=("arbitrary",),
                                             vmem_limit_bytes=VMEM_LIMIT),
        name="merge_router",
    )(x2, oa, ob, *consts)


SUBLANES = 8


def _row_copy(src_ref, src_row, dst_ref, dst_row, sem):
    return pltpu.make_async_copy(src_ref.at[pl.ds(src_row, 1), :],
                                 dst_ref.at[pl.ds(dst_row, 1), :], sem)


def _dispatch_kernel(d0_ref, d1_ref, h_ref, xs_in_ref, xs_ref, sem, *, bt):
    del xs_in_ref
    i = pl.program_id(0)

    def issue(g, carry):
        base = pl.multiple_of(g * SUBLANES, SUBLANES)
        rows = h_ref.at[pl.ds(base, SUBLANES), :]
        for u in range(SUBLANES):
            t = i * bt + base + u
            _row_copy(rows, u, xs_ref, d0_ref[t], sem).start()
            _row_copy(rows, u, xs_ref, d1_ref[t], sem).start()
        return carry

    lax.fori_loop(0, bt // SUBLANES, issue, 0)

    for _ in range(2):
        pltpu.make_async_copy(h_ref, xs_ref.at[pl.ds(0, bt), :], sem).wait()


def _dispatch_call(dest0, dest1, h2, p_rows):
    n, d = h2.shape
    bt = MOVE_ROWS
    grid_spec = pltpu.PrefetchScalarGridSpec(
        num_scalar_prefetch=2,
        grid=(n // bt,),
        in_specs=[pl.BlockSpec((bt, d), lambda i, d0, d1: (i, 0)),
                  pl.BlockSpec(memory_space=pl.ANY)],
        out_specs=pl.BlockSpec(memory_space=pl.ANY),
        scratch_shapes=[pltpu.SemaphoreType.DMA(())],
    )
    return pl.pallas_call(
        functools.partial(_dispatch_kernel, bt=bt),
        grid_spec=grid_spec,
        out_shape=jax.ShapeDtypeStruct((p_rows, d), h2.dtype),
        input_output_aliases={3: 0},
        compiler_params=pltpu.CompilerParams(dimension_semantics=("arbitrary",),
                                             vmem_limit_bytes=VMEM_LIMIT),
        name="dispatch",
    )(dest0, dest1, h2, jnp.zeros((p_rows, d), h2.dtype))


def _expert_kernel(blk_e_ref, used_ref, x_ref, w1_ref, w3_ref, w2_ref, y_ref, w1b, w3b, w2b):
    i = pl.program_id(0)
    live = i < used_ref[0]
    fresh = (i == 0) | (blk_e_ref[i] != blk_e_ref[jnp.maximum(i - 1, 0)])

    @pl.when(live & fresh)
    def _():
        w1b[...] = w1_ref[...].astype(BF16)
        w3b[...] = w3_ref[...].astype(BF16)
        w2b[...] = w2_ref[...].astype(BF16)

    @pl.when(live)
    def _():
        xb = x_ref[...].astype(BF16)
        a = jnp.dot(xb, w1b[...], preferred_element_type=F32)
        g = jnp.dot(xb, w3b[...], preferred_element_type=F32)
        mid = (a / (1.0 + jnp.exp(-a))) * g
        y_ref[...] = jnp.dot(mid.astype(BF16), w2b[...], preferred_element_type=F32)

    @pl.when(i >= used_ref[0])
    def _():
        y_ref[...] = jnp.zeros_like(y_ref)


def _expert_call(blk_e, n_used, xs, w1, w3, w2):
    p_rows, d = xs.shape
    blk = MOE_ROWS
    x_idx = lambda i, be, nu: (jnp.minimum(i, nu[0] - 1), 0)
    grid_spec = pltpu.PrefetchScalarGridSpec(
        num_scalar_prefetch=2,
        grid=(p_rows // blk,),
        in_specs=[pl.BlockSpec((blk, d), x_idx),
                  pl.BlockSpec((None, d, D_EXPERT), lambda i, be, nu: (be[i], 0, 0)),
                  pl.BlockSpec((None, d, D_EXPERT), lambda i, be, nu: (be[i], 0, 0)),
                  pl.BlockSpec((None, D_EXPERT, d), lambda i, be, nu: (be[i], 0, 0))],
        out_specs=pl.BlockSpec((blk, d), lambda i, be, nu: (i, 0)),
        scratch_shapes=[pltpu.VMEM((d, D_EXPERT), BF16), pltpu.VMEM((d, D_EXPERT), BF16),
                        pltpu.VMEM((D_EXPERT, d), BF16)],
    )
    return pl.pallas_call(
        _expert_kernel,
        grid_spec=grid_spec,
        out_shape=jax.ShapeDtypeStruct((p_rows, d), F32),
        compiler_params=pltpu.CompilerParams(dimension_semantics=("arbitrary",),
                                             vmem_limit_bytes=VMEM_LIMIT),
        name="experts",
    )(blk_e, n_used, xs, w1, w3, w2)


def _combine_kernel(d0_ref, d1_ref, x1_ref, mf_ref, g_ref, yb_ref, o_ref, buf0, buf1, sem, *, bt):
    i = pl.program_id(0)

    def issue(g, carry):
        base = pl.multiple_of(g * SUBLANES, SUBLANES)
        rows0 = buf0.at[pl.ds(base, SUBLANES), :]
        rows1 = buf1.at[pl.ds(base, SUBLANES), :]
        for u in range(SUBLANES):
            t = i * bt + base + u
            _row_copy(yb_ref, d0_ref[t], rows0, u, sem).start()
            _row_copy(yb_ref, d1_ref[t], rows1, u, sem).start()
        return carry

    lax.fori_loop(0, bt // SUBLANES, issue, 0)

    for buf in (buf0, buf1):
        pltpu.make_async_copy(yb_ref.at[pl.ds(0, bt), :], buf, sem).wait()

    mf = mf_ref[...]
    x2 = x1_ref[...] + mf[:, 0:1] * buf0[...] + mf[:, 1:2] * buf1[...]
    o_ref[...] = _rms(x2, g_ref[...])


def _combine_call(dest0, dest1, x1, mf, g_final, yb):
    n, d = x1.shape
    bt = MOVE_ROWS
    grid_spec = pltpu.PrefetchScalarGridSpec(
        num_scalar_prefetch=2,
        grid=(n // bt,),
        in_specs=[pl.BlockSpec((bt, d), lambda i, d0, d1: (i, 0)),
                  pl.BlockSpec((bt, LANE), lambda i, d0, d1: (i, 0)),
                  pl.BlockSpec((1, d), lambda i, d0, d1: (0, 0)),
                  pl.BlockSpec(memory_space=pl.ANY)],
        out_specs=pl.BlockSpec((bt, d), lambda i, d0, d1: (i, 0)),
        scratch_shapes=[pltpu.VMEM((bt, d), F32), pltpu.VMEM((bt, d), F32),
                        pltpu.SemaphoreType.DMA(())],
    )
    return pl.pallas_call(
        functools.partial(_combine_kernel, bt=bt),
        grid_spec=grid_spec,
        out_shape=jax.ShapeDtypeStruct((n, d), F32),
        compiler_params=pltpu.CompilerParams(dimension_semantics=("arbitrary",),
                                             vmem_limit_bytes=VMEM_LIMIT),
        name="combine",
    )(dest0, dest1, x1, mf, g_final, yb)


def _pad_heads(w, heads, width, padded, scale=1.0):
    d = w.shape[0]
    w = (w * scale).reshape(d, heads, width)
    return jnp.pad(w, ((0, 0), (0, 0), (0, padded - width))).reshape(d, heads * padded)


def _build_proj_params(w_in, b_fgate):
    fw, dw = FOX_HEADS * FOX_HEAD_DIM, DIFF_HEADS * DIFF_V_DIM
    o = 0
    fq, fk, fv = (w_in[:, o + k * fw:o + (k + 1) * fw] for k in range(3))
    o += 3 * fw
    ff = w_in[:, o:o + FOX_HEADS]
    o += FOX_HEADS
    dq, dk, dv = (w_in[:, o + k * dw:o + (k + 1) * dw] for k in range(3))
    d = w_in.shape[0]
    dq = dq.reshape(d, DIFF_HEADS, 2, DIFF_HEAD_DIM)
    dk = dk.reshape(d, DIFF_HEADS, 2, DIFF_HEAD_DIM)
    fscale = FOX_HEAD_DIM ** -0.5 * LOG2E
    dscale = DIFF_HEAD_DIM ** -0.5 * LOG2E
    half = lambda t, k: t[:, :, k, :].reshape(d, DIFF_HEADS * DIFF_HEAD_DIM)
    w_t = jnp.concatenate([
        _pad_heads(fq, FOX_HEADS, FOX_HEAD_DIM, LANE, fscale),
        _pad_heads(fv, FOX_HEADS, FOX_HEAD_DIM, FOX_V_ROWS),
        _pad_heads(half(dq, 0), DIFF_HEADS, DIFF_HEAD_DIM, LANE, dscale),
        _pad_heads(half(dq, 1), DIFF_HEADS, DIFF_HEAD_DIM, LANE, dscale),
        _pad_heads(dv, DIFF_HEADS, DIFF_V_DIM, DIFF_V_ROWS),
    ], axis=1).T.astype(BF16)
    w_row = jnp.concatenate([
        _pad_heads(fk, FOX_HEADS, FOX_HEAD_DIM, LANE),
        jnp.pad(ff, ((0, 0), (0, LANE - FOX_HEADS))),
        _pad_heads(half(dk, 0), DIFF_HEADS, DIFF_HEAD_DIM, LANE),
        _pad_heads(half(dk, 1), DIFF_HEADS, DIFF_HEAD_DIM, LANE),
    ], axis=1).astype(BF16)

    slopes = [2.0 ** (-8.0 / DIFF_HEADS * (hh + 1)) for hh in range(DIFF_HEADS)]
    b = jnp.zeros((_T_ROWS,), F32)
    for hh in range(FOX_HEADS):
        b = b.at[_TQF + hh * LANE + FOX_HEAD_DIM:_TQF + hh * LANE + FOX_HEAD_DIM + 3].set(-1.0)
        b = b.at[_TVF + hh * FOX_V_ROWS + FOX_HEAD_DIM].set(1.0)
    for hh in range(DIFF_HEADS):
        for base in (_TQ1, _TQ2):
            b = b.at[base + hh * LANE + DIFF_HEAD_DIM:base + hh * LANE + DIFF_HEAD_DIM + 3].set(slopes[hh])
        b = b.at[_TVD + hh * DIFF_V_ROWS + DIFF_V_DIM].set(1.0)
    bf_pad = jnp.pad(b_fgate, (0, LANE - FOX_HEADS))[None, :]
    return w_t, b[:, None], w_row, bf_pad, slopes


def kernel(x, g_mix, w_in, b_fgate, b_gate, lam_q1, lam_k1, lam_q2, lam_k2, g_subln, w_pa, w_pb, w_o,
           g_moe, w_group, b_group, w_expert, b_expert, w1, w3, w2, g_final):
    bsz, seq, d = x.shape
    n = bsz * seq
    tile = ATT_TILE
    nk = seq // tile
    x2 = x.reshape(n, d)

    w_t, b_t, w_row, bf_pad, slopes = _build_proj_params(w_in[0], b_fgate[0])
    qft, vft, q1t, q2t, vdt, kf, k1, k2, fcol, cblk = _proj_call(x2, g_mix, w_t, b_t, w_row, bf_pad, bsz, seq)
    shp = lambda a: a.reshape(bsz, seq, a.shape[1])
    c_flat = cblk[:, 0, :FOX_HEADS].reshape(bsz, nk, FOX_HEADS).transpose(0, 2, 1).reshape(-1)
    f_rows = (fcol[:, :FOX_HEADS].reshape(bsz, nk, tile, FOX_HEADS).transpose(0, 3, 1, 2)
              .reshape(bsz, FOX_HEADS, nk, 1, tile))

    oa = _fox_call(c_flat, qft, shp(kf), vft, f_rows, tile)
    lam = (jnp.exp(jnp.sum(lam_q1[0] * lam_k1[0])) - jnp.exp(jnp.sum(lam_q2[0] * lam_k2[0])) + LAM_INIT)
    scal = jnp.concatenate([jnp.asarray(slopes, F32), lam[None].astype(F32)])
    ob = _diff_call(scal, q1t, q2t, shp(k1), shp(k2), vdt, g_subln[0][:, None], tile)

    o = 3 * FOX_HEADS * FOX_HEAD_DIM + FOX_HEADS + 3 * DIFF_HEADS * DIFF_V_DIM
    w_g = w_in[0][:, o:].astype(BF16)
    w_pa_pad = jnp.pad(w_pa[0].reshape(FOX_HEADS, FOX_HEAD_DIM, d),
                       ((0, 0), (0, LANE - FOX_HEAD_DIM), (0, 0))).reshape(FOX_HEADS * LANE, d).astype(BF16)
    w_r = jnp.zeros((d, 2 * LANE), F32)
    w_r = w_r.at[:, :N_EXPERTS].set(w_expert[0]).at[:, LANE:LANE + N_GROUPS].set(w_group[0])
    wr_hi = w_r.astype(BF16)
    wr_lo = (w_r - wr_hi.astype(F32)).astype(BF16)
    b_r = jnp.zeros((1, 2 * LANE), F32)
    b_r = b_r.at[0, :N_EXPERTS].set(b_expert[0].reshape(-1)).at[0, LANE:LANE + N_GROUPS].set(b_group[0])
    x1, h2, mi, mf, cnt = _merge_call(
        x2, oa.reshape(n, -1), ob.reshape(n, -1), g_mix, w_g, b_gate, w_pa_pad,
        w_pb[0].astype(BF16), w_o[0].astype(BF16), g_moe, wr_hi, wr_lo, b_r)

    blk = MOE_ROWS
    counts = cnt[0, :N_EXPERTS].astype(jnp.int32)
    padded = (counts + blk - 1) // blk * blk
    pend = jnp.cumsum(padded)
    pstart = pend - padded
    meta = mi[:, :4].T
    dest0 = pstart[meta[0]] + meta[2]
    dest1 = pstart[meta[1]] + meta[3]
    p_rows = n * 2 + N_EXPERTS * blk
    n_blk = p_rows // blk
    blk_pos = jnp.arange(n_blk, dtype=jnp.int32) * blk
    blk_e = jnp.minimum(jnp.sum(pend[None, :] <= blk_pos[:, None], axis=1), N_EXPERTS - 1).astype(jnp.int32)
    n_used = (pend[-1:] // blk).astype(jnp.int32)

    xs = _dispatch_call(dest0, dest1, h2, p_rows)
    yb = _expert_call(blk_e, n_used, xs, w1[0], w3[0], w2[0])
    out = _combine_call(dest0, dest1, x1, mf, g_final[None, :], yb)
    return out.reshape(bsz, seq, d)
```

```python
import functools

import jax
import jax.numpy as jnp
from jax import lax
from jax.experimental import pallas as pl
from jax.experimental.pallas import tpu as pltpu

F32 = jnp.float32
BF16 = jnp.bfloat16

D_MODEL = 1024
FOX_HEADS = 8
FOX_HEAD_DIM = 64
DIFF_HEADS = 4
DIFF_HEAD_DIM = 64
DIFF_V_DIM = 128
CHUNK = 64
N_GROUPS = 4
EXPERTS_PER_GROUP = 8
N_EXPERTS = 32
D_EXPERT = 512
EPS = 1e-6
LAM_INIT = 0.8 - 0.6 * 1.0

LANE = 128
NEG = -1e30
LOG2E = 1.4426950408889634
ATT_TILE = 512
MERGE_ROWS = 512
MOE_ROWS = 512
MOVE_ROWS = 512
VMEM_LIMIT = 56 * 1024 * 1024

FOX_V_ROWS = 80
DIFF_V_ROWS = 144

_TQF, _TVF, _TQ1 = 0, 1024, 1024 + FOX_HEADS * FOX_V_ROWS
_TQ2, _TVD = _TQ1 + 512, _TQ1 + 1024
_T_ROWS = _TVD + DIFF_HEADS * DIFF_V_ROWS
_KF, _FF, _K1, _K2, _ROW_COLS = 0, 1024, 1152, 1664, 2176


def _rms(x, g):
    return x * lax.rsqrt(jnp.mean(x * x, axis=-1, keepdims=True) + EPS) * g


def _split3(r):
    r0 = r.astype(BF16).astype(F32)
    r1 = (r - r0).astype(BF16).astype(F32)
    r2 = (r - r0 - r1).astype(BF16).astype(F32)
    return r0, r1, r2


_NT = (((1,), (1,)), ((), ()))


def _proj_kernel(x_ref, g_ref, wt_ref, bt_ref, w_ref, bf_ref,
                 qft_ref, vft_ref, q1t_ref, q2t_ref, vdt_ref, kf_ref, k1_ref, k2_ref,
                 fcol_ref, cblk_ref, carry_ref, *, bm, steps_per_seq):
    i = pl.program_id(0)
    h = _rms(x_ref[...], g_ref[...]).astype(BF16)

    zt = lax.dot_general(wt_ref[...], h, _NT, preferred_element_type=F32) + bt_ref[...]

    def heads(lo, n_heads, rows):
        return zt[lo:lo + n_heads * rows].reshape(n_heads, rows, bm).astype(BF16)

    qft_ref[...] = heads(_TQF, FOX_HEADS, LANE)
    vft_ref[...] = heads(_TVF, FOX_HEADS, FOX_V_ROWS)
    q1t_ref[...] = heads(_TQ1, DIFF_HEADS, LANE)
    q2t_ref[...] = heads(_TQ2, DIFF_HEADS, LANE)
    vdt_ref[...] = heads(_TVD, DIFF_HEADS, DIFF_V_ROWS)

    z = jnp.dot(h, w_ref[...], preferred_element_type=F32)

    zf = z[:, _FF:_K1] + bf_ref[...]
    logf = jnp.minimum(zf, 0.0) - jnp.log1p(jnp.exp(-jnp.abs(zf)))

    @pl.when(i % steps_per_seq == 0)
    def _():
        carry_ref[...] = jnp.zeros_like(carry_ref)

    c = carry_ref[...]
    row = lax.broadcasted_iota(jnp.int32, (bm, bm), 0)
    col = lax.broadcasted_iota(jnp.int32, (bm, bm), 1)
    tri = (col <= row).astype(BF16)
    l0, l1, l2 = _split3(logf)
    rel = (jnp.dot(tri, l0.astype(BF16), preferred_element_type=F32)
           + jnp.dot(tri, l1.astype(BF16), preferred_element_type=F32)
           + jnp.dot(tri, l2.astype(BF16), preferred_element_type=F32))
    fcum = rel + c
    carry_ref[...] = fcum[bm - 1:bm, :]
    fcol_ref[...] = fcum
    cblk_ref[...] = jnp.broadcast_to(c, cblk_ref.shape)

    lane = lax.broadcasted_iota(jnp.int32, (bm, LANE), 1)
    rel2 = rel * LOG2E
    for hh in range(FOX_HEADS):
        r0, r1, r2 = _split3(rel2[:, hh:hh + 1])
        aug = jnp.where(lane == FOX_HEAD_DIM, r0,
                        jnp.where(lane == FOX_HEAD_DIM + 1, r1,
                                  jnp.where(lane == FOX_HEAD_DIM + 2, r2, 0.0)))
        lo = _KF + hh * LANE
        kf_ref[:, hh * LANE:(hh + 1) * LANE] = (z[:, lo:lo + LANE] + aug).astype(BF16)

    width = DIFF_HEADS * LANE
    jrel = lax.broadcasted_iota(jnp.int32, (bm, width), 0)
    lane4 = lax.broadcasted_iota(jnp.int32, (bm, width), 1) & (LANE - 1)
    j0, j1, j2 = _split3(jrel.astype(F32) * LOG2E)
    augk = jnp.where(lane4 == DIFF_HEAD_DIM, j0,
                     jnp.where(lane4 == DIFF_HEAD_DIM + 1, j1,
                               jnp.where(lane4 == DIFF_HEAD_DIM + 2, j2, 0.0)))
    k1_ref[...] = (z[:, _K1:_K2] + augk).astype(BF16)
    k2_ref[...] = (z[:, _K2:_ROW_COLS] + augk).astype(BF16)


def _proj_call(x2, g_mix, w_t, b_t, w_row, bf_pad, bsz, seq):
    n = x2.shape[0]
    bm = ATT_TILE
    steps = n // bm
    spp = seq // bm
    row = lambda w: pl.BlockSpec((bm, w), lambda i: (i, 0))
    once = lambda a: pl.BlockSpec(a.shape, lambda i: (0,) * a.ndim, pipeline_mode=pl.Buffered(1))
    tshape = lambda heads, rows: jax.ShapeDtypeStruct((bsz, heads, spp, rows, bm), BF16)
    tspec = lambda heads, rows: pl.BlockSpec((None, heads, None, rows, bm),
                                             lambda i: (i // spp, 0, i % spp, 0, 0))
    out_shape = (
        tshape(FOX_HEADS, LANE), tshape(FOX_HEADS, FOX_V_ROWS),
        tshape(DIFF_HEADS, LANE), tshape(DIFF_HEADS, LANE), tshape(DIFF_HEADS, DIFF_V_ROWS),
        jax.ShapeDtypeStruct((n, FOX_HEADS * LANE), BF16),
        jax.ShapeDtypeStruct((n, DIFF_HEADS * LANE), BF16),
        jax.ShapeDtypeStruct((n, DIFF_HEADS * LANE), BF16),
        jax.ShapeDtypeStruct((n, LANE), F32),
        jax.ShapeDtypeStruct((steps, 8, LANE), F32),
    )
    out_specs = (
        tspec(FOX_HEADS, LANE), tspec(FOX_HEADS, FOX_V_ROWS),
        tspec(DIFF_HEADS, LANE), tspec(DIFF_HEADS, LANE), tspec(DIFF_HEADS, DIFF_V_ROWS),
        row(FOX_HEADS * LANE), row(DIFF_HEADS * LANE), row(DIFF_HEADS * LANE), row(LANE),
        pl.BlockSpec((None, 8, LANE), lambda i: (i, 0, 0)),
    )
    return pl.pallas_call(
        functools.partial(_proj_kernel, bm=bm, steps_per_seq=spp),
        grid=(steps,),
        in_specs=[row(D_MODEL), once(g_mix), once(w_t), once(b_t), once(w_row), once(bf_pad)],
        out_specs=out_specs,
        out_shape=out_shape,
        scratch_shapes=[pltpu.VMEM((1, LANE), F32)],
        compiler_params=pltpu.CompilerParams(dimension_semantics=("arbitrary",),
                                             vmem_limit_bytes=VMEM_LIMIT),
        name="proj",
    )(x2, g_mix, w_t, b_t, w_row, bf_pad)


def _softmax_step(s_ref, m_ref, acc_ref, vt, d, tile, fix=None):
    half = tile // 2
    for c0 in (0, half):
        cols = slice(c0, c0 + half)
        s = s_ref[:, cols]
        if fix is not None:
            s = fix(s, c0)
        dc = d[:, cols]
        m_old = m_ref[:, cols]
        m_new = jnp.maximum(m_old, jnp.max(s, axis=0, keepdims=True) + dc)
        p = jnp.exp2(s - (m_new - dc))
        acc_ref[:, cols] = (jnp.exp2(m_old - m_new) * acc_ref[:, cols]
                            + jnp.dot(vt, p.astype(BF16), preferred_element_type=F32))
        m_ref[:, cols] = m_new


def _tile_loop(i, fill, step, last):
    fill(0, 0)

    def pair(jj, carry):
        j = 2 * jj
        fill(1, j + 1)
        step(0, j)
        fill(0, j + 2)
        step(1, j + 1)
        return carry

    lax.fori_loop(0, i // 2, pair, 0)

    @pl.when(i % 2 == 1)
    def _():
        fill(1, i)
        step(0, i - 1)
        last(1)

    @pl.when(i % 2 == 0)
    def _():
        last(0)


def _fox_kernel(c_ref, qt_ref, k_ref, vt_ref, f_ref, o_ref, sa_ref, sb_ref, m_ref, acc_ref, *, tile, nk):
    b, hh, i = pl.program_id(0), pl.program_id(1), pl.program_id(2)
    qt = qt_ref[...]
    fi = f_ref[...] * LOG2E
    m_ref[...] = jnp.full_like(m_ref, NEG)
    acc_ref[...] = jnp.zeros_like(acc_ref)
    base = (b * FOX_HEADS + hh) * nk
    bufs = (sa_ref, sb_ref)

    def fill(slot, j):
        off = pl.multiple_of(j * tile, tile)
        bufs[slot][...] = jnp.dot(k_ref[pl.ds(off, tile), :], qt, preferred_element_type=F32)

    def step(slot, j):
        _softmax_step(bufs[slot], m_ref, acc_ref, vt_ref[j], fi - c_ref[base + j] * LOG2E, tile)

    def last(slot):
        key = lax.broadcasted_iota(jnp.int32, (tile, tile // 2), 0)
        qry = lax.broadcasted_iota(jnp.int32, (tile, tile // 2), 1)
        causal = lambda s, c0: jnp.where(key <= qry + c0, s, NEG)
        _softmax_step(bufs[slot], m_ref, acc_ref, vt_ref[i], fi - c_ref[base + i] * LOG2E, tile, causal)

    _tile_loop(i, fill, step, last)

    acc = acc_ref[...]
    o = acc[:FOX_HEAD_DIM] / acc[FOX_HEAD_DIM:FOX_HEAD_DIM + 1]
    o = jnp.concatenate([o, jnp.zeros((LANE - FOX_HEAD_DIM, tile), F32)], axis=0)
    o_ref[...] = o.T.astype(BF16)


def _fox_call(c_flat, qft, kf, vft, f_rows, tile):
    bsz, _, nq, _, _ = qft.shape
    seq = nq * tile
    grid_spec = pltpu.PrefetchScalarGridSpec(
        num_scalar_prefetch=1,
        grid=(bsz, FOX_HEADS, nq),
        in_specs=[
            pl.BlockSpec((None, None, None, LANE, tile), lambda b, h, i, c: (b, h, i, 0, 0)),
            pl.BlockSpec((None, seq, LANE), lambda b, h, i, c: (b, 0, h)),
            pl.BlockSpec((None, None, nq, FOX_V_ROWS, tile), lambda b, h, i, c: (b, h, 0, 0, 0)),
            pl.BlockSpec((None, None, None, 1, tile), lambda b, h, i, c: (b, h, i, 0, 0)),
        ],
        out_specs=pl.BlockSpec((None, tile, LANE), lambda b, h, i, c: (b, i, h)),
        scratch_shapes=[pltpu.VMEM((tile, tile), F32), pltpu.VMEM((tile, tile), F32),
                        pltpu.VMEM((1, tile), F32), pltpu.VMEM((FOX_V_ROWS, tile), F32)],
    )
    return pl.pallas_call(
        functools.partial(_fox_kernel, tile=tile, nk=nq),
        grid_spec=grid_spec,
        out_shape=jax.ShapeDtypeStruct((bsz, seq, FOX_HEADS * LANE), BF16),
        compiler_params=pltpu.CompilerParams(
            dimension_semantics=("parallel", "parallel", "arbitrary"),
            vmem_limit_bytes=VMEM_LIMIT),
        name="fox_attn",
    )(c_flat, qft, kf, vft, f_rows)


def _diff_kernel(sc_ref, q1t_ref, q2t_ref, k1_ref, k2_ref, vt_ref, g_ref, o_ref,
                 s1a_ref, s1b_ref, s2a_ref, s2b_ref, m1_ref, a1_ref, m2_ref, a2_ref, bias_ref, *, tile):
    hh, i = pl.program_id(1), pl.program_id(2)
    slope = sc_ref[hh] * LOG2E
    lam = sc_ref[DIFF_HEADS]

    @pl.when(i == 0)
    def _():
        key = lax.broadcasted_iota(jnp.int32, (tile, tile), 0)
        qry = lax.broadcasted_iota(jnp.int32, (tile, tile), 1)
        bias = -slope * (key + jnp.abs(qry - key)).astype(F32)
        bias_ref[...] = jnp.where((key // CHUNK) <= (qry // CHUNK), bias, NEG)

    q1t = q1t_ref[...]
    q2t = q2t_ref[...]
    for m_ref, a_ref in ((m1_ref, a1_ref), (m2_ref, a2_ref)):
        m_ref[...] = jnp.full_like(m_ref, NEG)
        a_ref[...] = jnp.zeros_like(a_ref)
    bufs1 = (s1a_ref, s1b_ref)
    bufs2 = (s2a_ref, s2b_ref)
    qrel = lax.broadcasted_iota(jnp.int32, (1, tile), 1)

    def fill(slot, j):
        off = pl.multiple_of(j * tile, tile)
        bufs1[slot][...] = jnp.dot(k1_ref[pl.ds(off, tile), :], q1t, preferred_element_type=F32)
        bufs2[slot][...] = jnp.dot(k2_ref[pl.ds(off, tile), :], q2t, preferred_element_type=F32)

    def step(slot, j):
        d = slope * ((j - i) * tile - qrel).astype(F32)
        vt = vt_ref[j]
        _softmax_step(bufs1[slot], m1_ref, a1_ref, vt, d, tile)
        _softmax_step(bufs2[slot], m2_ref, a2_ref, vt, d, tile)

    def last(slot):
        half = tile // 2
        chunk_bias = lambda s, c0: s + bias_ref[:, c0:c0 + half]
        zero = jnp.zeros((1, tile), F32)
        vt = vt_ref[i]
        _softmax_step(bufs1[slot], m1_ref, a1_ref, vt, zero, tile, chunk_bias)
        _softmax_step(bufs2[slot], m2_ref, a2_ref, vt, zero, tile, chunk_bias)

    _tile_loop(i, fill, step, last)

    a1 = a1_ref[...]
    a2 = a2_ref[...]
    o1 = a1[:DIFF_V_DIM] / a1[DIFF_V_DIM:DIFF_V_DIM + 1]
    o2 = a2[:DIFF_V_DIM] / a2[DIFF_V_DIM:DIFF_V_DIM + 1]
    ob = o1 - lam * o2
    ob = ob * lax.rsqrt(jnp.mean(ob * ob, axis=0, keepdims=True) + EPS) * g_ref[...] * (1.0 - LAM_INIT)
    o_ref[...] = ob.T.astype(BF16)


def _diff_call(scal, q1t, q2t, k1, k2, vdt, g_col, tile):
    bsz, _, nq, _, _ = q1t.shape
    seq = nq * tile
    qspec = pl.BlockSpec((None, None, None, LANE, tile), lambda b, h, i, c: (b, h, i, 0, 0))
    kspec = pl.BlockSpec((None, seq, LANE), lambda b, h, i, c: (b, 0, h))
    score = pltpu.VMEM((tile, tile), F32)
    grid_spec = pltpu.PrefetchScalarGridSpec(
        num_scalar_prefetch=1,
        grid=(bsz, DIFF_HEADS, nq),
        in_specs=[qspec, qspec, kspec, kspec,
                  pl.BlockSpec((None, None, nq, DIFF_V_ROWS, tile), lambda b, h, i, c: (b, h, 0, 0, 0)),
                  pl.BlockSpec((DIFF_V_DIM, 1), lambda b, h, i, c: (0, 0))],
        out_specs=pl.BlockSpec((None, tile, LANE), lambda b, h, i, c: (b, i, h)),
        scratch_shapes=[score, score, score, score,
                        pltpu.VMEM((1, tile), F32), pltpu.VMEM((DIFF_V_ROWS, tile), F32),
                        pltpu.VMEM((1, tile), F32), pltpu.VMEM((DIFF_V_ROWS, tile), F32),
                        score],
    )
    return pl.pallas_call(
        functools.partial(_diff_kernel, tile=tile),
        grid_spec=grid_spec,
        out_shape=jax.ShapeDtypeStruct((bsz, seq, DIFF_HEADS * LANE), BF16),
        compiler_params=pltpu.CompilerParams(
            dimension_semantics=("arbitrary", "arbitrary", "arbitrary"),
            vmem_limit_bytes=VMEM_LIMIT),
        name="diff_attn",
    )(scal, q1t, q2t, k1, k2, vdt, g_col)


def _merge_kernel(x_ref, oa_ref, ob_ref, gmix_ref, wg_ref, bg_ref, wpa_ref, wpb_ref, wo_ref,
                  gmoe_ref, wrh_ref, wrl_ref, br_ref,
                  x1_ref, h2_ref, mi_ref, mf_ref, cnt_ref, run_ref, *, bm):
    i = pl.program_id(0)

    @pl.when(i == 0)
    def _():
        run_ref[...] = jnp.zeros_like(run_ref)

    x = x_ref[...]
    h = _rms(x, gmix_ref[...]).astype(BF16)
    gates = jnp.dot(h, wg_ref[...], preferred_element_type=F32) + bg_ref[...]
    gates = 1.0 / (1.0 + jnp.exp(-gates))
    ya = jnp.dot(oa_ref[...], wpa_ref[...], preferred_element_type=F32)
    yb = jnp.dot(ob_ref[...], wpb_ref[...], preferred_element_type=F32)
    y = gates[:, :D_MODEL] * ya + gates[:, D_MODEL:] * yb
    x1 = x + jnp.dot(y.astype(BF16), wo_ref[...], preferred_element_type=F32)
    x1_ref[...] = x1
    h2 = _rms(x1, gmoe_ref[...])
    h2_ref[...] = h2

    h2h = h2.astype(BF16)
    h2l = (h2 - h2h.astype(F32)).astype(BF16)
    r = (jnp.dot(h2h, wrh_ref[...], preferred_element_type=F32)
         + jnp.dot(h2l, wrh_ref[...], preferred_element_type=F32)
         + jnp.dot(h2h, wrl_ref[...], preferred_element_type=F32)) + br_ref[...]
    el_all = r[:, :LANE]
    gl = r[:, LANE:]

    lane_i = lax.broadcasted_iota(jnp.int32, (bm, LANE), 1)
    lane = lane_i.astype(F32)
    big = float(LANE)

    def first_argmax(vals):
        vmax = jnp.max(vals, axis=1, keepdims=True)
        idx = jnp.min(jnp.where(vals == vmax, lane, big), axis=1, keepdims=True)
        return vmax, idx

    glm = jnp.where(lane_i < N_GROUPS, gl, NEG)
    gmax, g_idx = first_argmax(glm)
    g_w = 1.0 / jnp.sum(jnp.exp(glm - gmax), axis=1, keepdims=True)

    in_group = ((lane_i >> 3).astype(F32) == g_idx) & (lane_i < N_EXPERTS)
    elm = jnp.where(in_group, el_all, NEG)
    e1, idx1 = first_argmax(elm)
    elm2 = jnp.where(lane == idx1, NEG, elm)
    e2, idx2 = first_argmax(elm2)
    t = jnp.exp(e2 - e1)
    w1 = g_w / (1.0 + t)
    w2 = g_w * t / (1.0 + t)

    oh1 = (lane == idx1).astype(F32)
    oh2 = (lane == idx2).astype(F32)
    oh = oh1 + oh2
    row = lax.broadcasted_iota(jnp.int32, (bm, bm), 0)
    col = lax.broadcasted_iota(jnp.int32, (bm, bm), 1)
    strict = (col < row).astype(BF16)
    before = jnp.dot(strict, oh.astype(BF16), preferred_element_type=F32) + run_ref[...]
    rank1 = jnp.sum(oh1 * before, axis=1, keepdims=True)
    rank2 = jnp.sum(oh2 * before, axis=1, keepdims=True)
    run_ref[...] = run_ref[...] + jnp.sum(oh, axis=0, keepdims=True)
    cnt_ref[...] = jnp.broadcast_to(run_ref[...], cnt_ref.shape)

    meta = jnp.where(lane_i == 0, idx1, jnp.where(lane_i == 1, idx2,
                     jnp.where(lane_i == 2, rank1, jnp.where(lane_i == 3, rank2, 0.0))))
    mi_ref[...] = meta.astype(jnp.int32)
    mf_ref[...] = jnp.where(lane_i == 0, w1, jnp.where(lane_i == 1, w2, 0.0))


def _merge_call(x2, oa, ob, g_mix, w_g, b_g, w_pa, w_pb, w_o, g_moe, wr_hi, wr_lo, b_r):
    n = x2.shape[0]
    bm = MERGE_ROWS
    row = lambda w: pl.BlockSpec((bm, w), lambda i: (i, 0))
    full = lambda a: pl.BlockSpec(a.shape, lambda i: (0,) * a.ndim)
    consts = (g_mix, w_g, b_g, w_pa, w_pb, w_o, g_moe, wr_hi, wr_lo, b_r)
    return pl.pallas_call(
        functools.partial(_merge_kernel, bm=bm),
        grid=(n // bm,),
        in_specs=[row(D_MODEL), row(oa.shape[1]), row(ob.shape[1])] + [full(a) for a in consts],
        out_specs=(row(D_MODEL), row(D_MODEL), row(LANE), row(LANE),
                   pl.BlockSpec((8, LANE), lambda i: (0, 0))),
        out_shape=(jax.ShapeDtypeStruct((n, D_MODEL), F32),
                   jax.ShapeDtypeStruct((n, D_MODEL), F32),
                   jax.ShapeDtypeStruct((n, LANE), jnp.int32),
                   jax.ShapeDtypeStruct((n, LANE), F32),
                   jax.ShapeDtypeStruct((8, LANE), F32)),
        scratch_shapes=[pltpu.VMEM((1, LANE), F32)],
        compiler_params=pltpu.CompilerParams(dimension_semantics=("arbitrary",),
                                             vmem_limit_bytes=VMEM_LIMIT),
        name="merge_router",
    )(x2, oa, ob, *consts)


SUBLANES = 8


def _row_copy(src_ref, src_row, dst_ref, dst_row, sem):
    return pltpu.make_async_copy(src_ref.at[pl.ds(src_row, 1), :],
                                 dst_ref.at[pl.ds(dst_row, 1), :], sem)


def _dispatch_kernel(d0_ref, d1_ref, h_ref, xs_in_ref, xs_ref, sem, *, bt):
    del xs_in_ref
    i = pl.program_id(0)

    def issue(g, carry):
        base = pl.multiple_of(g * SUBLANES, SUBLANES)
        rows = h_ref.at[pl.ds(base, SUBLANES), :]
        for u in range(SUBLANES):
            t = i * bt + base + u
            _row_copy(rows, u, xs_ref, d0_ref[t], sem).start()
            _row_copy(rows, u, xs_ref, d1_ref[t], sem).start()
        return carry

    lax.fori_loop(0, bt // SUBLANES, issue, 0)

    for _ in range(2):
        pltpu.make_async_copy(h_ref, xs_ref.at[pl.ds(0, bt), :], sem).wait()


def _dispatch_call(dest0, dest1, h2, p_rows):
    n, d = h2.shape
    bt = MOVE_ROWS
    grid_spec = pltpu.PrefetchScalarGridSpec(
        num_scalar_prefetch=2,
        grid=(n // bt,),
        in_specs=[pl.BlockSpec((bt, d), lambda i, d0, d1: (i, 0)),
                  pl.BlockSpec(memory_space=pl.ANY)],
        out_specs=pl.BlockSpec(memory_space=pl.ANY),
        scratch_shapes=[pltpu.SemaphoreType.DMA(())],
    )
    return pl.pallas_call(
        functools.partial(_dispatch_kernel, bt=bt),
        grid_spec=grid_spec,
        out_shape=jax.ShapeDtypeStruct((p_rows, d), h2.dtype),
        input_output_aliases={3: 0},
        compiler_params=pltpu.CompilerParams(dimension_semantics=("arbitrary",),
                                             vmem_limit_bytes=VMEM_LIMIT),
        name="dispatch",
    )(dest0, dest1, h2, jnp.zeros((p_rows, d), h2.dtype))


def _expert_kernel(blk_e_ref, used_ref, x_ref, w1_ref, w3_ref, w2_ref, y_ref, w1b, w3b, w2b):
    i = pl.program_id(0)
    live = i < used_ref[0]
    fresh = (i == 0) | (blk_e_ref[i] != blk_e_ref[jnp.maximum(i - 1, 0)])

    @pl.when(live & fresh)
    def _():
        w1b[...] = w1_ref[...].astype(BF16)
        w3b[...] = w3_ref[...].astype(BF16)
        w2b[...] = w2_ref[...].astype(BF16)

    @pl.when(live)
    def _():
        xb = x_ref[...].astype(BF16)
        a = jnp.dot(xb, w1b[...], preferred_element_type=F32)
        g = jnp.dot(xb, w3b[...], preferred_element_type=F32)
        mid = (a / (1.0 + jnp.exp(-a))) * g
        y_ref[...] = jnp.dot(mid.astype(BF16), w2b[...], preferred_element_type=F32)

    @pl.when(i >= used_ref[0])
    def _():
        y_ref[...] = jnp.zeros_like(y_ref)


def _expert_call(blk_e, n_used, xs, w1, w3, w2):
    p_rows, d = xs.shape
    blk = MOE_ROWS
    x_idx = lambda i, be, nu: (jnp.minimum(i, nu[0] - 1), 0)
    grid_spec = pltpu.PrefetchScalarGridSpec(
        num_scalar_prefetch=2,
        grid=(p_rows // blk,),
        in_specs=[pl.BlockSpec((blk, d), x_idx),
                  pl.BlockSpec((None, d, D_EXPERT), lambda i, be, nu: (be[i], 0, 0)),
                  pl.BlockSpec((None, d, D_EXPERT), lambda i, be, nu: (be[i], 0, 0)),
                  pl.BlockSpec((None, D_EXPERT, d), lambda i, be, nu: (be[i], 0, 0))],
        out_specs=pl.BlockSpec((blk, d), lambda i, be, nu: (i, 0)),
        scratch_shapes=[pltpu.VMEM((d, D_EXPERT), BF16), pltpu.VMEM((d, D_EXPERT), BF16),
                        pltpu.VMEM((D_EXPERT, d), BF16)],
    )
    return pl.pallas_call(
        _expert_kernel,
        grid_spec=grid_spec,
        out_shape=jax.ShapeDtypeStruct((p_rows, d), F32),
        compiler_params=pltpu.CompilerParams(dimension_semantics=("arbitrary",),
                                             vmem_limit_bytes=VMEM_LIMIT),
        name="experts",
    )(blk_e, n_used, xs, w1, w3, w2)


def _combine_kernel(d0_ref, d1_ref, x1_ref, mf_ref, g_ref, yb_ref, o_ref, buf0, buf1, sem, *, bt):
    i = pl.program_id(0)

    def issue(g, carry):
        base = pl.multiple_of(g * SUBLANES, SUBLANES)
        rows0 = buf0.at[pl.ds(base, SUBLANES), :]
        rows1 = buf1.at[pl.ds(base, SUBLANES), :]
        for u in range(SUBLANES):
            t = i * bt + base + u
            _row_copy(yb_ref, d0_ref[t], rows0, u, sem).start()
            _row_copy(yb_ref, d1_ref[t], rows1, u, sem).start()
        return carry

    lax.fori_loop(0, bt // SUBLANES, issue, 0)

    for buf in (buf0, buf1):
        pltpu.make_async_copy(yb_ref.at[pl.ds(0, bt), :], buf, sem).wait()

    mf = mf_ref[...]
    x2 = x1_ref[...] + mf[:, 0:1] * buf0[...] + mf[:, 1:2] * buf1[...]
    o_ref[...] = _rms(x2, g_ref[...])


def _combine_call(dest0, dest1, x1, mf, g_final, yb):
    n, d = x1.shape
    bt = MOVE_ROWS
    grid_spec = pltpu.PrefetchScalarGridSpec(
        num_scalar_prefetch=2,
        grid=(n // bt,),
        in_specs=[pl.BlockSpec((bt, d), lambda i, d0, d1: (i, 0)),
                  pl.BlockSpec((bt, LANE), lambda i, d0, d1: (i, 0)),
                  pl.BlockSpec((1, d), lambda i, d0, d1: (0, 0)),
                  pl.BlockSpec(memory_space=pl.ANY)],
        out_specs=pl.BlockSpec((bt, d), lambda i, d0, d1: (i, 0)),
        scratch_shapes=[pltpu.VMEM((bt, d), F32), pltpu.VMEM((bt, d), F32),
                        pltpu.SemaphoreType.DMA(())],
    )
    return pl.pallas_call(
        functools.partial(_combine_kernel, bt=bt),
        grid_spec=grid_spec,
        out_shape=jax.ShapeDtypeStruct((n, d), F32),
        compiler_params=pltpu.CompilerParams(dimension_semantics=("arbitrary",),
                                             vmem_limit_bytes=VMEM_LIMIT),
        name="combine",
    )(dest0, dest1, x1, mf, g_final, yb)


def _pad_heads(w, heads, width, padded, scale=1.0):
    d = w.shape[0]
    w = (w * scale).reshape(d, heads, width)
    return jnp.pad(w, ((0, 0), (0, 0), (0, padded - width))).reshape(d, heads * padded)


def _build_proj_params(w_in, b_fgate):
    fw, dw = FOX_HEADS * FOX_HEAD_DIM, DIFF_HEADS * DIFF_V_DIM
    o = 0
    fq, fk, fv = (w_in[:, o + k * fw:o + (k + 1) * fw] for k in range(3))
    o += 3 * fw
    ff = w_in[:, o:o + FOX_HEADS]
    o += FOX_HEADS
    dq, dk, dv = (w_in[:, o + k * dw:o + (k + 1) * dw] for k in range(3))
    d = w_in.shape[0]
    dq = dq.reshape(d, DIFF_HEADS, 2, DIFF_HEAD_DIM)
    dk = dk.reshape(d, DIFF_HEADS, 2, DIFF_HEAD_DIM)
    fscale = FOX_HEAD_DIM ** -0.5 * LOG2E
    dscale = DIFF_HEAD_DIM ** -0.5 * LOG2E
    half = lambda t, k: t[:, :, k, :].reshape(d, DIFF_HEADS * DIFF_HEAD_DIM)
    w_t = jnp.concatenate([
        _pad_heads(fq, FOX_HEADS, FOX_HEAD_DIM, LANE, fscale),
        _pad_heads(fv, FOX_HEADS, FOX_HEAD_DIM, FOX_V_ROWS),
        _pad_heads(half(dq, 0), DIFF_HEADS, DIFF_HEAD_DIM, LANE, dscale),
        _pad_heads(half(dq, 1), DIFF_HEADS, DIFF_HEAD_DIM, LANE, dscale),
        _pad_heads(dv, DIFF_HEADS, DIFF_V_DIM, DIFF_V_ROWS),
    ], axis=1).T.astype(BF16)
    w_row = jnp.concatenate([
        _pad_heads(fk, FOX_HEADS, FOX_HEAD_DIM, LANE),
        jnp.pad(ff, ((0, 0), (0, LANE - FOX_HEADS))),
        _pad_heads(half(dk, 0), DIFF_HEADS, DIFF_HEAD_DIM, LANE),
        _pad_heads(half(dk, 1), DIFF_HEADS, DIFF_HEAD_DIM, LANE),
    ], axis=1).astype(BF16)

    slopes = [2.0 ** (-8.0 / DIFF_HEADS * (hh + 1)) for hh in range(DIFF_HEADS)]
    b = jnp.zeros((_T_ROWS,), F32)
    for hh in range(FOX_HEADS):
        b = b.at[_TQF + hh * LANE + FOX_HEAD_DIM:_TQF + hh * LANE + FOX_HEAD_DIM + 3].set(-1.0)
        b = b.at[_TVF + hh * FOX_V_ROWS + FOX_HEAD_DIM].set(1.0)
    for hh in range(DIFF_HEADS):
        for base in (_TQ1, _TQ2):
            b = b.at[base + hh * LANE + DIFF_HEAD_DIM:base + hh * LANE + DIFF_HEAD_DIM + 3].set(slopes[hh])
        b = b.at[_TVD + hh * DIFF_V_ROWS + DIFF_V_DIM].set(1.0)
    bf_pad = jnp.pad(b_fgate, (0, LANE - FOX_HEADS))[None, :]
    return w_t, b[:, None], w_row, bf_pad, slopes


def kernel(x, g_mix, w_in, b_fgate, b_gate, lam_q1, lam_k1, lam_q2, lam_k2, g_subln, w_pa, w_pb, w_o,
           g_moe, w_group, b_group, w_expert, b_expert, w1, w3, w2, g_final):
    bsz, seq, d = x.shape
    n = bsz * seq
    tile = ATT_TILE
    nk = seq // tile
    x2 = x.reshape(n, d)

    w_t, b_t, w_row, bf_pad, slopes = _build_proj_params(w_in[0], b_fgate[0])
    qft, vft, q1t, q2t, vdt, kf, k1, k2, fcol, cblk = _proj_call(x2, g_mix, w_t, b_t, w_row, bf_pad, bsz, seq)
    shp = lambda a: a.reshape(bsz, seq, a.shape[1])
    c_flat = cblk[:, 0, :FOX_HEADS].reshape(bsz, nk, FOX_HEADS).transpose(0, 2, 1).reshape(-1)
    f_rows = (fcol[:, :FOX_HEADS].reshape(bsz, nk, tile, FOX_HEADS).transpose(0, 3, 1, 2)
              .reshape(bsz, FOX_HEADS, nk, 1, tile))

    oa = _fox_call(c_flat, qft, shp(kf), vft, f_rows, tile)
    lam = (jnp.exp(jnp.sum(lam_q1[0] * lam_k1[0])) - jnp.exp(jnp.sum(lam_q2[0] * lam_k2[0])) + LAM_INIT)
    scal = jnp.concatenate([jnp.asarray(slopes, F32), lam[None].astype(F32)])
    ob = _diff_call(scal, q1t, q2t, shp(k1), shp(k2), vdt, g_subln[0][:, None], tile)

    o = 3 * FOX_HEADS * FOX_HEAD_DIM + FOX_HEADS + 3 * DIFF_HEADS * DIFF_V_DIM
    w_g = w_in[0][:, o:].astype(BF16)
    w_pa_pad = jnp.pad(w_pa[0].reshape(FOX_HEADS, FOX_HEAD_DIM, d),
                       ((0, 0), (0, LANE - FOX_HEAD_DIM), (0, 0))).reshape(FOX_HEADS * LANE, d).astype(BF16)
    w_r = jnp.zeros((d, 2 * LANE), F32)
    w_r = w_r.at[:, :N_EXPERTS].set(w_expert[0]).at[:, LANE:LANE + N_GROUPS].set(w_group[0])
    wr_hi = w_r.astype(BF16)
    wr_lo = (w_r - wr_hi.astype(F32)).astype(BF16)
    b_r = jnp.zeros((1, 2 * LANE), F32)
    b_r = b_r.at[0, :N_EXPERTS].set(b_expert[0].reshape(-1)).at[0, LANE:LANE + N_GROUPS].set(b_group[0])
    x1, h2, mi, mf, cnt = _merge_call(
        x2, oa.reshape(n, -1), ob.reshape(n, -1), g_mix, w_g, b_gate, w_pa_pad,
        w_pb[0].astype(BF16), w_o[0].astype(BF16), g_moe, wr_hi, wr_lo, b_r)

    blk = MOE_ROWS
    counts = cnt[0, :N_EXPERTS].astype(jnp.int32)
    padded = (counts + blk - 1) // blk * blk
    pend = jnp.cumsum(padded)
    pstart = pend - padded
    meta = mi[:, :4].T
    dest0 = pstart[meta[0]] + meta[2]
    dest1 = pstart[meta[1]] + meta[3]
    p_rows = n * 2 + N_EXPERTS * blk
    n_blk = p_rows // blk
    blk_pos = jnp.arange(n_blk, dtype=jnp.int32) * blk
    blk_e = jnp.minimum(jnp.sum(pend[None, :] <= blk_pos[:, None], axis=1), N_EXPERTS - 1).astype(jnp.int32)
    n_used = (pend[-1:] // blk).astype(jnp.int32)

    xs = _dispatch_call(dest0, dest1, h2, p_rows)
    yb = _expert_call(blk_e, n_used, xs, w1[0], w3[0], w2[0])
    out = _combine_call(dest0, dest1, x1, mf, g_final[None, :], yb)
    return out.reshape(bsz, seq, d)
```

```python
import functools

import jax
import jax.numpy as jnp
from jax import lax
from jax.experimental import pallas as pl
from jax.experimental.pallas import tpu as pltpu

F32 = jnp.float32
BF16 = jnp.bfloat16

D_MODEL = 1024
FOX_HEADS = 8
FOX_HEAD_DIM = 64
DIFF_HEADS = 4
DIFF_HEAD_DIM = 64
DIFF_V_DIM = 128
CHUNK = 64
N_GROUPS = 4
EXPERTS_PER_GROUP = 8
N_EXPERTS = 32
D_EXPERT = 512
EPS = 1e-6
LAM_INIT = 0.8 - 0.6 * 1.0

LANE = 128
NEG = -1e30
LOG2E = 1.4426950408889634
ATT_TILE = 512
MERGE_ROWS = 512
MOE_ROWS = 512
MOVE_ROWS = 512
VMEM_LIMIT = 56 * 1024 * 1024

FOX_V_ROWS = 80
DIFF_V_ROWS = 144

_TQF, _TVF, _TQ1 = 0, 1024, 1024 + FOX_HEADS * FOX_V_ROWS
_TQ2, _TVD = _TQ1 + 512, _TQ1 + 1024
_T_ROWS = _TVD + DIFF_HEADS * DIFF_V_ROWS
_KF, _FF, _K1, _K2, _ROW_COLS = 0, 1024, 1152, 1664, 2176


def _rms(x, g):
    return x * lax.rsqrt(jnp.mean(x * x, axis=-1, keepdims=True) + EPS) * g


def _split3(r):
    r0 = r.astype(BF16).astype(F32)
    r1 = (r - r0).astype(BF16).astype(F32)
    r2 = (r - r0 - r1).astype(BF16).astype(F32)
    return r0, r1, r2


_NT = (((1,), (1,)), ((), ()))


def _proj_kernel(x_ref, g_ref, wt_ref, bt_ref, w_ref, bf_ref,
                 qft_ref, vft_ref, q1t_ref, q2t_ref, vdt_ref, kf_ref, k1_ref, k2_ref,
                 fcol_ref, cblk_ref, carry_ref, *, bm, steps_per_seq):
    i = pl.program_id(0)
    h = _rms(x_ref[...], g_ref[...]).astype(BF16)

    zt = lax.dot_general(wt_ref[...], h, _NT, preferred_element_type=F32) + bt_ref[...]

    def heads(lo, n_heads, rows):
        return zt[lo:lo + n_heads * rows].reshape(n_heads, rows, bm).astype(BF16)

    qft_ref[...] = heads(_TQF, FOX_HEADS, LANE)
    vft_ref[...] = heads(_TVF, FOX_HEADS, FOX_V_ROWS)
    q1t_ref[...] = heads(_TQ1, DIFF_HEADS, LANE)
    q2t_ref[...] = heads(_TQ2, DIFF_HEADS, LANE)
    vdt_ref[...] = heads(_TVD, DIFF_HEADS, DIFF_V_ROWS)

    z = jnp.dot(h, w_ref[...], preferred_element_type=F32)

    zf = z[:, _FF:_K1] + bf_ref[...]
    logf = jnp.minimum(zf, 0.0) - jnp.log1p(jnp.exp(-jnp.abs(zf)))

    @pl.when(i % steps_per_seq == 0)
    def _():
        carry_ref[...] = jnp.zeros_like(carry_ref)

    c = carry_ref[...]
    row = lax.broadcasted_iota(jnp.int32, (bm, bm), 0)
    col = lax.broadcasted_iota(jnp.int32, (bm, bm), 1)
    tri = (col <= row).astype(BF16)
    l0, l1, l2 = _split3(logf)
    rel = (jnp.dot(tri, l0.astype(BF16), preferred_element_type=F32)
           + jnp.dot(tri, l1.astype(BF16), preferred_element_type=F32)
           + jnp.dot(tri, l2.astype(BF16), preferred_element_type=F32))
    fcum = rel + c
    carry_ref[...] = fcum[bm - 1:bm, :]
    fcol_ref[...] = fcum
    cblk_ref[...] = jnp.broadcast_to(c, cblk_ref.shape)

    lane = lax.broadcasted_iota(jnp.int32, (bm, LANE), 1)
    rel2 = rel * LOG2E
    for hh in range(FOX_HEADS):
        r0, r1, r2 = _split3(rel2[:, hh:hh + 1])
        aug = jnp.where(lane == FOX_HEAD_DIM, r0,
                        jnp.where(lane == FOX_HEAD_DIM + 1, r1,
                                  jnp.where(lane == FOX_HEAD_DIM + 2, r2, 0.0)))
        lo = _KF + hh * LANE
        kf_ref[:, hh * LANE:(hh + 1) * LANE] = (z[:, lo:lo + LANE] + aug).astype(BF16)

    width = DIFF_HEADS * LANE
    jrel = lax.broadcasted_iota(jnp.int32, (bm, width), 0)
    lane4 = lax.broadcasted_iota(jnp.int32, (bm, width), 1) & (LANE - 1)
    j0, j1, j2 = _split3(jrel.astype(F32) * LOG2E)
    augk = jnp.where(lane4 == DIFF_HEAD_DIM, j0,
                     jnp.where(lane4 == DIFF_HEAD_DIM + 1, j1,
                               jnp.where(lane4 == DIFF_HEAD_DIM + 2, j2, 0.0)))
    k1_ref[...] = (z[:, _K1:_K2] + augk).astype(BF16)
    k2_ref[...] = (z[:, _K2:_ROW_COLS] + augk).astype(BF16)


def _proj_call(x2, g_mix, w_t, b_t, w_row, bf_pad, bsz, seq):
    n = x2.shape[0]
    bm = ATT_TILE
    steps = n // bm
    spp = seq // bm
    row = lambda w: pl.BlockSpec((bm, w), lambda i: (i, 0))
    once = lambda a: pl.BlockSpec(a.shape, lambda i: (0,) * a.ndim, pipeline_mode=pl.Buffered(1))
    tshape = lambda heads, rows: jax.ShapeDtypeStruct((bsz, heads, spp, rows, bm), BF16)
    tspec = lambda heads, rows: pl.BlockSpec((None, heads, None, rows, bm),
                                             lambda i: (i // spp, 0, i % spp, 0, 0))
    out_shape = (
        tshape(FOX_HEADS, LANE), tshape(FOX_HEADS, FOX_V_ROWS),
        tshape(DIFF_HEADS, LANE), tshape(DIFF_HEADS, LANE), tshape(DIFF_HEADS, DIFF_V_ROWS),
        jax.ShapeDtypeStruct((n, FOX_HEADS * LANE), BF16),
        jax.ShapeDtypeStruct((n, DIFF_HEADS * LANE), BF16),
        jax.ShapeDtypeStruct((n, DIFF_HEADS * LANE), BF16),
        jax.ShapeDtypeStruct((n, LANE), F32),
        jax.ShapeDtypeStruct((steps, 8, LANE), F32),
    )
    out_specs = (
        tspec(FOX_HEADS, LANE), tspec(FOX_HEADS, FOX_V_ROWS),
        tspec(DIFF_HEADS, LANE), tspec(DIFF_HEADS, LANE), tspec(DIFF_HEADS, DIFF_V_ROWS),
        row(FOX_HEADS * LANE), row(DIFF_HEADS * LANE), row(DIFF_HEADS * LANE), row(LANE),
        pl.BlockSpec((None, 8, LANE), lambda i: (i, 0, 0)),
    )
    return pl.pallas_call(
        functools.partial(_proj_kernel, bm=bm, steps_per_seq=spp),
        grid=(steps,),
        in_specs=[row(D_MODEL), once(g_mix), once(w_t), once(b_t), once(w_row), once(bf_pad)],
        out_specs=out_specs,
        out_shape=out_shape,
        scratch_shapes=[pltpu.VMEM((1, LANE), F32)],
        compiler_params=pltpu.CompilerParams(dimension_semantics=("arbitrary",),
                                             vmem_limit_bytes=VMEM_LIMIT),
        name="proj",
    )(x2, g_mix, w_t, b_t, w_row, bf_pad)


def _softmax_step(s_ref, m_ref, acc_ref, vt, d, tile, fix=None):
    half = tile // 2
    for c0 in (0, half):
        cols = slice(c0, c0 + half)
        s = s_ref[:, cols]
        if fix is not None:
            s = fix(s, c0)
        dc = d[:, cols]
        m_old = m_ref[:, cols]
        m_new = jnp.maximum(m_old, jnp.max(s, axis=0, keepdims=True) + dc)
        p = jnp.exp2(s - (m_new - dc))
        acc_ref[:, cols] = (jnp.exp2(m_old - m_new) * acc_ref[:, cols]
                            + jnp.dot(vt, p.astype(BF16), preferred_element_type=F32))
        m_ref[:, cols] = m_new


def _tile_loop(i, fill, step, last):
    fill(0, 0)

    def quad(q, carry):
        j = 4 * q
        for u in range(4):
            fill((u + 1) % 2, j + u + 1)
            step(u % 2, j + u)
        return carry

    lax.fori_loop(0, i // 4, quad, 0)
    done = (i // 4) * 4

    def pair(jj, carry):
        j = done + 2 * jj
        fill(1, j + 1)
        step(0, j)
        fill(0, j + 2)
        step(1, j + 1)
        return carry

    lax.fori_loop(0, (i % 4) // 2, pair, 0)

    @pl.when(i % 2 == 1)
    def _():
        fill(1, i)
        step(0, i - 1)
        last(1)

    @pl.when(i % 2 == 0)
    def _():
        last(0)


def _fox_kernel(c_ref, qt_ref, k_ref, vt_ref, f_ref, o_ref, sa_ref, sb_ref, m_ref, acc_ref, *, tile, nk):
    b, hh, i = pl.program_id(0), pl.program_id(1), pl.program_id(2)
    qt = qt_ref[...]
    fi = f_ref[...] * LOG2E
    m_ref[...] = jnp.full_like(m_ref, NEG)
    acc_ref[...] = jnp.zeros_like(acc_ref)
    base = (b * FOX_HEADS + hh) * nk
    bufs = (sa_ref, sb_ref)

    def fill(slot, j):
        off = pl.multiple_of(j * tile, tile)
        bufs[slot][...] = jnp.dot(k_ref[pl.ds(off, tile), :], qt, preferred_element_type=F32)

    def step(slot, j):
        _softmax_step(bufs[slot], m_ref, acc_ref, vt_ref[j], fi - c_ref[base + j] * LOG2E, tile)

    def last(slot):
        key = lax.broadcasted_iota(jnp.int32, (tile, tile // 2), 0)
        qry = lax.broadcasted_iota(jnp.int32, (tile, tile // 2), 1)
        causal = lambda s, c0: jnp.where(key <= qry + c0, s, NEG)
        _softmax_step(bufs[slot], m_ref, acc_ref, vt_ref[i], fi - c_ref[base + i] * LOG2E, tile, causal)

    _tile_loop(i, fill, step, last)

    acc = acc_ref[...]
    o = acc[:FOX_HEAD_DIM] / acc[FOX_HEAD_DIM:FOX_HEAD_DIM + 1]
    o = jnp.concatenate([o, jnp.zeros((LANE - FOX_HEAD_DIM, tile), F32)], axis=0)
    o_ref[...] = o.T.astype(BF16)


def _fox_call(c_flat, qft, kf, vft, f_rows, tile):
    bsz, _, nq, _, _ = qft.shape
    seq = nq * tile
    grid_spec = pltpu.PrefetchScalarGridSpec(
        num_scalar_prefetch=1,
        grid=(bsz, FOX_HEADS, nq),
        in_specs=[
            pl.BlockSpec((None, None, None, LANE, tile), lambda b, h, i, c: (b, h, i, 0, 0)),
            pl.BlockSpec((None, seq, LANE), lambda b, h, i, c: (b, 0, h)),
            pl.BlockSpec((None, None, nq, FOX_V_ROWS, tile), lambda b, h, i, c: (b, h, 0, 0, 0)),
            pl.BlockSpec((None, None, None, 1, tile), lambda b, h, i, c: (b, h, i, 0, 0)),
        ],
        out_specs=pl.BlockSpec((None, tile, LANE), lambda b, h, i, c: (b, i, h)),
        scratch_shapes=[pltpu.VMEM((tile, tile), F32), pltpu.VMEM((tile, tile), F32),
                        pltpu.VMEM((1, tile), F32), pltpu.VMEM((FOX_V_ROWS, tile), F32)],
    )
    return pl.pallas_call(
        functools.partial(_fox_kernel, tile=tile, nk=nq),
        grid_spec=grid_spec,
        out_shape=jax.ShapeDtypeStruct((bsz, seq, FOX_HEADS * LANE), BF16),
        compiler_params=pltpu.CompilerParams(
            dimension_semantics=("parallel", "parallel", "arbitrary"),
            vmem_limit_bytes=VMEM_LIMIT),
        name="fox_attn",
    )(c_flat, qft, kf, vft, f_rows)


def _diff_kernel(sc_ref, q1t_ref, q2t_ref, k1_ref, k2_ref, vt_ref, g_ref, o_ref,
                 s1a_ref, s1b_ref, s2a_ref, s2b_ref, m1_ref, a1_ref, m2_ref, a2_ref, bias_ref, *, tile):
    hh, i = pl.program_id(1), pl.program_id(2)
    slope = sc_ref[hh] * LOG2E
    lam = sc_ref[DIFF_HEADS]

    @pl.when(i == 0)
    def _():
        key = lax.broadcasted_iota(jnp.int32, (tile, tile), 0)
        qry = lax.broadcasted_iota(jnp.int32, (tile, tile), 1)
        bias = -slope * (key + jnp.abs(qry - key)).astype(F32)
        bias_ref[...] = jnp.where((key // CHUNK) <= (qry // CHUNK), bias, NEG)

    q1t = q1t_ref[...]
    q2t = q2t_ref[...]
    for m_ref, a_ref in ((m1_ref, a1_ref), (m2_ref, a2_ref)):
        m_ref[...] = jnp.full_like(m_ref, NEG)
        a_ref[...] = jnp.zeros_like(a_ref)
    bufs1 = (s1a_ref, s1b_ref)
    bufs2 = (s2a_ref, s2b_ref)
    qrel = lax.broadcasted_iota(jnp.int32, (1, tile), 1)

    def fill(slot, j):
        off = pl.multiple_of(j * tile, tile)
        bufs1[slot][...] = jnp.dot(k1_ref[pl.ds(off, tile), :], q1t, preferred_element_type=F32)
        bufs2[slot][...] = jnp.dot(k2_ref[pl.ds(off, tile), :], q2t, preferred_element_type=F32)

    def step(slot, j):
        d = slope * ((j - i) * tile - qrel).astype(F32)
        vt = vt_ref[j]
        _softmax_step(bufs1[slot], m1_ref, a1_ref, vt, d, tile)
        _softmax_step(bufs2[slot], m2_ref, a2_ref, vt, d, tile)

    def last(slot):
        half = tile // 2
        chunk_bias = lambda s, c0: s + bias_ref[:, c0:c0 + half]
        zero = jnp.zeros((1, tile), F32)
        vt = vt_ref[i]
        _softmax_step(bufs1[slot], m1_ref, a1_ref, vt, zero, tile, chunk_bias)
        _softmax_step(bufs2[slot], m2_ref, a2_ref, vt, zero, tile, chunk_bias)

    _tile_loop(i, fill, step, last)

    a1 = a1_ref[...]
    a2 = a2_ref[...]
    o1 = a1[:DIFF_V_DIM] / a1[DIFF_V_DIM:DIFF_V_DIM + 1]
    o2 = a2[:DIFF_V_DIM] / a2[DIFF_V_DIM:DIFF_V_DIM + 1]
    ob = o1 - lam * o2
    ob = ob * lax.rsqrt(jnp.mean(ob * ob, axis=0, keepdims=True) + EPS) * g_ref[...] * (1.0 - LAM_INIT)
    o_ref[...] = ob.T.astype(BF16)


def _diff_call(scal, q1t, q2t, k1, k2, vdt, g_col, tile):
    bsz, _, nq, _, _ = q1t.shape
    seq = nq * tile
    qspec = pl.BlockSpec((None, None, None, LANE, tile), lambda b, h, i, c: (b, h, i, 0, 0))
    kspec = pl.BlockSpec((None, seq, LANE), lambda b, h, i, c: (b, 0, h))
    score = pltpu.VMEM((tile, tile), F32)
    grid_spec = pltpu.PrefetchScalarGridSpec(
        num_scalar_prefetch=1,
        grid=(bsz, DIFF_HEADS, nq),
        in_specs=[qspec, qspec, kspec, kspec,
                  pl.BlockSpec((None, None, nq, DIFF_V_ROWS, tile), lambda b, h, i, c: (b, h, 0, 0, 0)),
                  pl.BlockSpec((DIFF_V_DIM, 1), lambda b, h, i, c: (0, 0))],
        out_specs=pl.BlockSpec((None, tile, LANE), lambda b, h, i, c: (b, i, h)),
        scratch_shapes=[score, score, score, score,
                        pltpu.VMEM((1, tile), F32), pltpu.VMEM((DIFF_V_ROWS, tile), F32),
                        pltpu.VMEM((1, tile), F32), pltpu.VMEM((DIFF_V_ROWS, tile), F32),
                        score],
    )
    return pl.pallas_call(
        functools.partial(_diff_kernel, tile=tile),
        grid_spec=grid_spec,
        out_shape=jax.ShapeDtypeStruct((bsz, seq, DIFF_HEADS * LANE), BF16),
        compiler_params=pltpu.CompilerParams(
            dimension_semantics=("arbitrary", "arbitrary", "arbitrary"),
            vmem_limit_bytes=VMEM_LIMIT),
        name="diff_attn",
    )(scal, q1t, q2t, k1, k2, vdt, g_col)


def _merge_kernel(x_ref, oa_ref, ob_ref, gmix_ref, wg_ref, bg_ref, wpa_ref, wpb_ref, wo_ref,
                  gmoe_ref, wrh_ref, wrl_ref, br_ref,
                  x1_ref, h2_ref, mi_ref, mf_ref, cnt_ref, run_ref, *, bm):
    i = pl.program_id(0)

    @pl.when(i == 0)
    def _():
        run_ref[...] = jnp.zeros_like(run_ref)

    x = x_ref[...]
    h = _rms(x, gmix_ref[...]).astype(BF16)
    gates = jnp.dot(h, wg_ref[...], preferred_element_type=F32) + bg_ref[...]
    gates = 1.0 / (1.0 + jnp.exp(-gates))
    ya = jnp.dot(oa_ref[...], wpa_ref[...], preferred_element_type=F32)
    yb = jnp.dot(ob_ref[...], wpb_ref[...], preferred_element_type=F32)
    y = gates[:, :D_MODEL] * ya + gates[:, D_MODEL:] * yb
    x1 = x + jnp.dot(y.astype(BF16), wo_ref[...], preferred_element_type=F32)
    x1_ref[...] = x1
    h2 = _rms(x1, gmoe_ref[...])
    h2_ref[...] = h2

    h2h = h2.astype(BF16)
    h2l = (h2 - h2h.astype(F32)).astype(BF16)
    r = (jnp.dot(h2h, wrh_ref[...], preferred_element_type=F32)
         + jnp.dot(h2l, wrh_ref[...], preferred_element_type=F32)
         + jnp.dot(h2h, wrl_ref[...], preferred_element_type=F32)) + br_ref[...]
    el_all = r[:, :LANE]
    gl = r[:, LANE:]

    lane_i = lax.broadcasted_iota(jnp.int32, (bm, LANE), 1)
    lane = lane_i.astype(F32)
    big = float(LANE)

    def first_argmax(vals):
        vmax = jnp.max(vals, axis=1, keepdims=True)
        idx = jnp.min(jnp.where(vals == vmax, lane, big), axis=1, keepdims=True)
        return vmax, idx

    glm = jnp.where(lane_i < N_GROUPS, gl, NEG)
    gmax, g_idx = first_argmax(glm)
    g_w = 1.0 / jnp.sum(jnp.exp(glm - gmax), axis=1, keepdims=True)

    in_group = ((lane_i >> 3).astype(F32) == g_idx) & (lane_i < N_EXPERTS)
    elm = jnp.where(in_group, el_all, NEG)
    e1, idx1 = first_argmax(elm)
    elm2 = jnp.where(lane == idx1, NEG, elm)
    e2, idx2 = first_argmax(elm2)
    t = jnp.exp(e2 - e1)
    w1 = g_w / (1.0 + t)
    w2 = g_w * t / (1.0 + t)

    oh1 = (lane == idx1).astype(F32)
    oh2 = (lane == idx2).astype(F32)
    oh = oh1 + oh2
    row = lax.broadcasted_iota(jnp.int32, (bm, bm), 0)
    col = lax.broadcasted_iota(jnp.int32, (bm, bm), 1)
    strict = (col < row).astype(BF16)
    before = jnp.dot(strict, oh.astype(BF16), preferred_element_type=F32) + run_ref[...]
    rank1 = jnp.sum(oh1 * before, axis=1, keepdims=True)
    rank2 = jnp.sum(oh2 * before, axis=1, keepdims=True)
    run_ref[...] = run_ref[...] + jnp.sum(oh, axis=0, keepdims=True)
    cnt_ref[...] = jnp.broadcast_to(run_ref[...], cnt_ref.shape)

    meta = jnp.where(lane_i == 0, idx1, jnp.where(lane_i == 1, idx2,
                     jnp.where(lane_i == 2, rank1, jnp.where(lane_i == 3, rank2, 0.0))))
    mi_ref[...] = meta.astype(jnp.int32)
    mf_ref[...] = jnp.where(lane_i == 0, w1, jnp.where(lane_i == 1, w2, 0.0))


def _merge_call(x2, oa, ob, g_mix, w_g, b_g, w_pa, w_pb, w_o, g_moe, wr_hi, wr_lo, b_r):
    n = x2.shape[0]
    bm = MERGE_ROWS
    row = lambda w: pl.BlockSpec((bm, w), lambda i: (i, 0))
    full = lambda a: pl.BlockSpec(a.shape, lambda i: (0,) * a.ndim)
    consts = (g_mix, w_g, b_g, w_pa, w_pb, w_o, g_moe, wr_hi, wr_lo, b_r)
    return pl.pallas_call(
        functools.partial(_merge_kernel, bm=bm),
        grid=(n // bm,),
        in_specs=[row(D_MODEL), row(oa.shape[1]), row(ob.shape[1])] + [full(a) for a in consts],
        out_specs=(row(D_MODEL), row(D_MODEL), row(LANE), row(LANE),
                   pl.BlockSpec((8, LANE), lambda i: (0, 0))),
        out_shape=(jax.ShapeDtypeStruct((n, D_MODEL), F32),
                   jax.ShapeDtypeStruct((n, D_MODEL), F32),
                   jax.ShapeDtypeStruct((n, LANE), jnp.int32),
                   jax.ShapeDtypeStruct((n, LANE), F32),
                   jax.ShapeDtypeStruct((8, LANE), F32)),
        scratch_shapes=[pltpu.VMEM((1, LANE), F32)],
        compiler_params=pltpu.CompilerParams(dimension_semantics=("arbitrary",),
                                             vmem_limit_bytes=VMEM_LIMIT),
        name="merge_router",
    )(x2, oa, ob, *consts)


SUBLANES = 8


def _row_copy(src_ref, src_row, dst_ref, dst_row, sem):
    return pltpu.make_async_copy(src_ref.at[pl.ds(src_row, 1), :],
                                 dst_ref.at[pl.ds(dst_row, 1), :], sem)


def _dispatch_kernel(d0_ref, d1_ref, h_ref, xs_in_ref, xs_ref, sem, *, bt):
    del xs_in_ref
    i = pl.program_id(0)

    def issue(g, carry):
        base = pl.multiple_of(g * SUBLANES, SUBLANES)
        rows = h_ref.at[pl.ds(base, SUBLANES), :]
        for u in range(SUBLANES):
            t = i * bt + base + u
            _row_copy(rows, u, xs_ref, d0_ref[t], sem).start()
            _row_copy(rows, u, xs_ref, d1_ref[t], sem).start()
        return carry

    lax.fori_loop(0, bt // SUBLANES, issue, 0)

    for _ in range(2):
        pltpu.make_async_copy(h_ref, xs_ref.at[pl.ds(0, bt), :], sem).wait()


def _dispatch_call(dest0, dest1, h2, p_rows):
    n, d = h2.shape
    bt = MOVE_ROWS
    grid_spec = pltpu.PrefetchScalarGridSpec(
        num_scalar_prefetch=2,
        grid=(n // bt,),
        in_specs=[pl.BlockSpec((bt, d), lambda i, d0, d1: (i, 0)),
                  pl.BlockSpec(memory_space=pl.ANY)],
        out_specs=pl.BlockSpec(memory_space=pl.ANY),
        scratch_shapes=[pltpu.SemaphoreType.DMA(())],
    )
    return pl.pallas_call(
        functools.partial(_dispatch_kernel, bt=bt),
        grid_spec=grid_spec,
        out_shape=jax.ShapeDtypeStruct((p_rows, d), h2.dtype),
        input_output_aliases={3: 0},
        compiler_params=pltpu.CompilerParams(dimension_semantics=("arbitrary",),
                                             vmem_limit_bytes=VMEM_LIMIT),
        name="dispatch",
    )(dest0, dest1, h2, jnp.zeros((p_rows, d), h2.dtype))


def _expert_kernel(blk_e_ref, used_ref, x_ref, w1_ref, w3_ref, w2_ref, y_ref, w1b, w3b, w2b):
    i = pl.program_id(0)
    live = i < used_ref[0]
    fresh = (i == 0) | (blk_e_ref[i] != blk_e_ref[jnp.maximum(i - 1, 0)])

    @pl.when(live & fresh)
    def _():
        w1b[...] = w1_ref[...].astype(BF16)
        w3b[...] = w3_ref[...].astype(BF16)
        w2b[...] = w2_ref[...].astype(BF16)

    @pl.when(live)
    def _():
        xb = x_ref[...].astype(BF16)
        a = jnp.dot(xb, w1b[...], preferred_element_type=F32)
        g = jnp.dot(xb, w3b[...], preferred_element_type=F32)
        mid = (a / (1.0 + jnp.exp(-a))) * g
        y_ref[...] = jnp.dot(mid.astype(BF16), w2b[...], preferred_element_type=F32)

    @pl.when(i >= used_ref[0])
    def _():
        y_ref[...] = jnp.zeros_like(y_ref)


def _expert_call(blk_e, n_used, xs, w1, w3, w2):
    p_rows, d = xs.shape
    blk = MOE_ROWS
    x_idx = lambda i, be, nu: (jnp.minimum(i, nu[0] - 1), 0)
    grid_spec = pltpu.PrefetchScalarGridSpec(
        num_scalar_prefetch=2,
        grid=(p_rows // blk,),
        in_specs=[pl.BlockSpec((blk, d), x_idx),
                  pl.BlockSpec((None, d, D_EXPERT), lambda i, be, nu: (be[i], 0, 0)),
                  pl.BlockSpec((None, d, D_EXPERT), lambda i, be, nu: (be[i], 0, 0)),
                  pl.BlockSpec((None, D_EXPERT, d), lambda i, be, nu: (be[i], 0, 0))],
        out_specs=pl.BlockSpec((blk, d), lambda i, be, nu: (i, 0)),
        scratch_shapes=[pltpu.VMEM((d, D_EXPERT), BF16), pltpu.VMEM((d, D_EXPERT), BF16),
                        pltpu.VMEM((D_EXPERT, d), BF16)],
    )
    return pl.pallas_call(
        _expert_kernel,
        grid_spec=grid_spec,
        out_shape=jax.ShapeDtypeStruct((p_rows, d), F32),
        compiler_params=pltpu.CompilerParams(dimension_semantics=("arbitrary",),
                                             vmem_limit_bytes=VMEM_LIMIT),
        name="experts",
    )(blk_e, n_used, xs, w1, w3, w2)


def _combine_kernel(d0_ref, d1_ref, x1_ref, mf_ref, g_ref, yb_ref, o_ref, buf0, buf1, sem, *, bt):
    i = pl.program_id(0)

    def issue(g, carry):
        base = pl.multiple_of(g * SUBLANES, SUBLANES)
        rows0 = buf0.at[pl.ds(base, SUBLANES), :]
        rows1 = buf1.at[pl.ds(base, SUBLANES), :]
        for u in range(SUBLANES):
            t = i * bt + base + u
            _row_copy(yb_ref, d0_ref[t], rows0, u, sem).start()
            _row_copy(yb_ref, d1_ref[t], rows1, u, sem).start()
        return carry

    lax.fori_loop(0, bt // SUBLANES, issue, 0)

    for buf in (buf0, buf1):
        pltpu.make_async_copy(yb_ref.at[pl.ds(0, bt), :], buf, sem).wait()

    mf = mf_ref[...]
    x2 = x1_ref[...] + mf[:, 0:1] * buf0[...] + mf[:, 1:2] * buf1[...]
    o_ref[...] = _rms(x2, g_ref[...])


def _combine_call(dest0, dest1, x1, mf, g_final, yb):
    n, d = x1.shape
    bt = MOVE_ROWS
    grid_spec = pltpu.PrefetchScalarGridSpec(
        num_scalar_prefetch=2,
        grid=(n // bt,),
        in_specs=[pl.BlockSpec((bt, d), lambda i, d0, d1: (i, 0)),
                  pl.BlockSpec((bt, LANE), lambda i, d0, d1: (i, 0)),
                  pl.BlockSpec((1, d), lambda i, d0, d1: (0, 0)),
                  pl.BlockSpec(memory_space=pl.ANY)],
        out_specs=pl.BlockSpec((bt, d), lambda i, d0, d1: (i, 0)),
        scratch_shapes=[pltpu.VMEM((bt, d), F32), pltpu.VMEM((bt, d), F32),
                        pltpu.SemaphoreType.DMA(())],
    )
    return pl.pallas_call(
        functools.partial(_combine_kernel, bt=bt),
        grid_spec=grid_spec,
        out_shape=jax.ShapeDtypeStruct((n, d), F32),
        compiler_params=pltpu.CompilerParams(dimension_semantics=("arbitrary",),
                                             vmem_limit_bytes=VMEM_LIMIT),
        name="combine",
    )(dest0, dest1, x1, mf, g_final, yb)


def _pad_heads(w, heads, width, padded, scale=1.0):
    d = w.shape[0]
    w = (w * scale).reshape(d, heads, width)
    return jnp.pad(w, ((0, 0), (0, 0), (0, padded - width))).reshape(d, heads * padded)


def _build_proj_params(w_in, b_fgate):
    fw, dw = FOX_HEADS * FOX_HEAD_DIM, DIFF_HEADS * DIFF_V_DIM
    o = 0
    fq, fk, fv = (w_in[:, o + k * fw:o + (k + 1) * fw] for k in range(3))
    o += 3 * fw
    ff = w_in[:, o:o + FOX_HEADS]
    o += FOX_HEADS
    dq, dk, dv = (w_in[:, o + k * dw:o + (k + 1) * dw] for k in range(3))
    d = w_in.shape[0]
    dq = dq.reshape(d, DIFF_HEADS, 2, DIFF_HEAD_DIM)
    dk = dk.reshape(d, DIFF_HEADS, 2, DIFF_HEAD_DIM)
    fscale = FOX_HEAD_DIM ** -0.5 * LOG2E
    dscale = DIFF_HEAD_DIM ** -0.5 * LOG2E
    half = lambda t, k: t[:, :, k, :].reshape(d, DIFF_HEADS * DIFF_HEAD_DIM)
    w_t = jnp.concatenate([
        _pad_heads(fq, FOX_HEADS, FOX_HEAD_DIM, LANE, fscale),
        _pad_heads(fv, FOX_HEADS, FOX_HEAD_DIM, FOX_V_ROWS),
        _pad_heads(half(dq, 0), DIFF_HEADS, DIFF_HEAD_DIM, LANE, dscale),
        _pad_heads(half(dq, 1), DIFF_HEADS, DIFF_HEAD_DIM, LANE, dscale),
        _pad_heads(dv, DIFF_HEADS, DIFF_V_DIM, DIFF_V_ROWS),
    ], axis=1).T.astype(BF16)
    w_row = jnp.concatenate([
        _pad_heads(fk, FOX_HEADS, FOX_HEAD_DIM, LANE),
        jnp.pad(ff, ((0, 0), (0, LANE - FOX_HEADS))),
        _pad_heads(half(dk, 0), DIFF_HEADS, DIFF_HEAD_DIM, LANE),
        _pad_heads(half(dk, 1), DIFF_HEADS, DIFF_HEAD_DIM, LANE),
    ], axis=1).astype(BF16)

    slopes = [2.0 ** (-8.0 / DIFF_HEADS * (hh + 1)) for hh in range(DIFF_HEADS)]
    b = jnp.zeros((_T_ROWS,), F32)
    for hh in range(FOX_HEADS):
        b = b.at[_TQF + hh * LANE + FOX_HEAD_DIM:_TQF + hh * LANE + FOX_HEAD_DIM + 3].set(-1.0)
        b = b.at[_TVF + hh * FOX_V_ROWS + FOX_HEAD_DIM].set(1.0)
    for hh in range(DIFF_HEADS):
        for base in (_TQ1, _TQ2):
            b = b.at[base + hh * LANE + DIFF_HEAD_DIM:base + hh * LANE + DIFF_HEAD_DIM + 3].set(slopes[hh])
        b = b.at[_TVD + hh * DIFF_V_ROWS + DIFF_V_DIM].set(1.0)
    bf_pad = jnp.pad(b_fgate, (0, LANE - FOX_HEADS))[None, :]
    return w_t, b[:, None], w_row, bf_pad, slopes


def kernel(x, g_mix, w_in, b_fgate, b_gate, lam_q1, lam_k1, lam_q2, lam_k2, g_subln, w_pa, w_pb, w_o,
           g_moe, w_group, b_group, w_expert, b_expert, w1, w3, w2, g_final):
    bsz, seq, d = x.shape
    n = bsz * seq
    tile = ATT_TILE
    nk = seq // tile
    x2 = x.reshape(n, d)

    w_t, b_t, w_row, bf_pad, slopes = _build_proj_params(w_in[0], b_fgate[0])
    qft, vft, q1t, q2t, vdt, kf, k1, k2, fcol, cblk = _proj_call(x2, g_mix, w_t, b_t, w_row, bf_pad, bsz, seq)
    shp = lambda a: a.reshape(bsz, seq, a.shape[1])
    c_flat = cblk[:, 0, :FOX_HEADS].reshape(bsz, nk, FOX_HEADS).transpose(0, 2, 1).reshape(-1)
    f_rows = (fcol[:, :FOX_HEADS].reshape(bsz, nk, tile, FOX_HEADS).transpose(0, 3, 1, 2)
              .reshape(bsz, FOX_HEADS, nk, 1, tile))

    oa = _fox_call(c_flat, qft, shp(kf), vft, f_rows, tile)
    lam = (jnp.exp(jnp.sum(lam_q1[0] * lam_k1[0])) - jnp.exp(jnp.sum(lam_q2[0] * lam_k2[0])) + LAM_INIT)
    scal = jnp.concatenate([jnp.asarray(slopes, F32), lam[None].astype(F32)])
    ob = _diff_call(scal, q1t, q2t, shp(k1), shp(k2), vdt, g_subln[0][:, None], tile)

    o = 3 * FOX_HEADS * FOX_HEAD_DIM + FOX_HEADS + 3 * DIFF_HEADS * DIFF_V_DIM
    w_g = w_in[0][:, o:].astype(BF16)
    w_pa_pad = jnp.pad(w_pa[0].reshape(FOX_HEADS, FOX_HEAD_DIM, d),
                       ((0, 0), (0, LANE - FOX_HEAD_DIM), (0, 0))).reshape(FOX_HEADS * LANE, d).astype(BF16)
    w_r = jnp.zeros((d, 2 * LANE), F32)
    w_r = w_r.at[:, :N_EXPERTS].set(w_expert[0]).at[:, LANE:LANE + N_GROUPS].set(w_group[0])
    wr_hi = w_r.astype(BF16)
    wr_lo = (w_r - wr_hi.astype(F32)).astype(BF16)
    b_r = jnp.zeros((1, 2 * LANE), F32)
    b_r = b_r.at[0, :N_EXPERTS].set(b_expert[0].reshape(-1)).at[0, LANE:LANE + N_GROUPS].set(b_group[0])
    x1, h2, mi, mf, cnt = _merge_call(
        x2, oa.reshape(n, -1), ob.reshape(n, -1), g_mix, w_g, b_gate, w_pa_pad,
        w_pb[0].astype(BF16), w_o[0].astype(BF16), g_moe, wr_hi, wr_lo, b_r)

    blk = MOE_ROWS
    counts = cnt[0, :N_EXPERTS].astype(jnp.int32)
    padded = (counts + blk - 1) // blk * blk
    pend = jnp.cumsum(padded)
    pstart = pend - padded
    meta = mi[:, :4].T
    dest0 = pstart[meta[0]] + meta[2]
    dest1 = pstart[meta[1]] + meta[3]
    p_rows = n * 2 + N_EXPERTS * blk
    n_blk = p_rows // blk
    blk_pos = jnp.arange(n_blk, dtype=jnp.int32) * blk
    blk_e = jnp.minimum(jnp.sum(pend[None, :] <= blk_pos[:, None], axis=1), N_EXPERTS - 1).astype(jnp.int32)
    n_used = (pend[-1:] // blk).astype(jnp.int32)

    xs = _dispatch_call(dest0, dest1, h2, p_rows)
    yb = _expert_call(blk_e, n_used, xs, w1[0], w3[0], w2[0])
    out = _combine_call(dest0, dest1, x1, mf, g_final[None, :], yb)
    return out.reshape(bsz, seq, d)
```

```python
import functools

import jax
import jax.numpy as jnp
from jax import lax
from jax.experimental import pallas as pl
from jax.experimental.pallas import tpu as pltpu

F32 = jnp.float32
BF16 = jnp.bfloat16

D_MODEL = 1024
FOX_HEADS = 8
FOX_HEAD_DIM = 64
DIFF_HEADS = 4
DIFF_HEAD_DIM = 64
DIFF_V_DIM = 128
CHUNK = 64
N_GROUPS = 4
EXPERTS_PER_GROUP = 8
N_EXPERTS = 32
D_EXPERT = 512
EPS = 1e-6
LAM_INIT = 0.8 - 0.6 * 1.0

LANE = 128
SUBLANES = 8
NEG = -1e30
LOG2E = 1.4426950408889634
ATT_TILE = 512
MERGE_ROWS = 512
MOE_ROWS = 512
MOVE_ROWS = 512
VMEM_LIMIT = 56 * 1024 * 1024

FOX_V_ROWS = 80
DIFF_V_ROWS = 144

_TQF, _TVF, _TQ1 = 0, 1024, 1024 + FOX_HEADS * FOX_V_ROWS
_TQ2, _TVD = _TQ1 + 512, _TQ1 + 1024
_T_ROWS = _TVD + DIFF_HEADS * DIFF_V_ROWS
_KF, _FF, _K1, _K2, _ROW_COLS = 0, 1024, 1152, 1664, 2176


def _rms(x, g):
    return x * lax.rsqrt(jnp.mean(x * x, axis=-1, keepdims=True) + EPS) * g


def _split3(r):
    r0 = r.astype(BF16).astype(F32)
    r1 = (r - r0).astype(BF16).astype(F32)
    r2 = (r - r0 - r1).astype(BF16).astype(F32)
    return r0, r1, r2


_NT = (((1,), (1,)), ((), ()))


def _proj_kernel(x_ref, g_ref, wt_ref, bt_ref, w_ref, bf_ref,
                 qft_ref, vft_ref, q1t_ref, q2t_ref, vdt_ref, kf_ref, k1_ref, k2_ref,
                 fcol_ref, cblk_ref, carry_ref, *, bm, steps_per_seq):
    i = pl.program_id(0)
    h = _rms(x_ref[...], g_ref[...]).astype(BF16)

    zt = lax.dot_general(wt_ref[...], h, _NT, preferred_element_type=F32) + bt_ref[...]

    def heads(lo, n_heads, rows):
        return zt[lo:lo + n_heads * rows].reshape(n_heads, rows, bm).astype(BF16)

    qft_ref[...] = heads(_TQF, FOX_HEADS, LANE)
    vft_ref[...] = heads(_TVF, FOX_HEADS, FOX_V_ROWS)
    q1t_ref[...] = heads(_TQ1, DIFF_HEADS, LANE)
    q2t_ref[...] = heads(_TQ2, DIFF_HEADS, LANE)
    vdt_ref[...] = heads(_TVD, DIFF_HEADS, DIFF_V_ROWS)

    z = jnp.dot(h, w_ref[...], preferred_element_type=F32)

    zf = z[:, _FF:_K1] + bf_ref[...]
    logf = jnp.minimum(zf, 0.0) - jnp.log1p(jnp.exp(-jnp.abs(zf)))

    @pl.when(i % steps_per_seq == 0)
    def _():
        carry_ref[...] = jnp.zeros_like(carry_ref)

    c = carry_ref[...]
    row = lax.broadcasted_iota(jnp.int32, (bm, bm), 0)
    col = lax.broadcasted_iota(jnp.int32, (bm, bm), 1)
    tri = (col <= row).astype(BF16)
    l0, l1, l2 = _split3(logf)
    rel = (jnp.dot(tri, l0.astype(BF16), preferred_element_type=F32)
           + jnp.dot(tri, l1.astype(BF16), preferred_element_type=F32)
           + jnp.dot(tri, l2.astype(BF16), preferred_element_type=F32))
    fcum = rel + c
    carry_ref[...] = fcum[bm - 1:bm, :]
    fcol_ref[...] = fcum
    cblk_ref[...] = jnp.broadcast_to(c, cblk_ref.shape)

    lane = lax.broadcasted_iota(jnp.int32, (bm, LANE), 1)
    rel2 = rel * LOG2E
    for hh in range(FOX_HEADS):
        r0, r1, r2 = _split3(rel2[:, hh:hh + 1])
        aug = jnp.where(lane == FOX_HEAD_DIM, r0,
                        jnp.where(lane == FOX_HEAD_DIM + 1, r1,
                                  jnp.where(lane == FOX_HEAD_DIM + 2, r2, 0.0)))
        lo = _KF + hh * LANE
        kf_ref[:, hh * LANE:(hh + 1) * LANE] = (z[:, lo:lo + LANE] + aug).astype(BF16)

    width = DIFF_HEADS * LANE
    jrel = lax.broadcasted_iota(jnp.int32, (bm, width), 0)
    lane4 = lax.broadcasted_iota(jnp.int32, (bm, width), 1) & (LANE - 1)
    j0, j1, j2 = _split3(jrel.astype(F32) * LOG2E)
    augk = jnp.where(lane4 == DIFF_HEAD_DIM, j0,
                     jnp.where(lane4 == DIFF_HEAD_DIM + 1, j1,
                               jnp.where(lane4 == DIFF_HEAD_DIM + 2, j2, 0.0)))
    k1_ref[...] = (z[:, _K1:_K2] + augk).astype(BF16)
    k2_ref[...] = (z[:, _K2:_ROW_COLS] + augk).astype(BF16)


def _proj_call(x2, g_mix, w_t, b_t, w_row, bf_pad, bsz, seq):
    n = x2.shape[0]
    bm = ATT_TILE
    steps = n // bm
    spp = seq // bm
    row = lambda w: pl.BlockSpec((bm, w), lambda i: (i, 0))
    once = lambda a: pl.BlockSpec(a.shape, lambda i: (0,) * a.ndim, pipeline_mode=pl.Buffered(1))
    tshape = lambda heads, rows: jax.ShapeDtypeStruct((bsz, heads, spp, rows, bm), BF16)
    tspec = lambda heads, rows: pl.BlockSpec((None, heads, None, rows, bm),
                                             lambda i: (i // spp, 0, i % spp, 0, 0))
    out_shape = (
        tshape(FOX_HEADS, LANE), tshape(FOX_HEADS, FOX_V_ROWS),
        tshape(DIFF_HEADS, LANE), tshape(DIFF_HEADS, LANE), tshape(DIFF_HEADS, DIFF_V_ROWS),
        jax.ShapeDtypeStruct((n, FOX_HEADS * LANE), BF16),
        jax.ShapeDtypeStruct((n, DIFF_HEADS * LANE), BF16),
        jax.ShapeDtypeStruct((n, DIFF_HEADS * LANE), BF16),
        jax.ShapeDtypeStruct((n, LANE), F32),
        jax.ShapeDtypeStruct((steps, 8, LANE), F32),
    )
    out_specs = (
        tspec(FOX_HEADS, LANE), tspec(FOX_HEADS, FOX_V_ROWS),
        tspec(DIFF_HEADS, LANE), tspec(DIFF_HEADS, LANE), tspec(DIFF_HEADS, DIFF_V_ROWS),
        row(FOX_HEADS * LANE), row(DIFF_HEADS * LANE), row(DIFF_HEADS * LANE), row(LANE),
        pl.BlockSpec((None, 8, LANE), lambda i: (i, 0, 0)),
    )
    return pl.pallas_call(
        functools.partial(_proj_kernel, bm=bm, steps_per_seq=spp),
        grid=(steps,),
        in_specs=[row(D_MODEL), once(g_mix), once(w_t), once(b_t), once(w_row), once(bf_pad)],
        out_specs=out_specs,
        out_shape=out_shape,
        scratch_shapes=[pltpu.VMEM((1, LANE), F32)],
        compiler_params=pltpu.CompilerParams(dimension_semantics=("arbitrary",),
                                             vmem_limit_bytes=VMEM_LIMIT),
        name="proj",
    )(x2, g_mix, w_t, b_t, w_row, bf_pad)


def _softmax_step(s_ref, m_ref, acc_ref, vt, d, tile, fix=None):
    half = tile // 2
    for c0 in (0, half):
        cols = slice(c0, c0 + half)
        s = s_ref[:, cols]
        if fix is not None:
            s = fix(s, c0)
        dc = d[:, cols]
        m_old = m_ref[:, cols]
        m_new = jnp.maximum(m_old, jnp.max(s, axis=0, keepdims=True) + dc)
        p = jnp.exp2(s - (m_new - dc))
        acc_ref[:, cols] = (jnp.exp2(m_old - m_new) * acc_ref[:, cols]
                            + jnp.dot(vt, p.astype(BF16), preferred_element_type=F32))
        m_ref[:, cols] = m_new


def _tile_loop(i, fill, step, last):
    fill(0, 0)

    def run(first, trips, width):
        def body(q, carry):
            j = first + width * q
            for u in range(width):
                fill((u + 1) % 2, j + u + 1)
                step(u % 2, j + u)
            return carry

        lax.fori_loop(0, trips, body, 0)

    run(0, i // 8, 8)
    run((i // 8) * 8, (i % 8) // 4, 4)
    done = (i // 4) * 4

    def pair(jj, carry):
        j = done + 2 * jj
        fill(1, j + 1)
        step(0, j)
        fill(0, j + 2)
        step(1, j + 1)
        return carry

    lax.fori_loop(0, (i % 4) // 2, pair, 0)

    @pl.when(i % 2 == 1)
    def _():
        fill(1, i)
        step(0, i - 1)
        last(1)

    @pl.when(i % 2 == 0)
    def _():
        last(0)


def _fox_kernel(c_ref, qt_ref, k_ref, vt_ref, f_ref, o_ref, sa_ref, sb_ref, m_ref, acc_ref, *, tile, nk):
    b, hh, i = pl.program_id(0), pl.program_id(1), pl.program_id(2)
    qt = qt_ref[...]
    fi = f_ref[...] * LOG2E
    m_ref[...] = jnp.full_like(m_ref, NEG)
    acc_ref[...] = jnp.zeros_like(acc_ref)
    base = (b * FOX_HEADS + hh) * nk
    bufs = (sa_ref, sb_ref)

    def fill(slot, j):
        off = pl.multiple_of(j * tile, tile)
        bufs[slot][...] = jnp.dot(k_ref[pl.ds(off, tile), :], qt, preferred_element_type=F32)

    def step(slot, j):
        _softmax_step(bufs[slot], m_ref, acc_ref, vt_ref[j], fi - c_ref[base + j] * LOG2E, tile)

    def last(slot):
        key = lax.broadcasted_iota(jnp.int32, (tile, tile // 2), 0)
        qry = lax.broadcasted_iota(jnp.int32, (tile, tile // 2), 1)
        causal = lambda s, c0: jnp.where(key <= qry + c0, s, NEG)
        _softmax_step(bufs[slot], m_ref, acc_ref, vt_ref[i], fi - c_ref[base + i] * LOG2E, tile, causal)

    _tile_loop(i, fill, step, last)

    acc = acc_ref[...]
    o = acc[:FOX_HEAD_DIM] / acc[FOX_HEAD_DIM:FOX_HEAD_DIM + 1]
    o = jnp.concatenate([o, jnp.zeros((LANE - FOX_HEAD_DIM, tile), F32)], axis=0)
    o_ref[...] = o.T.astype(BF16)


def _fox_call(c_flat, qft, kf, vft, f_rows, tile):
    bsz, _, nq, _, _ = qft.shape
    seq = nq * tile
    grid_spec = pltpu.PrefetchScalarGridSpec(
        num_scalar_prefetch=1,
        grid=(bsz, FOX_HEADS, nq),
        in_specs=[
            pl.BlockSpec((None, None, None, LANE, tile), lambda b, h, i, c: (b, h, i, 0, 0)),
            pl.BlockSpec((None, seq, LANE), lambda b, h, i, c: (b, 0, h)),
            pl.BlockSpec((None, None, nq, FOX_V_ROWS, tile), lambda b, h, i, c: (b, h, 0, 0, 0)),
            pl.BlockSpec((None, None, None, 1, tile), lambda b, h, i, c: (b, h, i, 0, 0)),
        ],
        out_specs=pl.BlockSpec((None, tile, LANE), lambda b, h, i, c: (b, i, h)),
        scratch_shapes=[pltpu.VMEM((tile, tile), F32), pltpu.VMEM((tile, tile), F32),
                        pltpu.VMEM((1, tile), F32), pltpu.VMEM((FOX_V_ROWS, tile), F32)],
    )
    return pl.pallas_call(
        functools.partial(_fox_kernel, tile=tile, nk=nq),
        grid_spec=grid_spec,
        out_shape=jax.ShapeDtypeStruct((bsz, seq, FOX_HEADS * LANE), BF16),
        compiler_params=pltpu.CompilerParams(
            dimension_semantics=("parallel", "parallel", "arbitrary"),
            vmem_limit_bytes=VMEM_LIMIT),
        name="fox_attn",
    )(c_flat, qft, kf, vft, f_rows)


def _diff_kernel(sc_ref, q1t_ref, q2t_ref, k1_ref, k2_ref, vt_ref, g_ref, o_ref,
                 s1a_ref, s1b_ref, s2a_ref, s2b_ref, m1_ref, a1_ref, m2_ref, a2_ref, bias_ref, *, tile):
    hh, i = pl.program_id(1), pl.program_id(2)
    slope = sc_ref[hh] * LOG2E
    lam = sc_ref[DIFF_HEADS]

    @pl.when(i == 0)
    def _():
        key = lax.broadcasted_iota(jnp.int32, (tile, tile), 0)
        qry = lax.broadcasted_iota(jnp.int32, (tile, tile), 1)
        bias = -slope * (key + jnp.abs(qry - key)).astype(F32)
        bias_ref[...] = jnp.where((key // CHUNK) <= (qry // CHUNK), bias, NEG)

    q1t = q1t_ref[...]
    q2t = q2t_ref[...]
    for m_ref, a_ref in ((m1_ref, a1_ref), (m2_ref, a2_ref)):
        m_ref[...] = jnp.full_like(m_ref, NEG)
        a_ref[...] = jnp.zeros_like(a_ref)
    bufs1 = (s1a_ref, s1b_ref)
    bufs2 = (s2a_ref, s2b_ref)
    qrel = lax.broadcasted_iota(jnp.int32, (1, tile), 1)

    def fill(slot, j):
        off = pl.multiple_of(j * tile, tile)
        bufs1[slot][...] = jnp.dot(k1_ref[pl.ds(off, tile), :], q1t, preferred_element_type=F32)
        bufs2[slot][...] = jnp.dot(k2_ref[pl.ds(off, tile), :], q2t, preferred_element_type=F32)

    def step(slot, j):
        d = slope * ((j - i) * tile - qrel).astype(F32)
        vt = vt_ref[j]
        _softmax_step(bufs1[slot], m1_ref, a1_ref, vt, d, tile)
        _softmax_step(bufs2[slot], m2_ref, a2_ref, vt, d, tile)

    def last(slot):
        half = tile // 2
        chunk_bias = lambda s, c0: s + bias_ref[:, c0:c0 + half]
        zero = jnp.zeros((1, tile), F32)
        vt = vt_ref[i]
        _softmax_step(bufs1[slot], m1_ref, a1_ref, vt, zero, tile, chunk_bias)
        _softmax_step(bufs2[slot], m2_ref, a2_ref, vt, zero, tile, chunk_bias)

    _tile_loop(i, fill, step, last)

    a1 = a1_ref[...]
    a2 = a2_ref[...]
    o1 = a1[:DIFF_V_DIM] / a1[DIFF_V_DIM:DIFF_V_DIM + 1]
    o2 = a2[:DIFF_V_DIM] / a2[DIFF_V_DIM:DIFF_V_DIM + 1]
    ob = o1 - lam * o2
    ob = ob * lax.rsqrt(jnp.mean(ob * ob, axis=0, keepdims=True) + EPS) * g_ref[...] * (1.0 - LAM_INIT)
    o_ref[...] = ob.T.astype(BF16)


def _diff_call(scal, q1t, q2t, k1, k2, vdt, g_col, tile):
    bsz, _, nq, _, _ = q1t.shape
    seq = nq * tile
    qspec = pl.BlockSpec((None, None, None, LANE, tile), lambda b, h, i, c: (b, h, i, 0, 0))
    kspec = pl.BlockSpec((None, seq, LANE), lambda b, h, i, c: (b, 0, h))
    score = pltpu.VMEM((tile, tile), F32)
    grid_spec = pltpu.PrefetchScalarGridSpec(
        num_scalar_prefetch=1,
        grid=(bsz, DIFF_HEADS, nq),
        in_specs=[qspec, qspec, kspec, kspec,
                  pl.BlockSpec((None, None, nq, DIFF_V_ROWS, tile), lambda b, h, i, c: (b, h, 0, 0, 0)),
                  pl.BlockSpec((DIFF_V_DIM, 1), lambda b, h, i, c: (0, 0))],
        out_specs=pl.BlockSpec((None, tile, LANE), lambda b, h, i, c: (b, i, h)),
        scratch_shapes=[score, score, score, score,
                        pltpu.VMEM((1, tile), F32), pltpu.VMEM((DIFF_V_ROWS, tile), F32),
                        pltpu.VMEM((1, tile), F32), pltpu.VMEM((DIFF_V_ROWS, tile), F32),
                        score],
    )
    return pl.pallas_call(
        functools.partial(_diff_kernel, tile=tile),
        grid_spec=grid_spec,
        out_shape=jax.ShapeDtypeStruct((bsz, seq, DIFF_HEADS * LANE), BF16),
        compiler_params=pltpu.CompilerParams(
            dimension_semantics=("arbitrary", "arbitrary", "arbitrary"),
            vmem_limit_bytes=VMEM_LIMIT),
        name="diff_attn",
    )(scal, q1t, q2t, k1, k2, vdt, g_col)


def _merge_kernel(x_ref, oa_ref, ob_ref, gmix_ref, wg_ref, bg_ref, wpa_ref, wpb_ref, wo_ref,
                  gmoe_ref, wrh_ref, wrl_ref, br_ref,
                  x1_ref, h2_ref, mi_ref, mf_ref, cnt_ref, run_ref, *, bm):
    i = pl.program_id(0)

    @pl.when(i == 0)
    def _():
        run_ref[...] = jnp.zeros_like(run_ref)

    x = x_ref[...]
    h = _rms(x, gmix_ref[...]).astype(BF16)
    gates = jnp.dot(h, wg_ref[...], preferred_element_type=F32) + bg_ref[...]
    gates = 1.0 / (1.0 + jnp.exp(-gates))
    ya = jnp.dot(oa_ref[...], wpa_ref[...], preferred_element_type=F32)
    yb = jnp.dot(ob_ref[...], wpb_ref[...], preferred_element_type=F32)
    y = gates[:, :D_MODEL] * ya + gates[:, D_MODEL:] * yb
    x1 = x + jnp.dot(y.astype(BF16), wo_ref[...], preferred_element_type=F32)
    x1_ref[...] = x1
    h2 = _rms(x1, gmoe_ref[...])
    h2_ref[...] = h2

    h2h = h2.astype(BF16)
    h2l = (h2 - h2h.astype(F32)).astype(BF16)
    r = (jnp.dot(h2h, wrh_ref[...], preferred_element_type=F32)
         + jnp.dot(h2l, wrh_ref[...], preferred_element_type=F32)
         + jnp.dot(h2h, wrl_ref[...], preferred_element_type=F32)) + br_ref[...]
    el_all = r[:, :LANE]
    gl = r[:, LANE:]

    lane_i = lax.broadcasted_iota(jnp.int32, (bm, LANE), 1)
    lane = lane_i.astype(F32)
    big = float(LANE)

    def first_argmax(vals):
        vmax = jnp.max(vals, axis=1, keepdims=True)
        idx = jnp.min(jnp.where(vals == vmax, lane, big), axis=1, keepdims=True)
        return vmax, idx

    glm = jnp.where(lane_i < N_GROUPS, gl, NEG)
    gmax, g_idx = first_argmax(glm)
    g_w = 1.0 / jnp.sum(jnp.exp(glm - gmax), axis=1, keepdims=True)

    in_group = ((lane_i >> 3).astype(F32) == g_idx) & (lane_i < N_EXPERTS)
    elm = jnp.where(in_group, el_all, NEG)
    e1, idx1 = first_argmax(elm)
    elm2 = jnp.where(lane == idx1, NEG, elm)
    e2, idx2 = first_argmax(elm2)
    t = jnp.exp(e2 - e1)
    w1 = g_w / (1.0 + t)
    w2 = g_w * t / (1.0 + t)

    oh1 = (lane == idx1).astype(F32)
    oh2 = (lane == idx2).astype(F32)
    oh = oh1 + oh2
    row = lax.broadcasted_iota(jnp.int32, (bm, bm), 0)
    col = lax.broadcasted_iota(jnp.int32, (bm, bm), 1)
    strict = (col < row).astype(BF16)
    before = jnp.dot(strict, oh.astype(BF16), preferred_element_type=F32) + run_ref[...]
    rank1 = jnp.sum(oh1 * before, axis=1, keepdims=True)
    rank2 = jnp.sum(oh2 * before, axis=1, keepdims=True)
    run_ref[...] = run_ref[...] + jnp.sum(oh, axis=0, keepdims=True)
    cnt_ref[...] = jnp.broadcast_to(run_ref[...], cnt_ref.shape)

    meta = jnp.where(lane_i == 0, idx1, jnp.where(lane_i == 1, idx2,
                     jnp.where(lane_i == 2, rank1, jnp.where(lane_i == 3, rank2, 0.0))))
    mi_ref[...] = meta.T[:SUBLANES].astype(jnp.int32)
    mf_ref[...] = jnp.where(lane_i == 0, w1, jnp.where(lane_i == 1, w2, 0.0))


def _merge_call(x2, oa, ob, g_mix, w_g, b_g, w_pa, w_pb, w_o, g_moe, wr_hi, wr_lo, b_r):
    n = x2.shape[0]
    bm = MERGE_ROWS
    row = lambda w: pl.BlockSpec((bm, w), lambda i: (i, 0))
    full = lambda a: pl.BlockSpec(a.shape, lambda i: (0,) * a.ndim)
    consts = (g_mix, w_g, b_g, w_pa, w_pb, w_o, g_moe, wr_hi, wr_lo, b_r)
    return pl.pallas_call(
        functools.partial(_merge_kernel, bm=bm),
        grid=(n // bm,),
        in_specs=[row(D_MODEL), row(oa.shape[1]), row(ob.shape[1])] + [full(a) for a in consts],
        out_specs=(row(D_MODEL), row(D_MODEL),
                   pl.BlockSpec((None, SUBLANES, bm), lambda i: (i, 0, 0)), row(LANE),
                   pl.BlockSpec((8, LANE), lambda i: (0, 0))),
        out_shape=(jax.ShapeDtypeStruct((n, D_MODEL), F32),
                   jax.ShapeDtypeStruct((n, D_MODEL), F32),
                   jax.ShapeDtypeStruct((n // bm, SUBLANES, bm), jnp.int32),
                   jax.ShapeDtypeStruct((n, LANE), F32),
                   jax.ShapeDtypeStruct((8, LANE), F32)),
        scratch_shapes=[pltpu.VMEM((1, LANE), F32)],
        compiler_params=pltpu.CompilerParams(dimension_semantics=("arbitrary",),
                                             vmem_limit_bytes=VMEM_LIMIT),
        name="merge_router",
    )(x2, oa, ob, *consts)


def _dest_kernel(pstart_ref, meta_ref, dest_ref):
    for k in range(2):
        expert = meta_ref[k]
        dest = meta_ref[2 + k]
        for e in range(N_EXPERTS):
            dest = dest + jnp.where(expert == e, pstart_ref[e], 0)
        dest_ref[k] = dest


def _dest_call(pstart, meta):
    _, rows, lanes = meta.shape
    grid_spec = pltpu.PrefetchScalarGridSpec(
        num_scalar_prefetch=1,
        grid=(1,),
        in_specs=[pl.BlockSpec(meta.shape, lambda i, ps: (0, 0, 0))],
        out_specs=pl.BlockSpec((2, rows, lanes), lambda i, ps: (0, 0, 0)),
    )
    return pl.pallas_call(
        _dest_kernel,
        grid_spec=grid_spec,
        out_shape=jax.ShapeDtypeStruct((2, rows, lanes), jnp.int32),
        name="dest",
    )(pstart, meta)


def _row_copy(src_ref, src_row, dst_ref, dst_row, sem):
    return pltpu.make_async_copy(src_ref.at[pl.ds(src_row, 1), :],
                                 dst_ref.at[pl.ds(dst_row, 1), :], sem)


def _dispatch_kernel(d0_ref, d1_ref, h_ref, xs_in_ref, xs_ref, sem, *, bt):
    del xs_in_ref
    i = pl.program_id(0)

    def issue(g, carry):
        base = pl.multiple_of(g * SUBLANES, SUBLANES)
        rows = h_ref.at[pl.ds(base, SUBLANES), :]
        for u in range(SUBLANES):
            t = i * bt + base + u
            _row_copy(rows, u, xs_ref, d0_ref[t], sem).start()
            _row_copy(rows, u, xs_ref, d1_ref[t], sem).start()
        return carry

    lax.fori_loop(0, bt // SUBLANES, issue, 0)

    for _ in range(2):
        pltpu.make_async_copy(h_ref, xs_ref.at[pl.ds(0, bt), :], sem).wait()


def _dispatch_call(dest0, dest1, h2, p_rows):
    n, d = h2.shape
    bt = MOVE_ROWS
    grid_spec = pltpu.PrefetchScalarGridSpec(
        num_scalar_prefetch=2,
        grid=(n // bt,),
        in_specs=[pl.BlockSpec((bt, d), lambda i, d0, d1: (i, 0)),
                  pl.BlockSpec(memory_space=pl.ANY)],
        out_specs=pl.BlockSpec(memory_space=pl.ANY),
        scratch_shapes=[pltpu.SemaphoreType.DMA(())],
    )
    return pl.pallas_call(
        functools.partial(_dispatch_kernel, bt=bt),
        grid_spec=grid_spec,
        out_shape=jax.ShapeDtypeStruct((p_rows, d), h2.dtype),
        input_output_aliases={3: 0},
        compiler_params=pltpu.CompilerParams(dimension_semantics=("arbitrary",),
                                             vmem_limit_bytes=VMEM_LIMIT),
        name="dispatch",
    )(dest0, dest1, h2, jnp.zeros((p_rows, d), h2.dtype))


def _expert_kernel(blk_e_ref, used_ref, x_ref, w1_ref, w3_ref, w2_ref, y_ref, w1b, w3b, w2b):
    i = pl.program_id(0)
    live = i < used_ref[0]
    fresh = (i == 0) | (blk_e_ref[i] != blk_e_ref[jnp.maximum(i - 1, 0)])

    @pl.when(live & fresh)
    def _():
        w1b[...] = w1_ref[...].astype(BF16)
        w3b[...] = w3_ref[...].astype(BF16)
        w2b[...] = w2_ref[...].astype(BF16)

    @pl.when(live)
    def _():
        xb = x_ref[...].astype(BF16)
        a = jnp.dot(xb, w1b[...], preferred_element_type=F32)
        g = jnp.dot(xb, w3b[...], preferred_element_type=F32)
        mid = (a / (1.0 + jnp.exp(-a))) * g
        y_ref[...] = jnp.dot(mid.astype(BF16), w2b[...], preferred_element_type=F32)

    @pl.when(i >= used_ref[0])
    def _():
        y_ref[...] = jnp.zeros_like(y_ref)


def _expert_call(blk_e, n_used, xs, w1, w3, w2):
    p_rows, d = xs.shape
    blk = MOE_ROWS
    x_idx = lambda i, be, nu: (jnp.minimum(i, nu[0] - 1), 0)
    grid_spec = pltpu.PrefetchScalarGridSpec(
        num_scalar_prefetch=2,
        grid=(p_rows // blk,),
        in_specs=[pl.BlockSpec((blk, d), x_idx),
                  pl.BlockSpec((None, d, D_EXPERT), lambda i, be, nu: (be[i], 0, 0)),
                  pl.BlockSpec((None, d, D_EXPERT), lambda i, be, nu: (be[i], 0, 0)),
                  pl.BlockSpec((None, D_EXPERT, d), lambda i, be, nu: (be[i], 0, 0))],
        out_specs=pl.BlockSpec((blk, d), lambda i, be, nu: (i, 0)),
        scratch_shapes=[pltpu.VMEM((d, D_EXPERT), BF16), pltpu.VMEM((d, D_EXPERT), BF16),
                        pltpu.VMEM((D_EXPERT, d), BF16)],
    )
    return pl.pallas_call(
        _expert_kernel,
        grid_spec=grid_spec,
        out_shape=jax.ShapeDtypeStruct((p_rows, d), F32),
        compiler_params=pltpu.CompilerParams(dimension_semantics=("arbitrary",),
                                             vmem_limit_bytes=VMEM_LIMIT),
        name="experts",
    )(blk_e, n_used, xs, w1, w3, w2)


def _combine_kernel(d0_ref, d1_ref, x1_ref, mf_ref, g_ref, yb_ref, o_ref, buf0, buf1, sem, *, bt):
    i = pl.program_id(0)

    def issue(g, carry):
        base = pl.multiple_of(g * SUBLANES, SUBLANES)
        rows0 = buf0.at[pl.ds(base, SUBLANES), :]
        rows1 = buf1.at[pl.ds(base, SUBLANES), :]
        for u in range(SUBLANES):
            t = i * bt + base + u
            _row_copy(yb_ref, d0_ref[t], rows0, u, sem).start()
            _row_copy(yb_ref, d1_ref[t], rows1, u, sem).start()
        return carry

    lax.fori_loop(0, bt // SUBLANES, issue, 0)

    for buf in (buf0, buf1):
        pltpu.make_async_copy(yb_ref.at[pl.ds(0, bt), :], buf, sem).wait()

    mf = mf_ref[...]
    x2 = x1_ref[...] + mf[:, 0:1] * buf0[...] + mf[:, 1:2] * buf1[...]
    o_ref[...] = _rms(x2, g_ref[...])


def _combine_call(dest0, dest1, x1, mf, g_final, yb):
    n, d = x1.shape
    bt = MOVE_ROWS
    grid_spec = pltpu.PrefetchScalarGridSpec(
        num_scalar_prefetch=2,
        grid=(n // bt,),
        in_specs=[pl.BlockSpec((bt, d), lambda i, d0, d1: (i, 0)),
                  pl.BlockSpec((bt, LANE), lambda i, d0, d1: (i, 0)),
                  pl.BlockSpec((1, d), lambda i, d0, d1: (0, 0)),
                  pl.BlockSpec(memory_space=pl.ANY)],
        out_specs=pl.BlockSpec((bt, d), lambda i, d0, d1: (i, 0)),
        scratch_shapes=[pltpu.VMEM((bt, d), F32), pltpu.VMEM((bt, d), F32),
                        pltpu.SemaphoreType.DMA(())],
    )
    return pl.pallas_call(
        functools.partial(_combine_kernel, bt=bt),
        grid_spec=grid_spec,
        out_shape=jax.ShapeDtypeStruct((n, d), F32),
        compiler_params=pltpu.CompilerParams(dimension_semantics=("arbitrary",),
                                             vmem_limit_bytes=VMEM_LIMIT),
        name="combine",
    )(dest0, dest1, x1, mf, g_final, yb)


def _pad_heads(w, heads, width, padded, scale=1.0):
    d = w.shape[0]
    w = (w * scale).reshape(d, heads, width)
    return jnp.pad(w, ((0, 0), (0, 0), (0, padded - width))).reshape(d, heads * padded)


def _build_proj_params(w_in, b_fgate):
    fw, dw = FOX_HEADS * FOX_HEAD_DIM, DIFF_HEADS * DIFF_V_DIM
    o = 0
    fq, fk, fv = (w_in[:, o + k * fw:o + (k + 1) * fw] for k in range(3))
    o += 3 * fw
    ff = w_in[:, o:o + FOX_HEADS]
    o += FOX_HEADS
    dq, dk, dv = (w_in[:, o + k * dw:o + (k + 1) * dw] for k in range(3))
    d = w_in.shape[0]
    dq = dq.reshape(d, DIFF_HEADS, 2, DIFF_HEAD_DIM)
    dk = dk.reshape(d, DIFF_HEADS, 2, DIFF_HEAD_DIM)
    fscale = FOX_HEAD_DIM ** -0.5 * LOG2E
    dscale = DIFF_HEAD_DIM ** -0.5 * LOG2E
    half = lambda t, k: t[:, :, k, :].reshape(d, DIFF_HEADS * DIFF_HEAD_DIM)
    w_t = jnp.concatenate([
        _pad_heads(fq, FOX_HEADS, FOX_HEAD_DIM, LANE, fscale),
        _pad_heads(fv, FOX_HEADS, FOX_HEAD_DIM, FOX_V_ROWS),
        _pad_heads(half(dq, 0), DIFF_HEADS, DIFF_HEAD_DIM, LANE, dscale),
        _pad_heads(half(dq, 1), DIFF_HEADS, DIFF_HEAD_DIM, LANE, dscale),
        _pad_heads(dv, DIFF_HEADS, DIFF_V_DIM, DIFF_V_ROWS),
    ], axis=1).T.astype(BF16)
    w_row = jnp.concatenate([
        _pad_heads(fk, FOX_HEADS, FOX_HEAD_DIM, LANE),
        jnp.pad(ff, ((0, 0), (0, LANE - FOX_HEADS))),
        _pad_heads(half(dk, 0), DIFF_HEADS, DIFF_HEAD_DIM, LANE),
        _pad_heads(half(dk, 1), DIFF_HEADS, DIFF_HEAD_DIM, LANE),
    ], axis=1).astype(BF16)

    slopes = [2.0 ** (-8.0 / DIFF_HEADS * (hh + 1)) for hh in range(DIFF_HEADS)]
    b = jnp.zeros((_T_ROWS,), F32)
    for hh in range(FOX_HEADS):
        b = b.at[_TQF + hh * LANE + FOX_HEAD_DIM:_TQF + hh * LANE + FOX_HEAD_DIM + 3].set(-1.0)
        b = b.at[_TVF + hh * FOX_V_ROWS + FOX_HEAD_DIM].set(1.0)
    for hh in range(DIFF_HEADS):
        for base in (_TQ1, _TQ2):
            b = b.at[base + hh * LANE + DIFF_HEAD_DIM:base + hh * LANE + DIFF_HEAD_DIM + 3].set(slopes[hh])
        b = b.at[_TVD + hh * DIFF_V_ROWS + DIFF_V_DIM].set(1.0)
    bf_pad = jnp.pad(b_fgate, (0, LANE - FOX_HEADS))[None, :]
    return w_t, b[:, None], w_row, bf_pad, slopes


def kernel(x, g_mix, w_in, b_fgate, b_gate, lam_q1, lam_k1, lam_q2, lam_k2, g_subln, w_pa, w_pb, w_o,
           g_moe, w_group, b_group, w_expert, b_expert, w1, w3, w2, g_final):
    bsz, seq, d = x.shape
    n = bsz * seq
    tile = ATT_TILE
    nk = seq // tile
    x2 = x.reshape(n, d)

    w_t, b_t, w_row, bf_pad, slopes = _build_proj_params(w_in[0], b_fgate[0])
    qft, vft, q1t, q2t, vdt, kf, k1, k2, fcol, cblk = _proj_call(x2, g_mix, w_t, b_t, w_row, bf_pad, bsz, seq)
    shp = lambda a: a.reshape(bsz, seq, a.shape[1])
    c_flat = cblk[:, 0, :FOX_HEADS].reshape(bsz, nk, FOX_HEADS).transpose(0, 2, 1).reshape(-1)
    f_rows = (fcol[:, :FOX_HEADS].reshape(bsz, nk, tile, FOX_HEADS).transpose(0, 3, 1, 2)
              .reshape(bsz, FOX_HEADS, nk, 1, tile))

    oa = _fox_call(c_flat, qft, shp(kf), vft, f_rows, tile)
    lam = (jnp.exp(jnp.sum(lam_q1[0] * lam_k1[0])) - jnp.exp(jnp.sum(lam_q2[0] * lam_k2[0])) + LAM_INIT)
    scal = jnp.concatenate([jnp.asarray(slopes, F32), lam[None].astype(F32)])
    ob = _diff_call(scal, q1t, q2t, shp(k1), shp(k2), vdt, g_subln[0][:, None], tile)

    o = 3 * FOX_HEADS * FOX_HEAD_DIM + FOX_HEADS + 3 * DIFF_HEADS * DIFF_V_DIM
    w_g = w_in[0][:, o:].astype(BF16)
    w_pa_pad = jnp.pad(w_pa[0].reshape(FOX_HEADS, FOX_HEAD_DIM, d),
                       ((0, 0), (0, LANE - FOX_HEAD_DIM), (0, 0))).reshape(FOX_HEADS * LANE, d).astype(BF16)
    w_r = jnp.zeros((d, 2 * LANE), F32)
    w_r = w_r.at[:, :N_EXPERTS].set(w_expert[0]).at[:, LANE:LANE + N_GROUPS].set(w_group[0])
    wr_hi = w_r.astype(BF16)
    wr_lo = (w_r - wr_hi.astype(F32)).astype(BF16)
    b_r = jnp.zeros((1, 2 * LANE), F32)
    b_r = b_r.at[0, :N_EXPERTS].set(b_expert[0].reshape(-1)).at[0, LANE:LANE + N_GROUPS].set(b_group[0])
    x1, h2, mi, mf, cnt = _merge_call(
        x2, oa.reshape(n, -1), ob.reshape(n, -1), g_mix, w_g, b_gate, w_pa_pad,
        w_pb[0].astype(BF16), w_o[0].astype(BF16), g_moe, wr_hi, wr_lo, b_r)

    blk = MOE_ROWS
    counts = cnt[0, :N_EXPERTS].astype(jnp.int32)
    padded = (counts + blk - 1) // blk * blk
    pend = jnp.cumsum(padded)
    pstart = pend - padded
    meta = mi.transpose(1, 0, 2).reshape(SUBLANES, n // LANE, LANE)
    dest = _dest_call(pstart.astype(jnp.int32), meta).reshape(2, n)
    dest0, dest1 = dest[0], dest[1]
    p_rows = n * 2 + N_EXPERTS * blk
    n_blk = p_rows // blk
    blk_pos = jnp.arange(n_blk, dtype=jnp.int32) * blk
    blk_e = jnp.minimum(jnp.sum(pend[None, :] <= blk_pos[:, None], axis=1), N_EXPERTS - 1).astype(jnp.int32)
    n_used = (pend[-1:] // blk).astype(jnp.int32)

    xs = _dispatch_call(dest0, dest1, h2, p_rows)
    yb = _expert_call(blk_e, n_used, xs, w1[0], w3[0], w2[0])
    out = _combine_call(dest0, dest1, x1, mf, g_final[None, :], yb)
    return out.reshape(bsz, seq, d)
```

```python
import functools

import jax
import jax.numpy as jnp
from jax import lax
from jax.experimental import pallas as pl
from jax.experimental.pallas import tpu as pltpu

F32 = jnp.float32
BF16 = jnp.bfloat16

D_MODEL = 1024
FOX_HEADS = 8
FOX_HEAD_DIM = 64
DIFF_HEADS = 4
DIFF_HEAD_DIM = 64
DIFF_V_DIM = 128
CHUNK = 64
N_GROUPS = 4
EXPERTS_PER_GROUP = 8
N_EXPERTS = 32
D_EXPERT = 512
EPS = 1e-6
LAM_INIT = 0.8 - 0.6 * 1.0

LANE = 128
SUBLANES = 8
NEG = -1e30
LOG2E = 1.4426950408889634
ATT_TILE = 512
MERGE_ROWS = 512
MOE_ROWS = 512
MOVE_ROWS = 1024
VMEM_LIMIT = 56 * 1024 * 1024

FOX_V_ROWS = 80
DIFF_V_ROWS = 144

_TQF, _TVF, _TQ1 = 0, 1024, 1024 + FOX_HEADS * FOX_V_ROWS
_TQ2, _TVD = _TQ1 + 512, _TQ1 + 1024
_T_ROWS = _TVD + DIFF_HEADS * DIFF_V_ROWS
_KF, _FF, _K1, _K2, _ROW_COLS = 0, 1024, 1152, 1664, 2176


def _rms(x, g):
    return x * lax.rsqrt(jnp.mean(x * x, axis=-1, keepdims=True) + EPS) * g


def _split3(r):
    r0 = r.astype(BF16).astype(F32)
    r1 = (r - r0).astype(BF16).astype(F32)
    r2 = (r - r0 - r1).astype(BF16).astype(F32)
    return r0, r1, r2


_NT = (((1,), (1,)), ((), ()))


def _proj_kernel(x_ref, g_ref, wt_ref, bt_ref, w_ref, bf_ref,
                 qft_ref, vft_ref, q1t_ref, q2t_ref, vdt_ref, kf_ref, k1_ref, k2_ref,
                 fcol_ref, cblk_ref, carry_ref, *, bm, steps_per_seq):
    i = pl.program_id(0)
    h = _rms(x_ref[...], g_ref[...]).astype(BF16)

    zt = lax.dot_general(wt_ref[...], h, _NT, preferred_element_type=F32) + bt_ref[...]

    def heads(lo, n_heads, rows):
        return zt[lo:lo + n_heads * rows].reshape(n_heads, rows, bm).astype(BF16)

    qft_ref[...] = heads(_TQF, FOX_HEADS, LANE)
    vft_ref[...] = heads(_TVF, FOX_HEADS, FOX_V_ROWS)
    q1t_ref[...] = heads(_TQ1, DIFF_HEADS, LANE)
    q2t_ref[...] = heads(_TQ2, DIFF_HEADS, LANE)
    vdt_ref[...] = heads(_TVD, DIFF_HEADS, DIFF_V_ROWS)

    z = jnp.dot(h, w_ref[...], preferred_element_type=F32)

    zf = z[:, _FF:_K1] + bf_ref[...]
    logf = jnp.minimum(zf, 0.0) - jnp.log1p(jnp.exp(-jnp.abs(zf)))

    @pl.when(i % steps_per_seq == 0)
    def _():
        carry_ref[...] = jnp.zeros_like(carry_ref)

    c = carry_ref[...]
    row = lax.broadcasted_iota(jnp.int32, (bm, bm), 0)
    col = lax.broadcasted_iota(jnp.int32, (bm, bm), 1)
    tri = (col <= row).astype(BF16)
    l0, l1, l2 = _split3(logf)
    rel = (jnp.dot(tri, l0.astype(BF16), preferred_element_type=F32)
           + jnp.dot(tri, l1.astype(BF16), preferred_element_type=F32)
           + jnp.dot(tri, l2.astype(BF16), preferred_element_type=F32))
    fcum = rel + c
    carry_ref[...] = fcum[bm - 1:bm, :]
    fcol_ref[...] = fcum
    cblk_ref[...] = jnp.broadcast_to(c, cblk_ref.shape)

    lane = lax.broadcasted_iota(jnp.int32, (bm, LANE), 1)
    rel2 = rel * LOG2E
    for hh in range(FOX_HEADS):
        r0, r1, r2 = _split3(rel2[:, hh:hh + 1])
        aug = jnp.where(lane == FOX_HEAD_DIM, r0,
                        jnp.where(lane == FOX_HEAD_DIM + 1, r1,
                                  jnp.where(lane == FOX_HEAD_DIM + 2, r2, 0.0)))
        lo = _KF + hh * LANE
        kf_ref[:, hh * LANE:(hh + 1) * LANE] = (z[:, lo:lo + LANE] + aug).astype(BF16)

    width = DIFF_HEADS * LANE
    jrel = lax.broadcasted_iota(jnp.int32, (bm, width), 0)
    lane4 = lax.broadcasted_iota(jnp.int32, (bm, width), 1) & (LANE - 1)
    j0, j1, j2 = _split3(jrel.astype(F32) * LOG2E)
    augk = jnp.where(lane4 == DIFF_HEAD_DIM, j0,
                     jnp.where(lane4 == DIFF_HEAD_DIM + 1, j1,
                               jnp.where(lane4 == DIFF_HEAD_DIM + 2, j2, 0.0)))
    k1_ref[...] = (z[:, _K1:_K2] + augk).astype(BF16)
    k2_ref[...] = (z[:, _K2:_ROW_COLS] + augk).astype(BF16)


def _proj_call(x2, g_mix, w_t, b_t, w_row, bf_pad, bsz, seq):
    n = x2.shape[0]
    bm = ATT_TILE
    steps = n // bm
    spp = seq // bm
    row = lambda w: pl.BlockSpec((bm, w), lambda i: (i, 0))
    once = lambda a: pl.BlockSpec(a.shape, lambda i: (0,) * a.ndim, pipeline_mode=pl.Buffered(1))
    tshape = lambda heads, rows: jax.ShapeDtypeStruct((bsz, heads, spp, rows, bm), BF16)
    tspec = lambda heads, rows: pl.BlockSpec((None, heads, None, rows, bm),
                                             lambda i: (i // spp, 0, i % spp, 0, 0))
    out_shape = (
        tshape(FOX_HEADS, LANE), tshape(FOX_HEADS, FOX_V_ROWS),
        tshape(DIFF_HEADS, LANE), tshape(DIFF_HEADS, LANE), tshape(DIFF_HEADS, DIFF_V_ROWS),
        jax.ShapeDtypeStruct((n, FOX_HEADS * LANE), BF16),
        jax.ShapeDtypeStruct((n, DIFF_HEADS * LANE), BF16),
        jax.ShapeDtypeStruct((n, DIFF_HEADS * LANE), BF16),
        jax.ShapeDtypeStruct((n, LANE), F32),
        jax.ShapeDtypeStruct((steps, 8, LANE), F32),
    )
    out_specs = (
        tspec(FOX_HEADS, LANE), tspec(FOX_HEADS, FOX_V_ROWS),
        tspec(DIFF_HEADS, LANE), tspec(DIFF_HEADS, LANE), tspec(DIFF_HEADS, DIFF_V_ROWS),
        row(FOX_HEADS * LANE), row(DIFF_HEADS * LANE), row(DIFF_HEADS * LANE), row(LANE),
        pl.BlockSpec((None, 8, LANE), lambda i: (i, 0, 0)),
    )
    return pl.pallas_call(
        functools.partial(_proj_kernel, bm=bm, steps_per_seq=spp),
        grid=(steps,),
        in_specs=[row(D_MODEL), once(g_mix), once(w_t), once(b_t), once(w_row), once(bf_pad)],
        out_specs=out_specs,
        out_shape=out_shape,
        scratch_shapes=[pltpu.VMEM((1, LANE), F32)],
        compiler_params=pltpu.CompilerParams(dimension_semantics=("arbitrary",),
                                             vmem_limit_bytes=VMEM_LIMIT),
        name="proj",
    )(x2, g_mix, w_t, b_t, w_row, bf_pad)


def _softmax_step(s_ref, m_ref, acc_ref, vt, d, tile, fix=None):
    half = tile // 2
    for c0 in (0, half):
        cols = slice(c0, c0 + half)
        s = s_ref[:, cols]
        if fix is not None:
            s = fix(s, c0)
        dc = d[:, cols]
        m_old = m_ref[:, cols]
        m_new = jnp.maximum(m_old, jnp.max(s, axis=0, keepdims=True) + dc)
        p = jnp.exp2(s - (m_new - dc))
        acc_ref[:, cols] = (jnp.exp2(m_old - m_new) * acc_ref[:, cols]
                            + jnp.dot(vt, p.astype(BF16), preferred_element_type=F32))
        m_ref[:, cols] = m_new


def _tile_loop(i, fill, step, last):
    fill(0, 0)

    def run(first, trips, width):
        def body(q, carry):
            j = first + width * q
            for u in range(width):
                fill((u + 1) % 2, j + u + 1)
                step(u % 2, j + u)
            return carry

        lax.fori_loop(0, trips, body, 0)

    run(0, i // 8, 8)
    run((i // 8) * 8, (i % 8) // 4, 4)
    run((i // 4) * 4, (i % 4) // 2, 2)

    @pl.when(i % 2 == 1)
    def _():
        fill(1, i)
        step(0, i - 1)
        last(1)

    @pl.when(i % 2 == 0)
    def _():
        last(0)


def _fox_kernel(c_ref, qt_ref, k_ref, vt_ref, f_ref, o_ref, sa_ref, sb_ref, m_ref, acc_ref, *, tile, nk):
    b, hh, i = pl.program_id(0), pl.program_id(1), pl.program_id(2)
    qt = qt_ref[...]
    fi = f_ref[...] * LOG2E
    m_ref[...] = jnp.full_like(m_ref, NEG)
    acc_ref[...] = jnp.zeros_like(acc_ref)
    base = (b * FOX_HEADS + hh) * nk
    bufs = (sa_ref, sb_ref)

    def fill(slot, j):
        off = pl.multiple_of(j * tile, tile)
        bufs[slot][...] = jnp.dot(k_ref[pl.ds(off, tile), :], qt, preferred_element_type=F32)

    def step(slot, j):
        _softmax_step(bufs[slot], m_ref, acc_ref, vt_ref[j], fi - c_ref[base + j] * LOG2E, tile)

    def last(slot):
        key = lax.broadcasted_iota(jnp.int32, (tile, tile // 2), 0)
        qry = lax.broadcasted_iota(jnp.int32, (tile, tile // 2), 1)
        causal = lambda s, c0: jnp.where(key <= qry + c0, s, NEG)
        _softmax_step(bufs[slot], m_ref, acc_ref, vt_ref[i], fi - c_ref[base + i] * LOG2E, tile, causal)

    _tile_loop(i, fill, step, last)

    acc = acc_ref[...]
    o = acc[:FOX_HEAD_DIM] / acc[FOX_HEAD_DIM:FOX_HEAD_DIM + 1]
    o = jnp.concatenate([o, jnp.zeros((LANE - FOX_HEAD_DIM, tile), F32)], axis=0)
    o_ref[...] = o.T.astype(BF16)


def _fox_call(c_flat, qft, kf, vft, f_rows, tile):
    bsz, _, nq, _, _ = qft.shape
    seq = nq * tile
    grid_spec = pltpu.PrefetchScalarGridSpec(
        num_scalar_prefetch=1,
        grid=(bsz, FOX_HEADS, nq),
        in_specs=[
            pl.BlockSpec((None, None, None, LANE, tile), lambda b, h, i, c: (b, h, i, 0, 0)),
            pl.BlockSpec((None, seq, LANE), lambda b, h, i, c: (b, 0, h)),
            pl.BlockSpec((None, None, nq, FOX_V_ROWS, tile), lambda b, h, i, c: (b, h, 0, 0, 0)),
            pl.BlockSpec((None, None, None, 1, tile), lambda b, h, i, c: (b, h, i, 0, 0)),
        ],
        out_specs=pl.BlockSpec((None, tile, LANE), lambda b, h, i, c: (b, i, h)),
        scratch_shapes=[pltpu.VMEM((tile, tile), F32), pltpu.VMEM((tile, tile), F32),
                        pltpu.VMEM((1, tile), F32), pltpu.VMEM((FOX_V_ROWS, tile), F32)],
    )
    return pl.pallas_call(
        functools.partial(_fox_kernel, tile=tile, nk=nq),
        grid_spec=grid_spec,
        out_shape=jax.ShapeDtypeStruct((bsz, seq, FOX_HEADS * LANE), BF16),
        compiler_params=pltpu.CompilerParams(
            dimension_semantics=("parallel", "parallel", "arbitrary"),
            vmem_limit_bytes=VMEM_LIMIT),
        name="fox_attn",
    )(c_flat, qft, kf, vft, f_rows)


def _diff_kernel(sc_ref, q1t_ref, q2t_ref, k1_ref, k2_ref, vt_ref, g_ref, o_ref,
                 s1a_ref, s1b_ref, s2a_ref, s2b_ref, m1_ref, a1_ref, m2_ref, a2_ref, bias_ref, *, tile):
    hh, i = pl.program_id(1), pl.program_id(2)
    slope = sc_ref[hh] * LOG2E
    lam = sc_ref[DIFF_HEADS]

    @pl.when(i == 0)
    def _():
        key = lax.broadcasted_iota(jnp.int32, (tile, tile), 0)
        qry = lax.broadcasted_iota(jnp.int32, (tile, tile), 1)
        bias = -slope * (key + jnp.abs(qry - key)).astype(F32)
        bias_ref[...] = jnp.where((key // CHUNK) <= (qry // CHUNK), bias, NEG)

    q1t = q1t_ref[...]
    q2t = q2t_ref[...]
    for m_ref, a_ref in ((m1_ref, a1_ref), (m2_ref, a2_ref)):
        m_ref[...] = jnp.full_like(m_ref, NEG)
        a_ref[...] = jnp.zeros_like(a_ref)
    bufs1 = (s1a_ref, s1b_ref)
    bufs2 = (s2a_ref, s2b_ref)
    qrel = lax.broadcasted_iota(jnp.int32, (1, tile), 1)

    def fill(slot, j):
        off = pl.multiple_of(j * tile, tile)
        bufs1[slot][...] = jnp.dot(k1_ref[pl.ds(off, tile), :], q1t, preferred_element_type=F32)
        bufs2[slot][...] = jnp.dot(k2_ref[pl.ds(off, tile), :], q2t, preferred_element_type=F32)

    def step(slot, j):
        d = slope * ((j - i) * tile - qrel).astype(F32)
        vt = vt_ref[j]
        _softmax_step(bufs1[slot], m1_ref, a1_ref, vt, d, tile)
        _softmax_step(bufs2[slot], m2_ref, a2_ref, vt, d, tile)

    def last(slot):
        half = tile // 2
        chunk_bias = lambda s, c0: s + bias_ref[:, c0:c0 + half]
        zero = jnp.zeros((1, tile), F32)
        vt = vt_ref[i]
        _softmax_step(bufs1[slot], m1_ref, a1_ref, vt, zero, tile, chunk_bias)
        _softmax_step(bufs2[slot], m2_ref, a2_ref, vt, zero, tile, chunk_bias)

    _tile_loop(i, fill, step, last)

    a1 = a1_ref[...]
    a2 = a2_ref[...]
    o1 = a1[:DIFF_V_DIM] / a1[DIFF_V_DIM:DIFF_V_DIM + 1]
    o2 = a2[:DIFF_V_DIM] / a2[DIFF_V_DIM:DIFF_V_DIM + 1]
    ob = o1 - lam * o2
    ob = ob * lax.rsqrt(jnp.mean(ob * ob, axis=0, keepdims=True) + EPS) * g_ref[...] * (1.0 - LAM_INIT)
    o_ref[...] = ob.T.astype(BF16)


def _diff_call(scal, q1t, q2t, k1, k2, vdt, g_col, tile):
    bsz, _, nq, _, _ = q1t.shape
    seq = nq * tile
    qspec = pl.BlockSpec((None, None, None, LANE, tile), lambda b, h, i, c: (b, h, i, 0, 0))
    kspec = pl.BlockSpec((None, seq, LANE), lambda b, h, i, c: (b, 0, h))
    score = pltpu.VMEM((tile, tile), F32)
    grid_spec = pltpu.PrefetchScalarGridSpec(
        num_scalar_prefetch=1,
        grid=(bsz, DIFF_HEADS, nq),
        in_specs=[qspec, qspec, kspec, kspec,
                  pl.BlockSpec((None, None, nq, DIFF_V_ROWS, tile), lambda b, h, i, c: (b, h, 0, 0, 0)),
                  pl.BlockSpec((DIFF_V_DIM, 1), lambda b, h, i, c: (0, 0))],
        out_specs=pl.BlockSpec((None, tile, LANE), lambda b, h, i, c: (b, i, h)),
        scratch_shapes=[score, score, score, score,
                        pltpu.VMEM((1, tile), F32), pltpu.VMEM((DIFF_V_ROWS, tile), F32),
                        pltpu.VMEM((1, tile), F32), pltpu.VMEM((DIFF_V_ROWS, tile), F32),
                        score],
    )
    return pl.pallas_call(
        functools.partial(_diff_kernel, tile=tile),
        grid_spec=grid_spec,
        out_shape=jax.ShapeDtypeStruct((bsz, seq, DIFF_HEADS * LANE), BF16),
        compiler_params=pltpu.CompilerParams(
            dimension_semantics=("arbitrary", "arbitrary", "arbitrary"),
            vmem_limit_bytes=VMEM_LIMIT),
        name="diff_attn",
    )(scal, q1t, q2t, k1, k2, vdt, g_col)


def _merge_kernel(x_ref, oa_ref, ob_ref, gmix_ref, wg_ref, bg_ref, wpa_ref, wpb_ref, wo_ref,
                  gmoe_ref, wrh_ref, wrl_ref, br_ref,
                  x1_ref, h2_ref, mi_ref, mf_ref, cnt_ref, run_ref, *, bm):
    i = pl.program_id(0)

    @pl.when(i == 0)
    def _():
        run_ref[...] = jnp.zeros_like(run_ref)

    x = x_ref[...]
    h = _rms(x, gmix_ref[...]).astype(BF16)
    gates = jnp.dot(h, wg_ref[...], preferred_element_type=F32) + bg_ref[...]
    gates = 1.0 / (1.0 + jnp.exp(-gates))
    ya = jnp.dot(oa_ref[...], wpa_ref[...], preferred_element_type=F32)
    yb = jnp.dot(ob_ref[...], wpb_ref[...], preferred_element_type=F32)
    y = gates[:, :D_MODEL] * ya + gates[:, D_MODEL:] * yb
    x1 = x + jnp.dot(y.astype(BF16), wo_ref[...], preferred_element_type=F32)
    x1_ref[...] = x1
    h2 = _rms(x1, gmoe_ref[...])
    h2_ref[...] = h2

    h2h = h2.astype(BF16)
    h2l = (h2 - h2h.astype(F32)).astype(BF16)
    r = (jnp.dot(h2h, wrh_ref[...], preferred_element_type=F32)
         + jnp.dot(h2l, wrh_ref[...], preferred_element_type=F32)
         + jnp.dot(h2h, wrl_ref[...], preferred_element_type=F32)) + br_ref[...]
    el_all = r[:, :LANE]
    gl = r[:, LANE:]

    lane_i = lax.broadcasted_iota(jnp.int32, (bm, LANE), 1)
    lane = lane_i.astype(F32)
    big = float(LANE)

    def first_argmax(vals):
        vmax = jnp.max(vals, axis=1, keepdims=True)
        idx = jnp.min(jnp.where(vals == vmax, lane, big), axis=1, keepdims=True)
        return vmax, idx

    glm = jnp.where(lane_i < N_GROUPS, gl, NEG)
    gmax, g_idx = first_argmax(glm)
    g_w = 1.0 / jnp.sum(jnp.exp(glm - gmax), axis=1, keepdims=True)

    in_group = ((lane_i >> 3).astype(F32) == g_idx) & (lane_i < N_EXPERTS)
    elm = jnp.where(in_group, el_all, NEG)
    e1, idx1 = first_argmax(elm)
    elm2 = jnp.where(lane == idx1, NEG, elm)
    e2, idx2 = first_argmax(elm2)
    t = jnp.exp(e2 - e1)
    w1 = g_w / (1.0 + t)
    w2 = g_w * t / (1.0 + t)

    oh1 = (lane == idx1).astype(F32)
    oh2 = (lane == idx2).astype(F32)
    oh = oh1 + oh2
    row = lax.broadcasted_iota(jnp.int32, (bm, bm), 0)
    col = lax.broadcasted_iota(jnp.int32, (bm, bm), 1)
    strict = (col < row).astype(BF16)
    before = jnp.dot(strict, oh.astype(BF16), preferred_element_type=F32) + run_ref[...]
    rank1 = jnp.sum(oh1 * before, axis=1, keepdims=True)
    rank2 = jnp.sum(oh2 * before, axis=1, keepdims=True)
    run_ref[...] = run_ref[...] + jnp.sum(oh, axis=0, keepdims=True)
    cnt_ref[...] = jnp.broadcast_to(run_ref[...], cnt_ref.shape)

    meta = jnp.where(lane_i == 0, idx1, jnp.where(lane_i == 1, idx2,
                     jnp.where(lane_i == 2, rank1, jnp.where(lane_i == 3, rank2, 0.0))))
    mi_ref[...] = meta.T[:SUBLANES].astype(jnp.int32)
    mf_ref[...] = jnp.where(lane_i == 0, w1, jnp.where(lane_i == 1, w2, 0.0))


def _merge_call(x2, oa, ob, g_mix, w_g, b_g, w_pa, w_pb, w_o, g_moe, wr_hi, wr_lo, b_r):
    n = x2.shape[0]
    bm = MERGE_ROWS
    row = lambda w: pl.BlockSpec((bm, w), lambda i: (i, 0))
    full = lambda a: pl.BlockSpec(a.shape, lambda i: (0,) * a.ndim)
    consts = (g_mix, w_g, b_g, w_pa, w_pb, w_o, g_moe, wr_hi, wr_lo, b_r)
    return pl.pallas_call(
        functools.partial(_merge_kernel, bm=bm),
        grid=(n // bm,),
        in_specs=[row(D_MODEL), row(oa.shape[1]), row(ob.shape[1])] + [full(a) for a in consts],
        out_specs=(row(D_MODEL), row(D_MODEL),
                   pl.BlockSpec((None, SUBLANES, bm), lambda i: (i, 0, 0)), row(LANE),
                   pl.BlockSpec((8, LANE), lambda i: (0, 0))),
        out_shape=(jax.ShapeDtypeStruct((n, D_MODEL), F32),
                   jax.ShapeDtypeStruct((n, D_MODEL), F32),
                   jax.ShapeDtypeStruct((n // bm, SUBLANES, bm), jnp.int32),
                   jax.ShapeDtypeStruct((n, LANE), F32),
                   jax.ShapeDtypeStruct((8, LANE), F32)),
        scratch_shapes=[pltpu.VMEM((1, LANE), F32)],
        compiler_params=pltpu.CompilerParams(dimension_semantics=("arbitrary",),
                                             vmem_limit_bytes=VMEM_LIMIT),
        name="merge_router",
    )(x2, oa, ob, *consts)


def _dest_kernel(pstart_ref, meta_ref, dest_ref):
    for k in range(2):
        expert = meta_ref[k]
        dest = meta_ref[2 + k]
        for e in range(N_EXPERTS):
            dest = dest + jnp.where(expert == e, pstart_ref[e], 0)
        dest_ref[k] = dest


def _dest_call(pstart, meta):
    _, rows, lanes = meta.shape
    grid_spec = pltpu.PrefetchScalarGridSpec(
        num_scalar_prefetch=1,
        grid=(1,),
        in_specs=[pl.BlockSpec(meta.shape, lambda i, ps: (0, 0, 0))],
        out_specs=pl.BlockSpec((2, rows, lanes), lambda i, ps: (0, 0, 0)),
    )
    return pl.pallas_call(
        _dest_kernel,
        grid_spec=grid_spec,
        out_shape=jax.ShapeDtypeStruct((2, rows, lanes), jnp.int32),
        name="dest",
    )(pstart, meta)


def _row_copy(src_ref, src_row, dst_ref, dst_row, sem):
    return pltpu.make_async_copy(src_ref.at[pl.ds(src_row, 1), :],
                                 dst_ref.at[pl.ds(dst_row, 1), :], sem)


def _dispatch_kernel(d0_ref, d1_ref, tail_ref, h_ref, xs_ref, zero_ref, sem, zsem, *, bt, blk):
    i = pl.program_id(0)

    @pl.when(i == 0)
    def _():
        zero_ref[...] = jnp.zeros_like(zero_ref)

        def fill(e):
            start = pl.multiple_of(tail_ref[e] * blk, blk)
            return pltpu.make_async_copy(zero_ref, xs_ref.at[pl.ds(start, blk), :], zsem)

        for e in range(2 * N_EXPERTS):
            @pl.when(tail_ref[e] >= 0)
            def _():
                fill(e).start()

        for e in range(2 * N_EXPERTS):
            @pl.when(tail_ref[e] >= 0)
            def _():
                fill(e).wait()

    def issue(g, carry):
        base = pl.multiple_of(g * SUBLANES, SUBLANES)
        rows = h_ref.at[pl.ds(base, SUBLANES), :]
        for u in range(SUBLANES):
            t = i * bt + base + u
            _row_copy(rows, u, xs_ref, d0_ref[t], sem).start()
            _row_copy(rows, u, xs_ref, d1_ref[t], sem).start()
        return carry

    lax.fori_loop(0, bt // SUBLANES, issue, 0)

    for _ in range(2):
        pltpu.make_async_copy(h_ref, xs_ref.at[pl.ds(0, bt), :], sem).wait()


def _dispatch_call(dest0, dest1, tail_blk, h2, p_rows):
    n, d = h2.shape
    bt = MOVE_ROWS
    blk = MOE_ROWS
    grid_spec = pltpu.PrefetchScalarGridSpec(
        num_scalar_prefetch=3,
        grid=(n // bt,),
        in_specs=[pl.BlockSpec((bt, d), lambda i, d0, d1, tb: (i, 0))],
        out_specs=pl.BlockSpec(memory_space=pl.ANY),
        scratch_shapes=[pltpu.VMEM((blk, d), h2.dtype), pltpu.SemaphoreType.DMA(()),
                        pltpu.SemaphoreType.DMA(())],
    )
    return pl.pallas_call(
        functools.partial(_dispatch_kernel, bt=bt, blk=blk),
        grid_spec=grid_spec,
        out_shape=jax.ShapeDtypeStruct((p_rows, d), h2.dtype),
        compiler_params=pltpu.CompilerParams(dimension_semantics=("arbitrary",),
                                             vmem_limit_bytes=VMEM_LIMIT),
        name="dispatch",
    )(dest0, dest1, tail_blk, h2)


def _expert_kernel(blk_e_ref, used_ref, x_ref, w1_ref, w3_ref, w2_ref, y_ref, w1b, w3b, w2b):
    i = pl.program_id(0)
    live = i < used_ref[0]
    fresh = (i == 0) | (blk_e_ref[i] != blk_e_ref[jnp.maximum(i - 1, 0)])

    @pl.when(live & fresh)
    def _():
        w1b[...] = w1_ref[...].astype(BF16)
        w3b[...] = w3_ref[...].astype(BF16)
        w2b[...] = w2_ref[...].astype(BF16)

    @pl.when(live)
    def _():
        xb = x_ref[...].astype(BF16)
        a = jnp.dot(xb, w1b[...], preferred_element_type=F32)
        g = jnp.dot(xb, w3b[...], preferred_element_type=F32)
        mid = (a / (1.0 + jnp.exp(-a))) * g
        y_ref[...] = jnp.dot(mid.astype(BF16), w2b[...], preferred_element_type=F32)

    @pl.when(i >= used_ref[0])
    def _():
        y_ref[...] = jnp.zeros_like(y_ref)


def _expert_call(blk_e, n_used, xs, w1, w3, w2):
    p_rows, d = xs.shape
    blk = MOE_ROWS
    x_idx = lambda i, be, nu: (jnp.minimum(i, nu[0] - 1), 0)
    grid_spec = pltpu.PrefetchScalarGridSpec(
        num_scalar_prefetch=2,
        grid=(p_rows // blk,),
        in_specs=[pl.BlockSpec((blk, d), x_idx),
                  pl.BlockSpec((None, d, D_EXPERT), lambda i, be, nu: (be[i], 0, 0)),
                  pl.BlockSpec((None, d, D_EXPERT), lambda i, be, nu: (be[i], 0, 0)),
                  pl.BlockSpec((None, D_EXPERT, d), lambda i, be, nu: (be[i], 0, 0))],
        out_specs=pl.BlockSpec((blk, d), lambda i, be, nu: (i, 0)),
        scratch_shapes=[pltpu.VMEM((d, D_EXPERT), BF16), pltpu.VMEM((d, D_EXPERT), BF16),
                        pltpu.VMEM((D_EXPERT, d), BF16)],
    )
    return pl.pallas_call(
        _expert_kernel,
        grid_spec=grid_spec,
        out_shape=jax.ShapeDtypeStruct((p_rows, d), F32),
        compiler_params=pltpu.CompilerParams(dimension_semantics=("arbitrary",),
                                             vmem_limit_bytes=VMEM_LIMIT),
        name="experts",
    )(blk_e, n_used, xs, w1, w3, w2)


def _combine_kernel(d0_ref, d1_ref, x1_ref, mf_ref, g_ref, yb_ref, o_ref, buf0, buf1, sem, *, bt):
    i = pl.program_id(0)

    def issue(g, carry):
        base = pl.multiple_of(g * SUBLANES, SUBLANES)
        rows0 = buf0.at[pl.ds(base, SUBLANES), :]
        rows1 = buf1.at[pl.ds(base, SUBLANES), :]
        for u in range(SUBLANES):
            t = i * bt + base + u
            _row_copy(yb_ref, d0_ref[t], rows0, u, sem).start()
            _row_copy(yb_ref, d1_ref[t], rows1, u, sem).start()
        return carry

    lax.fori_loop(0, bt // SUBLANES, issue, 0)

    for buf in (buf0, buf1):
        pltpu.make_async_copy(yb_ref.at[pl.ds(0, bt), :], buf, sem).wait()

    mf = mf_ref[...]
    x2 = x1_ref[...] + mf[:, 0:1] * buf0[...] + mf[:, 1:2] * buf1[...]
    o_ref[...] = _rms(x2, g_ref[...])


def _combine_call(dest0, dest1, x1, mf, g_final, yb):
    n, d = x1.shape
    bt = MOVE_ROWS
    grid_spec = pltpu.PrefetchScalarGridSpec(
        num_scalar_prefetch=2,
        grid=(n // bt,),
        in_specs=[pl.BlockSpec((bt, d), lambda i, d0, d1: (i, 0)),
                  pl.BlockSpec((bt, LANE), lambda i, d0, d1: (i, 0)),
                  pl.BlockSpec((1, d), lambda i, d0, d1: (0, 0)),
                  pl.BlockSpec(memory_space=pl.ANY)],
        out_specs=pl.BlockSpec((bt, d), lambda i, d0, d1: (i, 0)),
        scratch_shapes=[pltpu.VMEM((bt, d), F32), pltpu.VMEM((bt, d), F32),
                        pltpu.SemaphoreType.DMA(())],
    )
    return pl.pallas_call(
        functools.partial(_combine_kernel, bt=bt),
        grid_spec=grid_spec,
        out_shape=jax.ShapeDtypeStruct((n, d), F32),
        compiler_params=pltpu.CompilerParams(dimension_semantics=("arbitrary",),
                                             vmem_limit_bytes=VMEM_LIMIT),
        name="combine",
    )(dest0, dest1, x1, mf, g_final, yb)


def _pad_heads(w, heads, width, padded, scale=1.0):
    d = w.shape[0]
    w = (w * scale).reshape(d, heads, width)
    return jnp.pad(w, ((0, 0), (0, 0), (0, padded - width))).reshape(d, heads * padded)


def _build_proj_params(w_in, b_fgate):
    fw, dw = FOX_HEADS * FOX_HEAD_DIM, DIFF_HEADS * DIFF_V_DIM
    o = 0
    fq, fk, fv = (w_in[:, o + k * fw:o + (k + 1) * fw] for k in range(3))
    o += 3 * fw
    ff = w_in[:, o:o + FOX_HEADS]
    o += FOX_HEADS
    dq, dk, dv = (w_in[:, o + k * dw:o + (k + 1) * dw] for k in range(3))
    d = w_in.shape[0]
    dq = dq.reshape(d, DIFF_HEADS, 2, DIFF_HEAD_DIM)
    dk = dk.reshape(d, DIFF_HEADS, 2, DIFF_HEAD_DIM)
    fscale = FOX_HEAD_DIM ** -0.5 * LOG2E
    dscale = DIFF_HEAD_DIM ** -0.5 * LOG2E
    half = lambda t, k: t[:, :, k, :].reshape(d, DIFF_HEADS * DIFF_HEAD_DIM)
    w_t = jnp.concatenate([
        _pad_heads(fq, FOX_HEADS, FOX_HEAD_DIM, LANE, fscale),
        _pad_heads(fv, FOX_HEADS, FOX_HEAD_DIM, FOX_V_ROWS),
        _pad_heads(half(dq, 0), DIFF_HEADS, DIFF_HEAD_DIM, LANE, dscale),
        _pad_heads(half(dq, 1), DIFF_HEADS, DIFF_HEAD_DIM, LANE, dscale),
        _pad_heads(dv, DIFF_HEADS, DIFF_V_DIM, DIFF_V_ROWS),
    ], axis=1).T.astype(BF16)
    w_row = jnp.concatenate([
        _pad_heads(fk, FOX_HEADS, FOX_HEAD_DIM, LANE),
        jnp.pad(ff, ((0, 0), (0, LANE - FOX_HEADS))),
        _pad_heads(half(dk, 0), DIFF_HEADS, DIFF_HEAD_DIM, LANE),
        _pad_heads(half(dk, 1), DIFF_HEADS, DIFF_HEAD_DIM, LANE),
    ], axis=1).astype(BF16)

    slopes = [2.0 ** (-8.0 / DIFF_HEADS * (hh + 1)) for hh in range(DIFF_HEADS)]
    b = jnp.zeros((_T_ROWS,), F32)
    for hh in range(FOX_HEADS):
        b = b.at[_TQF + hh * LANE + FOX_HEAD_DIM:_TQF + hh * LANE + FOX_HEAD_DIM + 3].set(-1.0)
        b = b.at[_TVF + hh * FOX_V_ROWS + FOX_HEAD_DIM].set(1.0)
    for hh in range(DIFF_HEADS):
        for base in (_TQ1, _TQ2):
            b = b.at[base + hh * LANE + DIFF_HEAD_DIM:base + hh * LANE + DIFF_HEAD_DIM + 3].set(slopes[hh])
        b = b.at[_TVD + hh * DIFF_V_ROWS + DIFF_V_DIM].set(1.0)
    bf_pad = jnp.pad(b_fgate, (0, LANE - FOX_HEADS))[None, :]
    return w_t, b[:, None], w_row, bf_pad, slopes


def kernel(x, g_mix, w_in, b_fgate, b_gate, lam_q1, lam_k1, lam_q2, lam_k2, g_subln, w_pa, w_pb, w_o,
           g_moe, w_group, b_group, w_expert, b_expert, w1, w3, w2, g_final):
    bsz, seq, d = x.shape
    n = bsz * seq
    tile = ATT_TILE
    nk = seq // tile
    x2 = x.reshape(n, d)

    w_t, b_t, w_row, bf_pad, slopes = _build_proj_params(w_in[0], b_fgate[0])
    qft, vft, q1t, q2t, vdt, kf, k1, k2, fcol, cblk = _proj_call(x2, g_mix, w_t, b_t, w_row, bf_pad, bsz, seq)
    shp = lambda a: a.reshape(bsz, seq, a.shape[1])
    c_flat = cblk[:, 0, :FOX_HEADS].reshape(bsz, nk, FOX_HEADS).transpose(0, 2, 1).reshape(-1)
    f_rows = (fcol[:, :FOX_HEADS].reshape(bsz, nk, tile, FOX_HEADS).transpose(0, 3, 1, 2)
              .reshape(bsz, FOX_HEADS, nk, 1, tile))

    oa = _fox_call(c_flat, qft, shp(kf), vft, f_rows, tile)
    lam = (jnp.exp(jnp.sum(lam_q1[0] * lam_k1[0])) - jnp.exp(jnp.sum(lam_q2[0] * lam_k2[0])) + LAM_INIT)
    scal = jnp.concatenate([jnp.asarray(slopes, F32), lam[None].astype(F32)])
    ob = _diff_call(scal, q1t, q2t, shp(k1), shp(k2), vdt, g_subln[0][:, None], tile)

    o = 3 * FOX_HEADS * FOX_HEAD_DIM + FOX_HEADS + 3 * DIFF_HEADS * DIFF_V_DIM
    w_g = w_in[0][:, o:].astype(BF16)
    w_pa_pad = jnp.pad(w_pa[0].reshape(FOX_HEADS, FOX_HEAD_DIM, d),
                       ((0, 0), (0, LANE - FOX_HEAD_DIM), (0, 0))).reshape(FOX_HEADS * LANE, d).astype(BF16)
    w_r = jnp.zeros((d, 2 * LANE), F32)
    w_r = w_r.at[:, :N_EXPERTS].set(w_expert[0]).at[:, LANE:LANE + N_GROUPS].set(w_group[0])
    wr_hi = w_r.astype(BF16)
    wr_lo = (w_r - wr_hi.astype(F32)).astype(BF16)
    b_r = jnp.zeros((1, 2 * LANE), F32)
    b_r = b_r.at[0, :N_EXPERTS].set(b_expert[0].reshape(-1)).at[0, LANE:LANE + N_GROUPS].set(b_group[0])
    x1, h2, mi, mf, cnt = _merge_call(
        x2, oa.reshape(n, -1), ob.reshape(n, -1), g_mix, w_g, b_gate, w_pa_pad,
        w_pb[0].astype(BF16), w_o[0].astype(BF16), g_moe, wr_hi, wr_lo, b_r)

    blk = MOE_ROWS
    counts = cnt[0, :N_EXPERTS].astype(jnp.int32)
    padded = (counts + blk - 1) // blk * blk
    pend = jnp.cumsum(padded)
    pstart = pend - padded
    meta = mi.transpose(1, 0, 2).reshape(SUBLANES, n // LANE, LANE)
    dest = _dest_call(pstart.astype(jnp.int32), meta).reshape(2, n)
    dest0, dest1 = dest[0], dest[1]
    p_rows = n * 2 + N_EXPERTS * blk
    n_blk = p_rows // blk
    blk_pos = jnp.arange(n_blk, dtype=jnp.int32) * blk
    blk_e = jnp.minimum(jnp.sum(pend[None, :] <= blk_pos[:, None], axis=1), N_EXPERTS - 1).astype(jnp.int32)
    n_used = (pend[-1:] // blk).astype(jnp.int32)

    spare = n_used[0] + jnp.arange(N_EXPERTS, dtype=jnp.int32)
    tail_blk = jnp.concatenate([jnp.where(padded > 0, pend // blk - 1, -1),
                                jnp.where(spare < n_blk, spare, -1)]).astype(jnp.int32)
    xs = _dispatch_call(dest0, dest1, tail_blk, h2, p_rows)
    yb = _expert_call(blk_e, n_used, xs, w1[0], w3[0], w2[0])
    out = _combine_call(dest0, dest1, x1, mf, g_final[None, :], yb)
    return out.reshape(bsz, seq, d)
```

```python
import functools

import jax
import jax.numpy as jnp
from jax import lax
from jax.experimental import pallas as pl
from jax.experimental.pallas import tpu as pltpu

F32 = jnp.float32
BF16 = jnp.bfloat16

D_MODEL = 1024
FOX_HEADS = 8
FOX_HEAD_DIM = 64
DIFF_HEADS = 4
DIFF_HEAD_DIM = 64
DIFF_V_DIM = 128
CHUNK = 64
N_GROUPS = 4
EXPERTS_PER_GROUP = 8
N_EXPERTS = 32
D_EXPERT = 512
EPS = 1e-6
LAM_INIT = 0.8 - 0.6 * 1.0

LANE = 128
SUBLANES = 8
NEG = -1e30
LOG2E = 1.4426950408889634
ATT_TILE = 512
MERGE_ROWS = 512
MOE_ROWS = 512
MOVE_ROWS = 1024
VMEM_LIMIT = 56 * 1024 * 1024

FOX_V_ROWS = 80
DIFF_V_ROWS = 144

_TQF, _TVF, _TQ1 = 0, 1024, 1024 + FOX_HEADS * FOX_V_ROWS
_TQ2, _TVD = _TQ1 + 512, _TQ1 + 1024
_T_ROWS = _TVD + DIFF_HEADS * DIFF_V_ROWS
_KF, _FF, _K1, _K2, _ROW_COLS = 0, 1024, 1152, 1664, 2176


def _rms(x, g):
    return x * lax.rsqrt(jnp.mean(x * x, axis=-1, keepdims=True) + EPS) * g


def _split3(r):
    r0 = r.astype(BF16).astype(F32)
    r1 = (r - r0).astype(BF16).astype(F32)
    r2 = (r - r0 - r1).astype(BF16).astype(F32)
    return r0, r1, r2


_NT = (((1,), (1,)), ((), ()))


def _proj_kernel(x_ref, g_ref, wt_ref, bt_ref, w_ref, bf_ref,
                 qft_ref, vft_ref, q1t_ref, q2t_ref, vdt_ref, kf_ref, k1_ref, k2_ref,
                 fcol_ref, cblk_ref, carry_ref, *, bm, steps_per_seq):
    i = pl.program_id(0)
    h = _rms(x_ref[...], g_ref[...]).astype(BF16)

    zt = lax.dot_general(wt_ref[...], h, _NT, preferred_element_type=F32) + bt_ref[...]

    def heads(lo, n_heads, rows):
        return zt[lo:lo + n_heads * rows].reshape(n_heads, rows, bm).astype(BF16)

    qft_ref[...] = heads(_TQF, FOX_HEADS, LANE)
    vft_ref[...] = heads(_TVF, FOX_HEADS, FOX_V_ROWS)
    q1t_ref[...] = heads(_TQ1, DIFF_HEADS, LANE)
    q2t_ref[...] = heads(_TQ2, DIFF_HEADS, LANE)
    vdt_ref[...] = heads(_TVD, DIFF_HEADS, DIFF_V_ROWS)

    z = jnp.dot(h, w_ref[...], preferred_element_type=F32)

    zf = z[:, _FF:_K1] + bf_ref[...]
    logf = jnp.minimum(zf, 0.0) - jnp.log1p(jnp.exp(-jnp.abs(zf)))

    @pl.when(i % steps_per_seq == 0)
    def _():
        carry_ref[...] = jnp.zeros_like(carry_ref)

    c = carry_ref[...]
    row = lax.broadcasted_iota(jnp.int32, (bm, bm), 0)
    col = lax.broadcasted_iota(jnp.int32, (bm, bm), 1)
    tri = (col <= row).astype(BF16)
    l0, l1, l2 = _split3(logf)
    rel = (jnp.dot(tri, l0.astype(BF16), preferred_element_type=F32)
           + jnp.dot(tri, l1.astype(BF16), preferred_element_type=F32)
           + jnp.dot(tri, l2.astype(BF16), preferred_element_type=F32))
    fcum = rel + c
    carry_ref[...] = fcum[bm - 1:bm, :]
    fcol_ref[...] = fcum
    cblk_ref[...] = jnp.broadcast_to(c, cblk_ref.shape)

    lane = lax.broadcasted_iota(jnp.int32, (bm, LANE), 1)
    rel2 = rel * LOG2E
    for hh in range(FOX_HEADS):
        r0, r1, r2 = _split3(rel2[:, hh:hh + 1])
        aug = jnp.where(lane == FOX_HEAD_DIM, r0,
                        jnp.where(lane == FOX_HEAD_DIM + 1, r1,
                                  jnp.where(lane == FOX_HEAD_DIM + 2, r2, 0.0)))
        lo = _KF + hh * LANE
        kf_ref[:, hh * LANE:(hh + 1) * LANE] = (z[:, lo:lo + LANE] + aug).astype(BF16)

    width = DIFF_HEADS * LANE
    jrel = lax.broadcasted_iota(jnp.int32, (bm, width), 0)
    lane4 = lax.broadcasted_iota(jnp.int32, (bm, width), 1) & (LANE - 1)
    j0, j1, j2 = _split3(jrel.astype(F32) * LOG2E)
    augk = jnp.where(lane4 == DIFF_HEAD_DIM, j0,
                     jnp.where(lane4 == DIFF_HEAD_DIM + 1, j1,
                               jnp.where(lane4 == DIFF_HEAD_DIM + 2, j2, 0.0)))
    k1_ref[...] = (z[:, _K1:_K2] + augk).astype(BF16)
    k2_ref[...] = (z[:, _K2:_ROW_COLS] + augk).astype(BF16)


def _proj_call(x2, g_mix, w_t, b_t, w_row, bf_pad, bsz, seq):
    n = x2.shape[0]
    bm = ATT_TILE
    steps = n // bm
    spp = seq // bm
    row = lambda w: pl.BlockSpec((bm, w), lambda i: (i, 0))
    once = lambda a: pl.BlockSpec(a.shape, lambda i: (0,) * a.ndim, pipeline_mode=pl.Buffered(1))
    tshape = lambda heads, rows: jax.ShapeDtypeStruct((bsz, heads, spp, rows, bm), BF16)
    tspec = lambda heads, rows: pl.BlockSpec((None, heads, None, rows, bm),
                                             lambda i: (i // spp, 0, i % spp, 0, 0))
    out_shape = (
        tshape(FOX_HEADS, LANE), tshape(FOX_HEADS, FOX_V_ROWS),
        tshape(DIFF_HEADS, LANE), tshape(DIFF_HEADS, LANE), tshape(DIFF_HEADS, DIFF_V_ROWS),
        jax.ShapeDtypeStruct((n, FOX_HEADS * LANE), BF16),
        jax.ShapeDtypeStruct((n, DIFF_HEADS * LANE), BF16),
        jax.ShapeDtypeStruct((n, DIFF_HEADS * LANE), BF16),
        jax.ShapeDtypeStruct((n, LANE), F32),
        jax.ShapeDtypeStruct((steps, 8, LANE), F32),
    )
    out_specs = (
        tspec(FOX_HEADS, LANE), tspec(FOX_HEADS, FOX_V_ROWS),
        tspec(DIFF_HEADS, LANE), tspec(DIFF_HEADS, LANE), tspec(DIFF_HEADS, DIFF_V_ROWS),
        row(FOX_HEADS * LANE), row(DIFF_HEADS * LANE), row(DIFF_HEADS * LANE), row(LANE),
        pl.BlockSpec((None, 8, LANE), lambda i: (i, 0, 0)),
    )
    return pl.pallas_call(
        functools.partial(_proj_kernel, bm=bm, steps_per_seq=spp),
        grid=(steps,),
        in_specs=[row(D_MODEL), once(g_mix), once(w_t), once(b_t), once(w_row), once(bf_pad)],
        out_specs=out_specs,
        out_shape=out_shape,
        scratch_shapes=[pltpu.VMEM((1, LANE), F32)],
        compiler_params=pltpu.CompilerParams(dimension_semantics=("arbitrary",),
                                             vmem_limit_bytes=VMEM_LIMIT),
        name="proj",
    )(x2, g_mix, w_t, b_t, w_row, bf_pad)


def _softmax_step(s_ref, m_ref, acc_ref, vt, d, tile, fix=None):
    half = tile // 2
    for c0 in (0, half):
        cols = slice(c0, c0 + half)
        s = s_ref[:, cols]
        if fix is not None:
            s = fix(s, c0)
        dc = d[:, cols]
        m_old = m_ref[:, cols]
        m_new = jnp.maximum(m_old, jnp.max(s, axis=0, keepdims=True) + dc)
        p = jnp.exp2(s - (m_new - dc))
        acc_ref[:, cols] = (jnp.exp2(m_old - m_new) * acc_ref[:, cols]
                            + jnp.dot(vt, p.astype(BF16), preferred_element_type=F32))
        m_ref[:, cols] = m_new


def _tile_loop(i, fill, step, last):
    fill(0, 0)

    def run(first, trips, width):
        def body(q, carry):
            j = first + width * q
            for u in range(width):
                fill((u + 1) % 2, j + u + 1)
                step(u % 2, j + u)
            return carry

        lax.fori_loop(0, trips, body, 0)

    run(0, i // 8, 8)
    run((i // 8) * 8, (i % 8) // 4, 4)
    run((i // 4) * 4, (i % 4) // 2, 2)

    @pl.when(i % 2 == 1)
    def _():
        fill(1, i)
        step(0, i - 1)
        last(1)

    @pl.when(i % 2 == 0)
    def _():
        last(0)


def _fox_kernel(c_ref, qt_ref, k_ref, vt_ref, f_ref, o_ref, sa_ref, sb_ref, m_ref, acc_ref, *, tile, nk):
    b, hh = pl.program_id(0), pl.program_id(1)
    base = (b * FOX_HEADS + hh) * nk
    bufs = (sa_ref, sb_ref)

    def q_block(i, carry):
        fi = f_ref[i] * LOG2E
        m_ref[...] = jnp.full_like(m_ref, NEG)
        acc_ref[...] = jnp.zeros_like(acc_ref)

        def fill(slot, j):
            off = pl.multiple_of(j * tile, tile)
            bufs[slot][...] = jnp.dot(k_ref[pl.ds(off, tile), :], qt_ref[i], preferred_element_type=F32)

        def step(slot, j):
            _softmax_step(bufs[slot], m_ref, acc_ref, vt_ref[j], fi - c_ref[base + j] * LOG2E, tile)

        def last(slot):
            key = lax.broadcasted_iota(jnp.int32, (tile, tile // 2), 0)
            qry = lax.broadcasted_iota(jnp.int32, (tile, tile // 2), 1)
            causal = lambda s, c0: jnp.where(key <= qry + c0, s, NEG)
            _softmax_step(bufs[slot], m_ref, acc_ref, vt_ref[i], fi - c_ref[base + i] * LOG2E, tile, causal)

        _tile_loop(i, fill, step, last)

        acc = acc_ref[...]
        o = acc[:FOX_HEAD_DIM] / acc[FOX_HEAD_DIM:FOX_HEAD_DIM + 1]
        o = jnp.concatenate([o, jnp.zeros((LANE - FOX_HEAD_DIM, tile), F32)], axis=0)
        o_ref[pl.ds(pl.multiple_of(i * tile, tile), tile), :] = o.T.astype(BF16)
        return carry

    lax.fori_loop(0, nk, q_block, 0)


def _fox_call(c_flat, qft, kf, vft, f_rows, tile):
    bsz, _, nq, _, _ = qft.shape
    seq = nq * tile
    grid_spec = pltpu.PrefetchScalarGridSpec(
        num_scalar_prefetch=1,
        grid=(bsz, FOX_HEADS),
        in_specs=[
            pl.BlockSpec((None, None, nq, LANE, tile), lambda b, h, c: (b, h, 0, 0, 0)),
            pl.BlockSpec((None, seq, LANE), lambda b, h, c: (b, 0, h)),
            pl.BlockSpec((None, None, nq, FOX_V_ROWS, tile), lambda b, h, c: (b, h, 0, 0, 0)),
            pl.BlockSpec((None, None, nq, 1, tile), lambda b, h, c: (b, h, 0, 0, 0)),
        ],
        out_specs=pl.BlockSpec((None, seq, LANE), lambda b, h, c: (b, 0, h)),
        scratch_shapes=[pltpu.VMEM((tile, tile), F32), pltpu.VMEM((tile, tile), F32),
                        pltpu.VMEM((1, tile), F32), pltpu.VMEM((FOX_V_ROWS, tile), F32)],
    )
    return pl.pallas_call(
        functools.partial(_fox_kernel, tile=tile, nk=nq),
        grid_spec=grid_spec,
        out_shape=jax.ShapeDtypeStruct((bsz, seq, FOX_HEADS * LANE), BF16),
        compiler_params=pltpu.CompilerParams(
            dimension_semantics=("parallel", "parallel"),
            vmem_limit_bytes=VMEM_LIMIT),
        name="fox_attn",
    )(c_flat, qft, kf, vft, f_rows)


def _diff_kernel(sc_ref, q1t_ref, q2t_ref, k1_ref, k2_ref, vt_ref, g_ref, o_ref,
                 s1a_ref, s1b_ref, s2a_ref, s2b_ref, m1_ref, a1_ref, m2_ref, a2_ref, bias_ref, *, tile, nk):
    hh = pl.program_id(1)
    slope = sc_ref[hh] * LOG2E
    lam = sc_ref[DIFF_HEADS]

    key = lax.broadcasted_iota(jnp.int32, (tile, tile), 0)
    qry = lax.broadcasted_iota(jnp.int32, (tile, tile), 1)
    bias = -slope * (key + jnp.abs(qry - key)).astype(F32)
    bias_ref[...] = jnp.where((key // CHUNK) <= (qry // CHUNK), bias, NEG)

    bufs1 = (s1a_ref, s1b_ref)
    bufs2 = (s2a_ref, s2b_ref)

    def q_block(i, carry):
        for m_ref, a_ref in ((m1_ref, a1_ref), (m2_ref, a2_ref)):
            m_ref[...] = jnp.full_like(m_ref, NEG)
            a_ref[...] = jnp.zeros_like(a_ref)
        qrel = lax.broadcasted_iota(jnp.int32, (1, tile), 1)

        def fill(slot, j):
            off = pl.multiple_of(j * tile, tile)
            bufs1[slot][...] = jnp.dot(k1_ref[pl.ds(off, tile), :], q1t_ref[i], preferred_element_type=F32)
            bufs2[slot][...] = jnp.dot(k2_ref[pl.ds(off, tile), :], q2t_ref[i], preferred_element_type=F32)

        def step(slot, j):
            d = slope * ((j - i) * tile - qrel).astype(F32)
            vt = vt_ref[j]
            _softmax_step(bufs1[slot], m1_ref, a1_ref, vt, d, tile)
            _softmax_step(bufs2[slot], m2_ref, a2_ref, vt, d, tile)

        def last(slot):
            half = tile // 2
            chunk_bias = lambda s, c0: s + bias_ref[:, c0:c0 + half]
            zero = jnp.zeros((1, tile), F32)
            vt = vt_ref[i]
            _softmax_step(bufs1[slot], m1_ref, a1_ref, vt, zero, tile, chunk_bias)
            _softmax_step(bufs2[slot], m2_ref, a2_ref, vt, zero, tile, chunk_bias)

        _tile_loop(i, fill, step, last)

        a1 = a1_ref[...]
        a2 = a2_ref[...]
        o1 = a1[:DIFF_V_DIM] / a1[DIFF_V_DIM:DIFF_V_DIM + 1]
        o2 = a2[:DIFF_V_DIM] / a2[DIFF_V_DIM:DIFF_V_DIM + 1]
        ob = o1 - lam * o2
        ob = ob * lax.rsqrt(jnp.mean(ob * ob, axis=0, keepdims=True) + EPS) * g_ref[...] * (1.0 - LAM_INIT)
        o_ref[pl.ds(pl.multiple_of(i * tile, tile), tile), :] = ob.T.astype(BF16)
        return carry

    lax.fori_loop(0, nk, q_block, 0)


def _diff_call(scal, q1t, q2t, k1, k2, vdt, g_col, tile):
    bsz, _, nq, _, _ = q1t.shape
    seq = nq * tile
    qspec = pl.BlockSpec((None, None, nq, LANE, tile), lambda b, h, c: (b, h, 0, 0, 0))
    kspec = pl.BlockSpec((None, seq, LANE), lambda b, h, c: (b, 0, h))
    score = pltpu.VMEM((tile, tile), F32)
    grid_spec = pltpu.PrefetchScalarGridSpec(
        num_scalar_prefetch=1,
        grid=(bsz, DIFF_HEADS),
        in_specs=[qspec, qspec, kspec, kspec,
                  pl.BlockSpec((None, None, nq, DIFF_V_ROWS, tile), lambda b, h, c: (b, h, 0, 0, 0)),
                  pl.BlockSpec((DIFF_V_DIM, 1), lambda b, h, c: (0, 0))],
        out_specs=pl.BlockSpec((None, seq, LANE), lambda b, h, c: (b, 0, h)),
        scratch_shapes=[score, score, score, score,
                        pltpu.VMEM((1, tile), F32), pltpu.VMEM((DIFF_V_ROWS, tile), F32),
                        pltpu.VMEM((1, tile), F32), pltpu.VMEM((DIFF_V_ROWS, tile), F32),
                        score],
    )
    return pl.pallas_call(
        functools.partial(_diff_kernel, tile=tile, nk=nq),
        grid_spec=grid_spec,
        out_shape=jax.ShapeDtypeStruct((bsz, seq, DIFF_HEADS * LANE), BF16),
        compiler_params=pltpu.CompilerParams(
            dimension_semantics=("parallel", "parallel"),
            vmem_limit_bytes=VMEM_LIMIT),
        name="diff_attn",
    )(scal, q1t, q2t, k1, k2, vdt, g_col)


def _merge_kernel(x_ref, oa_ref, ob_ref, gmix_ref, wg_ref, bg_ref, wpa_ref, wpb_ref, wo_ref,
                  gmoe_ref, wrh_ref, wrl_ref, br_ref,
                  x1_ref, h2_ref, mi_ref, mf_ref, cnt_ref, run_ref, *, bm):
    i = pl.program_id(0)

    @pl.when(i == 0)
    def _():
        run_ref[...] = jnp.zeros_like(run_ref)

    x = x_ref[...]
    h = _rms(x, gmix_ref[...]).astype(BF16)
    gates = jnp.dot(h, wg_ref[...], preferred_element_type=F32) + bg_ref[...]
    gates = 1.0 / (1.0 + jnp.exp(-gates))
    ya = jnp.dot(oa_ref[...], wpa_ref[...], preferred_element_type=F32)
    yb = jnp.dot(ob_ref[...], wpb_ref[...], preferred_element_type=F32)
    y = gates[:, :D_MODEL] * ya + gates[:, D_MODEL:] * yb
    x1 = x + jnp.dot(y.astype(BF16), wo_ref[...], preferred_element_type=F32)
    x1_ref[...] = x1
    h2 = _rms(x1, gmoe_ref[...])
    h2_ref[...] = h2

    h2h = h2.astype(BF16)
    h2l = (h2 - h2h.astype(F32)).astype(BF16)
    r = (jnp.dot(h2h, wrh_ref[...], preferred_element_type=F32)
         + jnp.dot(h2l, wrh_ref[...], preferred_element_type=F32)
         + jnp.dot(h2h, wrl_ref[...], preferred_element_type=F32)) + br_ref[...]
    el_all = r[:, :LANE]
    gl = r[:, LANE:]

    lane_i = lax.broadcasted_iota(jnp.int32, (bm, LANE), 1)
    lane = lane_i.astype(F32)
    big = float(LANE)

    def first_argmax(vals):
        vmax = jnp.max(vals, axis=1, keepdims=True)
        idx = jnp.min(jnp.where(vals == vmax, lane, big), axis=1, keepdims=True)
        return vmax, idx

    glm = jnp.where(lane_i < N_GROUPS, gl, NEG)
    gmax, g_idx = first_argmax(glm)
    g_w = 1.0 / jnp.sum(jnp.exp(glm - gmax), axis=1, keepdims=True)

    in_group = ((lane_i >> 3).astype(F32) == g_idx) & (lane_i < N_EXPERTS)
    elm = jnp.where(in_group, el_all, NEG)
    e1, idx1 = first_argmax(elm)
    elm2 = jnp.where(lane == idx1, NEG, elm)
    e2, idx2 = first_argmax(elm2)
    t = jnp.exp(e2 - e1)
    w1 = g_w / (1.0 + t)
    w2 = g_w * t / (1.0 + t)

    oh1 = (lane == idx1).astype(F32)
    oh2 = (lane == idx2).astype(F32)
    oh = oh1 + oh2
    row = lax.broadcasted_iota(jnp.int32, (bm, bm), 0)
    col = lax.broadcasted_iota(jnp.int32, (bm, bm), 1)
    strict = (col < row).astype(BF16)
    before = jnp.dot(strict, oh.astype(BF16), preferred_element_type=F32) + run_ref[...]
    rank1 = jnp.sum(oh1 * before, axis=1, keepdims=True)
    rank2 = jnp.sum(oh2 * before, axis=1, keepdims=True)
    run_ref[...] = run_ref[...] + jnp.sum(oh, axis=0, keepdims=True)
    cnt_ref[...] = jnp.broadcast_to(run_ref[...], cnt_ref.shape)

    meta = jnp.where(lane_i == 0, idx1, jnp.where(lane_i == 1, idx2,
                     jnp.where(lane_i == 2, rank1, jnp.where(lane_i == 3, rank2, 0.0))))
    mi_ref[...] = meta.T[:SUBLANES].astype(jnp.int32)
    mf_ref[...] = jnp.where(lane_i == 0, w1, jnp.where(lane_i == 1, w2, 0.0))


def _merge_call(x2, oa, ob, g_mix, w_g, b_g, w_pa, w_pb, w_o, g_moe, wr_hi, wr_lo, b_r):
    n = x2.shape[0]
    bm = MERGE_ROWS
    row = lambda w: pl.BlockSpec((bm, w), lambda i: (i, 0))
    full = lambda a: pl.BlockSpec(a.shape, lambda i: (0,) * a.ndim)
    consts = (g_mix, w_g, b_g, w_pa, w_pb, w_o, g_moe, wr_hi, wr_lo, b_r)
    return pl.pallas_call(
        functools.partial(_merge_kernel, bm=bm),
        grid=(n // bm,),
        in_specs=[row(D_MODEL), row(oa.shape[1]), row(ob.shape[1])] + [full(a) for a in consts],
        out_specs=(row(D_MODEL), row(D_MODEL),
                   pl.BlockSpec((None, SUBLANES, bm), lambda i: (i, 0, 0)), row(LANE),
                   pl.BlockSpec((8, LANE), lambda i: (0, 0))),
        out_shape=(jax.ShapeDtypeStruct((n, D_MODEL), F32),
                   jax.ShapeDtypeStruct((n, D_MODEL), F32),
                   jax.ShapeDtypeStruct((n // bm, SUBLANES, bm), jnp.int32),
                   jax.ShapeDtypeStruct((n, LANE), F32),
                   jax.ShapeDtypeStruct((8, LANE), F32)),
        scratch_shapes=[pltpu.VMEM((1, LANE), F32)],
        compiler_params=pltpu.CompilerParams(dimension_semantics=("arbitrary",),
                                             vmem_limit_bytes=VMEM_LIMIT),
        name="merge_router",
    )(x2, oa, ob, *consts)


def _dest_kernel(pstart_ref, meta_ref, dest_ref):
    for k in range(2):
        expert = meta_ref[k]
        dest = meta_ref[2 + k]
        for e in range(N_EXPERTS):
            dest = dest + jnp.where(expert == e, pstart_ref[e], 0)
        dest_ref[k] = dest


def _dest_call(pstart, meta):
    _, rows, lanes = meta.shape
    grid_spec = pltpu.PrefetchScalarGridSpec(
        num_scalar_prefetch=1,
        grid=(1,),
        in_specs=[pl.BlockSpec(meta.shape, lambda i, ps: (0, 0, 0))],
        out_specs=pl.BlockSpec((2, rows, lanes), lambda i, ps: (0, 0, 0)),
    )
    return pl.pallas_call(
        _dest_kernel,
        grid_spec=grid_spec,
        out_shape=jax.ShapeDtypeStruct((2, rows, lanes), jnp.int32),
        name="dest",
    )(pstart, meta)


def _row_copy(src_ref, src_row, dst_ref, dst_row, sem):
    return pltpu.make_async_copy(src_ref.at[pl.ds(src_row, 1), :],
                                 dst_ref.at[pl.ds(dst_row, 1), :], sem)


def _dispatch_kernel(d0_ref, d1_ref, tail_ref, h_ref, xs_ref, zero_ref, sem, zsem, *, bt, blk):
    i = pl.program_id(0)

    @pl.when(i == 0)
    def _():
        zero_ref[...] = jnp.zeros_like(zero_ref)

        def fill(e):
            start = pl.multiple_of(tail_ref[e] * blk, blk)
            return pltpu.make_async_copy(zero_ref, xs_ref.at[pl.ds(start, blk), :], zsem)

        for e in range(2 * N_EXPERTS):
            @pl.when(tail_ref[e] >= 0)
            def _():
                fill(e).start()

        for e in range(2 * N_EXPERTS):
            @pl.when(tail_ref[e] >= 0)
            def _():
                fill(e).wait()

    def issue(g, carry):
        base = pl.multiple_of(g * SUBLANES, SUBLANES)
        rows = h_ref.at[pl.ds(base, SUBLANES), :]
        for u in range(SUBLANES):
            t = i * bt + base + u
            _row_copy(rows, u, xs_ref, d0_ref[t], sem).start()
            _row_copy(rows, u, xs_ref, d1_ref[t], sem).start()
        return carry

    lax.fori_loop(0, bt // SUBLANES, issue, 0)

    for _ in range(2):
        pltpu.make_async_copy(h_ref, xs_ref.at[pl.ds(0, bt), :], sem).wait()


def _dispatch_call(dest0, dest1, tail_blk, h2, p_rows):
    n, d = h2.shape
    bt = MOVE_ROWS
    blk = MOE_ROWS
    grid_spec = pltpu.PrefetchScalarGridSpec(
        num_scalar_prefetch=3,
        grid=(n // bt,),
        in_specs=[pl.BlockSpec((bt, d), lambda i, d0, d1, tb: (i, 0))],
        out_specs=pl.BlockSpec(memory_space=pl.ANY),
        scratch_shapes=[pltpu.VMEM((blk, d), h2.dtype), pltpu.SemaphoreType.DMA(()),
                        pltpu.SemaphoreType.DMA(())],
    )
    return pl.pallas_call(
        functools.partial(_dispatch_kernel, bt=bt, blk=blk),
        grid_spec=grid_spec,
        out_shape=jax.ShapeDtypeStruct((p_rows, d), h2.dtype),
        compiler_params=pltpu.CompilerParams(dimension_semantics=("arbitrary",),
                                             vmem_limit_bytes=VMEM_LIMIT),
        name="dispatch",
    )(dest0, dest1, tail_blk, h2)


def _expert_kernel(blk_e_ref, used_ref, x_ref, w1_ref, w3_ref, w2_ref, y_ref, w1b, w3b, w2b):
    i = pl.program_id(0)
    live = i < used_ref[0]
    fresh = (i == 0) | (blk_e_ref[i] != blk_e_ref[jnp.maximum(i - 1, 0)])

    @pl.when(live & fresh)
    def _():
        w1b[...] = w1_ref[...].astype(BF16)
        w3b[...] = w3_ref[...].astype(BF16)
        w2b[...] = w2_ref[...].astype(BF16)

    @pl.when(live)
    def _():
        xb = x_ref[...].astype(BF16)
        a = jnp.dot(xb, w1b[...], preferred_element_type=F32)
        g = jnp.dot(xb, w3b[...], preferred_element_type=F32)
        mid = (a / (1.0 + jnp.exp(-a))) * g
        y_ref[...] = jnp.dot(mid.astype(BF16), w2b[...], preferred_element_type=F32)

    @pl.when(i >= used_ref[0])
    def _():
        y_ref[...] = jnp.zeros_like(y_ref)


def _expert_call(blk_e, n_used, xs, w1, w3, w2):
    p_rows, d = xs.shape
    blk = MOE_ROWS
    x_idx = lambda i, be, nu: (jnp.minimum(i, nu[0] - 1), 0)
    grid_spec = pltpu.PrefetchScalarGridSpec(
        num_scalar_prefetch=2,
        grid=(p_rows // blk,),
        in_specs=[pl.BlockSpec((blk, d), x_idx),
                  pl.BlockSpec((None, d, D_EXPERT), lambda i, be, nu: (be[i], 0, 0)),
                  pl.BlockSpec((None, d, D_EXPERT), lambda i, be, nu: (be[i], 0, 0)),
                  pl.BlockSpec((None, D_EXPERT, d), lambda i, be, nu: (be[i], 0, 0))],
        out_specs=pl.BlockSpec((blk, d), lambda i, be, nu: (i, 0)),
        scratch_shapes=[pltpu.VMEM((d, D_EXPERT), BF16), pltpu.VMEM((d, D_EXPERT), BF16),
                        pltpu.VMEM((D_EXPERT, d), BF16)],
    )
    return pl.pallas_call(
        _expert_kernel,
        grid_spec=grid_spec,
        out_shape=jax.ShapeDtypeStruct((p_rows, d), F32),
        compiler_params=pltpu.CompilerParams(dimension_semantics=("arbitrary",),
                                             vmem_limit_bytes=VMEM_LIMIT),
        name="experts",
    )(blk_e, n_used, xs, w1, w3, w2)


def _combine_kernel(d0_ref, d1_ref, x1_ref, mf_ref, g_ref, yb_ref, o_ref, buf0, buf1, sem, *, bt):
    i = pl.program_id(0)

    def issue(g, carry):
        base = pl.multiple_of(g * SUBLANES, SUBLANES)
        rows0 = buf0.at[pl.ds(base, SUBLANES), :]
        rows1 = buf1.at[pl.ds(base, SUBLANES), :]
        for u in range(SUBLANES):
            t = i * bt + base + u
            _row_copy(yb_ref, d0_ref[t], rows0, u, sem).start()
            _row_copy(yb_ref, d1_ref[t], rows1, u, sem).start()
        return carry

    lax.fori_loop(0, bt // SUBLANES, issue, 0)

    for buf in (buf0, buf1):
        pltpu.make_async_copy(yb_ref.at[pl.ds(0, bt), :], buf, sem).wait()

    mf = mf_ref[...]
    x2 = x1_ref[...] + mf[:, 0:1] * buf0[...] + mf[:, 1:2] * buf1[...]
    o_ref[...] = _rms(x2, g_ref[...])


def _combine_call(dest0, dest1, x1, mf, g_final, yb):
    n, d = x1.shape
    bt = MOVE_ROWS
    grid_spec = pltpu.PrefetchScalarGridSpec(
        num_scalar_prefetch=2,
        grid=(n // bt,),
        in_specs=[pl.BlockSpec((bt, d), lambda i, d0, d1: (i, 0)),
                  pl.BlockSpec((bt, LANE), lambda i, d0, d1: (i, 0)),
                  pl.BlockSpec((1, d), lambda i, d0, d1: (0, 0)),
                  pl.BlockSpec(memory_space=pl.ANY)],
        out_specs=pl.BlockSpec((bt, d), lambda i, d0, d1: (i, 0)),
        scratch_shapes=[pltpu.VMEM((bt, d), F32), pltpu.VMEM((bt, d), F32),
                        pltpu.SemaphoreType.DMA(())],
    )
    return pl.pallas_call(
        functools.partial(_combine_kernel, bt=bt),
        grid_spec=grid_spec,
        out_shape=jax.ShapeDtypeStruct((n, d), F32),
        compiler_params=pltpu.CompilerParams(dimension_semantics=("arbitrary",),
                                             vmem_limit_bytes=VMEM_LIMIT),
        name="combine",
    )(dest0, dest1, x1, mf, g_final, yb)


def _pad_heads(w, heads, width, padded, scale=1.0):
    d = w.shape[0]
    w = (w * scale).reshape(d, heads, width)
    return jnp.pad(w, ((0, 0), (0, 0), (0, padded - width))).reshape(d, heads * padded)


def _build_proj_params(w_in, b_fgate):
    fw, dw = FOX_HEADS * FOX_HEAD_DIM, DIFF_HEADS * DIFF_V_DIM
    o = 0
    fq, fk, fv = (w_in[:, o + k * fw:o + (k + 1) * fw] for k in range(3))
    o += 3 * fw
    ff = w_in[:, o:o + FOX_HEADS]
    o += FOX_HEADS
    dq, dk, dv = (w_in[:, o + k * dw:o + (k + 1) * dw] for k in range(3))
    d = w_in.shape[0]
    dq = dq.reshape(d, DIFF_HEADS, 2, DIFF_HEAD_DIM)
    dk = dk.reshape(d, DIFF_HEADS, 2, DIFF_HEAD_DIM)
    fscale = FOX_HEAD_DIM ** -0.5 * LOG2E
    dscale = DIFF_HEAD_DIM ** -0.5 * LOG2E
    half = lambda t, k: t[:, :, k, :].reshape(d, DIFF_HEADS * DIFF_HEAD_DIM)
    w_t = jnp.concatenate([
        _pad_heads(fq, FOX_HEADS, FOX_HEAD_DIM, LANE, fscale),
        _pad_heads(fv, FOX_HEADS, FOX_HEAD_DIM, FOX_V_ROWS),
        _pad_heads(half(dq, 0), DIFF_HEADS, DIFF_HEAD_DIM, LANE, dscale),
        _pad_heads(half(dq, 1), DIFF_HEADS, DIFF_HEAD_DIM, LANE, dscale),
        _pad_heads(dv, DIFF_HEADS, DIFF_V_DIM, DIFF_V_ROWS),
    ], axis=1).T.astype(BF16)
    w_row = jnp.concatenate([
        _pad_heads(fk, FOX_HEADS, FOX_HEAD_DIM, LANE),
        jnp.pad(ff, ((0, 0), (0, LANE - FOX_HEADS))),
        _pad_heads(half(dk, 0), DIFF_HEADS, DIFF_HEAD_DIM, LANE),
        _pad_heads(half(dk, 1), DIFF_HEADS, DIFF_HEAD_DIM, LANE),
    ], axis=1).astype(BF16)

    slopes = [2.0 ** (-8.0 / DIFF_HEADS * (hh + 1)) for hh in range(DIFF_HEADS)]
    b = jnp.zeros((_T_ROWS,), F32)
    for hh in range(FOX_HEADS):
        b = b.at[_TQF + hh * LANE + FOX_HEAD_DIM:_TQF + hh * LANE + FOX_HEAD_DIM + 3].set(-1.0)
        b = b.at[_TVF + hh * FOX_V_ROWS + FOX_HEAD_DIM].set(1.0)
    for hh in range(DIFF_HEADS):
        for base in (_TQ1, _TQ2):
            b = b.at[base + hh * LANE + DIFF_HEAD_DIM:base + hh * LANE + DIFF_HEAD_DIM + 3].set(slopes[hh])
        b = b.at[_TVD + hh * DIFF_V_ROWS + DIFF_V_DIM].set(1.0)
    bf_pad = jnp.pad(b_fgate, (0, LANE - FOX_HEADS))[None, :]
    return w_t, b[:, None], w_row, bf_pad, slopes


def kernel(x, g_mix, w_in, b_fgate, b_gate, lam_q1, lam_k1, lam_q2, lam_k2, g_subln, w_pa, w_pb, w_o,
           g_moe, w_group, b_group, w_expert, b_expert, w1, w3, w2, g_final):
    bsz, seq, d = x.shape
    n = bsz * seq
    tile = ATT_TILE
    nk = seq // tile
    x2 = x.reshape(n, d)

    w_t, b_t, w_row, bf_pad, slopes = _build_proj_params(w_in[0], b_fgate[0])
    qft, vft, q1t, q2t, vdt, kf, k1, k2, fcol, cblk = _proj_call(x2, g_mix, w_t, b_t, w_row, bf_pad, bsz, seq)
    shp = lambda a: a.reshape(bsz, seq, a.shape[1])
    c_flat = cblk[:, 0, :FOX_HEADS].reshape(bsz, nk, FOX_HEADS).transpose(0, 2, 1).reshape(-1)
    f_rows = (fcol[:, :FOX_HEADS].reshape(bsz, nk, tile, FOX_HEADS).transpose(0, 3, 1, 2)
              .reshape(bsz, FOX_HEADS, nk, 1, tile))

    oa = _fox_call(c_flat, qft, shp(kf), vft, f_rows, tile)
    lam = (jnp.exp(jnp.sum(lam_q1[0] * lam_k1[0])) - jnp.exp(jnp.sum(lam_q2[0] * lam_k2[0])) + LAM_INIT)
    scal = jnp.concatenate([jnp.asarray(slopes, F32), lam[None].astype(F32)])
    ob = _diff_call(scal, q1t, q2t, shp(k1), shp(k2), vdt, g_subln[0][:, None], tile)

    o = 3 * FOX_HEADS * FOX_HEAD_DIM + FOX_HEADS + 3 * DIFF_HEADS * DIFF_V_DIM
    w_g = w_in[0][:, o:].astype(BF16)
    w_pa_pad = jnp.pad(w_pa[0].reshape(FOX_HEADS, FOX_HEAD_DIM, d),
                       ((0, 0), (0, LANE - FOX_HEAD_DIM), (0, 0))).reshape(FOX_HEADS * LANE, d).astype(BF16)
    w_r = jnp.zeros((d, 2 * LANE), F32)
    w_r = w_r.at[:, :N_EXPERTS].set(w_expert[0]).at[:, LANE:LANE + N_GROUPS].set(w_group[0])
    wr_hi = w_r.astype(BF16)
    wr_lo = (w_r - wr_hi.astype(F32)).astype(BF16)
    b_r = jnp.zeros((1, 2 * LANE), F32)
    b_r = b_r.at[0, :N_EXPERTS].set(b_expert[0].reshape(-1)).at[0, LANE:LANE + N_GROUPS].set(b_group[0])
    x1, h2, mi, mf, cnt = _merge_call(
        x2, oa.reshape(n, -1), ob.reshape(n, -1), g_mix, w_g, b_gate, w_pa_pad,
        w_pb[0].astype(BF16), w_o[0].astype(BF16), g_moe, wr_hi, wr_lo, b_r)

    blk = MOE_ROWS
    counts = cnt[0, :N_EXPERTS].astype(jnp.int32)
    padded = (counts + blk - 1) // blk * blk
    pend = jnp.cumsum(padded)
    pstart = pend - padded
    meta = mi.transpose(1, 0, 2).reshape(SUBLANES, n // LANE, LANE)
    dest = _dest_call(pstart.astype(jnp.int32), meta).reshape(2, n)
    dest0, dest1 = dest[0], dest[1]
    p_rows = n * 2 + N_EXPERTS * blk
    n_blk = p_rows // blk
    blk_pos = jnp.arange(n_blk, dtype=jnp.int32) * blk
    blk_e = jnp.minimum(jnp.sum(pend[None, :] <= blk_pos[:, None], axis=1), N_EXPERTS - 1).astype(jnp.int32)
    n_used = (pend[-1:] // blk).astype(jnp.int32)

    spare = n_used[0] + jnp.arange(N_EXPERTS, dtype=jnp.int32)
    tail_blk = jnp.concatenate([jnp.where(padded > 0, pend // blk - 1, -1),
                                jnp.where(spare < n_blk, spare, -1)]).astype(jnp.int32)
    xs = _dispatch_call(dest0, dest1, tail_blk, h2, p_rows)
    yb = _expert_call(blk_e, n_used, xs, w1[0], w3[0], w2[0])
    out = _combine_call(dest0, dest1, x1, mf, g_final[None, :], yb)
    return out.reshape(bsz, seq, d)
```

```python
import functools

import jax
import jax.numpy as jnp
from jax import lax
from jax.experimental import pallas as pl
from jax.experimental.pallas import tpu as pltpu

F32 = jnp.float32
BF16 = jnp.bfloat16

D_MODEL = 1024
FOX_HEADS = 8
FOX_HEAD_DIM = 64
DIFF_HEADS = 4
DIFF_HEAD_DIM = 64
DIFF_V_DIM = 128
CHUNK = 64
N_GROUPS = 4
EXPERTS_PER_GROUP = 8
N_EXPERTS = 32
D_EXPERT = 512
EPS = 1e-6
LAM_INIT = 0.8 - 0.6 * 1.0

LANE = 128
SUBLANES = 8
NEG = -1e30
LOG2E = 1.4426950408889634
ATT_TILE = 512
MERGE_ROWS = 512
MOE_ROWS = 512
MOVE_ROWS = 1024
VMEM_LIMIT = 56 * 1024 * 1024

FOX_V_ROWS = 80
DIFF_V_ROWS = 144

_TQF, _TVF, _TQ1 = 0, 1024, 1024 + FOX_HEADS * FOX_V_ROWS
_TQ2, _TVD = _TQ1 + 512, _TQ1 + 1024
_T_ROWS = _TVD + DIFF_HEADS * DIFF_V_ROWS
_KF, _FF, _K1, _K2, _ROW_COLS = 0, 1024, 1152, 1664, 2176


def _rms(x, g):
    return x * lax.rsqrt(jnp.mean(x * x, axis=-1, keepdims=True) + EPS) * g


def _split3(r):
    r0 = r.astype(BF16).astype(F32)
    r1 = (r - r0).astype(BF16).astype(F32)
    r2 = (r - r0 - r1).astype(BF16).astype(F32)
    return r0, r1, r2


_NT = (((1,), (1,)), ((), ()))


def _proj_kernel(x_ref, g_ref, wt_ref, bt_ref, w_ref, bf_ref,
                 qft_ref, vft_ref, q1t_ref, q2t_ref, vdt_ref, kf_ref, k1_ref, k2_ref,
                 fcol_ref, cblk_ref, carry_ref, *, bm, steps_per_seq):
    i = pl.program_id(0)
    h = _rms(x_ref[...], g_ref[...]).astype(BF16)

    zt = lax.dot_general(wt_ref[...], h, _NT, preferred_element_type=F32) + bt_ref[...]

    def heads(lo, n_heads, rows):
        return zt[lo:lo + n_heads * rows].reshape(n_heads, rows, bm).astype(BF16)

    qft_ref[...] = heads(_TQF, FOX_HEADS, LANE)
    vft_ref[...] = heads(_TVF, FOX_HEADS, FOX_V_ROWS)
    q1t_ref[...] = heads(_TQ1, DIFF_HEADS, LANE)
    q2t_ref[...] = heads(_TQ2, DIFF_HEADS, LANE)
    vdt_ref[...] = heads(_TVD, DIFF_HEADS, DIFF_V_ROWS)

    z = jnp.dot(h, w_ref[...], preferred_element_type=F32)

    zf = z[:, _FF:_K1] + bf_ref[...]
    logf = jnp.minimum(zf, 0.0) - jnp.log1p(jnp.exp(-jnp.abs(zf)))

    @pl.when(i % steps_per_seq == 0)
    def _():
        carry_ref[...] = jnp.zeros_like(carry_ref)

    c = carry_ref[...]
    row = lax.broadcasted_iota(jnp.int32, (bm, bm), 0)
    col = lax.broadcasted_iota(jnp.int32, (bm, bm), 1)
    tri = (col <= row).astype(BF16)
    l0, l1, l2 = _split3(logf)
    rel = (jnp.dot(tri, l0.astype(BF16), preferred_element_type=F32)
           + jnp.dot(tri, l1.astype(BF16), preferred_element_type=F32)
           + jnp.dot(tri, l2.astype(BF16), preferred_element_type=F32))
    fcum = rel + c
    carry_ref[...] = fcum[bm - 1:bm, :]
    fcol_ref[...] = fcum
    cblk_ref[...] = jnp.broadcast_to(c, cblk_ref.shape)

    lane = lax.broadcasted_iota(jnp.int32, (bm, LANE), 1)
    rel2 = rel * LOG2E
    for hh in range(FOX_HEADS):
        r0, r1, r2 = _split3(rel2[:, hh:hh + 1])
        aug = jnp.where(lane == FOX_HEAD_DIM, r0,
                        jnp.where(lane == FOX_HEAD_DIM + 1, r1,
                                  jnp.where(lane == FOX_HEAD_DIM + 2, r2, 0.0)))
        lo = _KF + hh * LANE
        kf_ref[:, hh * LANE:(hh + 1) * LANE] = (z[:, lo:lo + LANE] + aug).astype(BF16)

    width = DIFF_HEADS * LANE
    jrel = lax.broadcasted_iota(jnp.int32, (bm, width), 0)
    lane4 = lax.broadcasted_iota(jnp.int32, (bm, width), 1) & (LANE - 1)
    j0, j1, j2 = _split3(jrel.astype(F32) * LOG2E)
    augk = jnp.where(lane4 == DIFF_HEAD_DIM, j0,
                     jnp.where(lane4 == DIFF_HEAD_DIM + 1, j1,
                               jnp.where(lane4 == DIFF_HEAD_DIM + 2, j2, 0.0)))
    k1_ref[...] = (z[:, _K1:_K2] + augk).astype(BF16)
    k2_ref[...] = (z[:, _K2:_ROW_COLS] + augk).astype(BF16)


def _proj_call(x2, g_mix, w_t, b_t, w_row, bf_pad, bsz, seq):
    n = x2.shape[0]
    bm = ATT_TILE
    steps = n // bm
    spp = seq // bm
    row = lambda w: pl.BlockSpec((bm, w), lambda i: (i, 0))
    once = lambda a: pl.BlockSpec(a.shape, lambda i: (0,) * a.ndim, pipeline_mode=pl.Buffered(1))
    tshape = lambda heads, rows: jax.ShapeDtypeStruct((bsz, heads, spp, rows, bm), BF16)
    tspec = lambda heads, rows: pl.BlockSpec((None, heads, None, rows, bm),
                                             lambda i: (i // spp, 0, i % spp, 0, 0))
    out_shape = (
        tshape(FOX_HEADS, LANE), tshape(FOX_HEADS, FOX_V_ROWS),
        tshape(DIFF_HEADS, LANE), tshape(DIFF_HEADS, LANE), tshape(DIFF_HEADS, DIFF_V_ROWS),
        jax.ShapeDtypeStruct((n, FOX_HEADS * LANE), BF16),
        jax.ShapeDtypeStruct((n, DIFF_HEADS * LANE), BF16),
        jax.ShapeDtypeStruct((n, DIFF_HEADS * LANE), BF16),
        jax.ShapeDtypeStruct((n, LANE), F32),
        jax.ShapeDtypeStruct((steps, 8, LANE), F32),
    )
    out_specs = (
        tspec(FOX_HEADS, LANE), tspec(FOX_HEADS, FOX_V_ROWS),
        tspec(DIFF_HEADS, LANE), tspec(DIFF_HEADS, LANE), tspec(DIFF_HEADS, DIFF_V_ROWS),
        row(FOX_HEADS * LANE), row(DIFF_HEADS * LANE), row(DIFF_HEADS * LANE), row(LANE),
        pl.BlockSpec((None, 8, LANE), lambda i: (i, 0, 0)),
    )
    return pl.pallas_call(
        functools.partial(_proj_kernel, bm=bm, steps_per_seq=spp),
        grid=(steps,),
        in_specs=[row(D_MODEL), once(g_mix), once(w_t), once(b_t), once(w_row), once(bf_pad)],
        out_specs=out_specs,
        out_shape=out_shape,
        scratch_shapes=[pltpu.VMEM((1, LANE), F32)],
        compiler_params=pltpu.CompilerParams(dimension_semantics=("arbitrary",),
                                             vmem_limit_bytes=VMEM_LIMIT),
        name="proj",
    )(x2, g_mix, w_t, b_t, w_row, bf_pad)


def _softmax_step(s_ref, m_ref, acc_ref, vt, d, tile, fix=None):
    half = tile // 2
    for c0 in (0, half):
        cols = slice(c0, c0 + half)
        s = s_ref[:, cols]
        if fix is not None:
            s = fix(s, c0)
        dc = d[:, cols]
        m_old = m_ref[:, cols]
        m_new = jnp.maximum(m_old, jnp.max(s, axis=0, keepdims=True) + dc)
        p = jnp.exp2(s - (m_new - dc))
        acc_ref[:, cols] = (jnp.exp2(m_old - m_new) * acc_ref[:, cols]
                            + jnp.dot(vt, p.astype(BF16), preferred_element_type=F32))
        m_ref[:, cols] = m_new


def _tile_loop(i, fill, step, last):
    fill(0, 0)

    def run(first, trips, width):
        def body(q, carry):
            j = first + width * q
            for u in range(width):
                fill((u + 1) % 2, j + u + 1)
                step(u % 2, j + u)
            return carry

        lax.fori_loop(0, trips, body, 0)

    run(0, i // 8, 8)
    run((i // 8) * 8, (i % 8) // 4, 4)
    run((i // 4) * 4, (i % 4) // 2, 2)

    @pl.when(i % 2 == 1)
    def _():
        fill(1, i)
        step(0, i - 1)
        last(1)

    @pl.when(i % 2 == 0)
    def _():
        last(0)


def _fox_kernel(c_ref, qt_ref, k_ref, vt_ref, f_ref, o_ref, sa_ref, sb_ref, m_ref, acc_ref, *, tile, nk):
    b, hh = pl.program_id(0), pl.program_id(1)
    base = (b * FOX_HEADS + hh) * nk
    bufs = (sa_ref, sb_ref)

    def q_block(i, carry):
        fi = f_ref[i] * LOG2E
        m_ref[...] = jnp.full_like(m_ref, NEG)
        acc_ref[...] = jnp.zeros_like(acc_ref)

        def fill(slot, j):
            off = pl.multiple_of(j * tile, tile)
            bufs[slot][...] = jnp.dot(k_ref[pl.ds(off, tile), :], qt_ref[i], preferred_element_type=F32)

        def step(slot, j):
            _softmax_step(bufs[slot], m_ref, acc_ref, vt_ref[j], fi - c_ref[base + j] * LOG2E, tile)

        def last(slot):
            key = lax.broadcasted_iota(jnp.int32, (tile, tile // 2), 0)
            qry = lax.broadcasted_iota(jnp.int32, (tile, tile // 2), 1)
            causal = lambda s, c0: jnp.where(key <= qry + c0, s, NEG)
            _softmax_step(bufs[slot], m_ref, acc_ref, vt_ref[i], fi - c_ref[base + i] * LOG2E, tile, causal)

        _tile_loop(i, fill, step, last)

        acc = acc_ref[...]
        o = acc[:FOX_HEAD_DIM] / acc[FOX_HEAD_DIM:FOX_HEAD_DIM + 1]
        o = jnp.concatenate([o, jnp.zeros((LANE - FOX_HEAD_DIM, tile), F32)], axis=0)
        o_ref[pl.ds(pl.multiple_of(i * tile, tile), tile), :] = o.T.astype(BF16)
        return carry

    lax.fori_loop(0, nk, q_block, 0)


def _fox_call(c_flat, qft, kf, vft, f_rows, tile):
    bsz, _, nq, _, _ = qft.shape
    seq = nq * tile
    grid_spec = pltpu.PrefetchScalarGridSpec(
        num_scalar_prefetch=1,
        grid=(bsz, FOX_HEADS),
        in_specs=[
            pl.BlockSpec((None, None, nq, LANE, tile), lambda b, h, c: (b, h, 0, 0, 0)),
            pl.BlockSpec((None, seq, LANE), lambda b, h, c: (b, 0, h)),
            pl.BlockSpec((None, None, nq, FOX_V_ROWS, tile), lambda b, h, c: (b, h, 0, 0, 0)),
            pl.BlockSpec((None, None, nq, 1, tile), lambda b, h, c: (b, h, 0, 0, 0)),
        ],
        out_specs=pl.BlockSpec((None, seq, LANE), lambda b, h, c: (b, 0, h)),
        scratch_shapes=[pltpu.VMEM((tile, tile), F32), pltpu.VMEM((tile, tile), F32),
                        pltpu.VMEM((1, tile), F32), pltpu.VMEM((FOX_V_ROWS, tile), F32)],
    )
    return pl.pallas_call(
        functools.partial(_fox_kernel, tile=tile, nk=nq),
        grid_spec=grid_spec,
        out_shape=jax.ShapeDtypeStruct((bsz, seq, FOX_HEADS * LANE), BF16),
        compiler_params=pltpu.CompilerParams(
            dimension_semantics=("parallel", "parallel"),
            vmem_limit_bytes=VMEM_LIMIT),
        name="fox_attn",
    )(c_flat, qft, kf, vft, f_rows)


def _diff_kernel(sc_ref, q1t_ref, q2t_ref, k1_ref, k2_ref, vt_ref, g_ref, o_ref,
                 s1a_ref, s1b_ref, s2a_ref, s2b_ref, m1_ref, a1_ref, m2_ref, a2_ref, bias_ref, *, tile, nk):
    hh = pl.program_id(1)
    slope = sc_ref[hh] * LOG2E
    lam = sc_ref[DIFF_HEADS]

    key = lax.broadcasted_iota(jnp.int32, (tile, tile), 0)
    qry = lax.broadcasted_iota(jnp.int32, (tile, tile), 1)
    bias = -slope * (key + jnp.abs(qry - key)).astype(F32)
    bias_ref[...] = jnp.where((key // CHUNK) <= (qry // CHUNK), bias, NEG)

    bufs1 = (s1a_ref, s1b_ref)
    bufs2 = (s2a_ref, s2b_ref)

    def q_block(i, carry):
        for m_ref, a_ref in ((m1_ref, a1_ref), (m2_ref, a2_ref)):
            m_ref[...] = jnp.full_like(m_ref, NEG)
            a_ref[...] = jnp.zeros_like(a_ref)
        qrel = lax.broadcasted_iota(jnp.int32, (1, tile), 1)

        def fill(slot, j):
            off = pl.multiple_of(j * tile, tile)
            bufs1[slot][...] = jnp.dot(k1_ref[pl.ds(off, tile), :], q1t_ref[i], preferred_element_type=F32)
            bufs2[slot][...] = jnp.dot(k2_ref[pl.ds(off, tile), :], q2t_ref[i], preferred_element_type=F32)

        def step(slot, j):
            d = slope * ((j - i) * tile - qrel).astype(F32)
            vt = vt_ref[j]
            _softmax_step(bufs1[slot], m1_ref, a1_ref, vt, d, tile)
            _softmax_step(bufs2[slot], m2_ref, a2_ref, vt, d, tile)

        def last(slot):
            half = tile // 2
            chunk_bias = lambda s, c0: s + bias_ref[:, c0:c0 + half]
            zero = jnp.zeros((1, tile), F32)
            vt = vt_ref[i]
            _softmax_step(bufs1[slot], m1_ref, a1_ref, vt, zero, tile, chunk_bias)
            _softmax_step(bufs2[slot], m2_ref, a2_ref, vt, zero, tile, chunk_bias)

        _tile_loop(i, fill, step, last)

        a1 = a1_ref[...]
        a2 = a2_ref[...]
        o1 = a1[:DIFF_V_DIM] / a1[DIFF_V_DIM:DIFF_V_DIM + 1]
        o2 = a2[:DIFF_V_DIM] / a2[DIFF_V_DIM:DIFF_V_DIM + 1]
        ob = o1 - lam * o2
        ob = ob * lax.rsqrt(jnp.mean(ob * ob, axis=0, keepdims=True) + EPS) * g_ref[...] * (1.0 - LAM_INIT)
        o_ref[pl.ds(pl.multiple_of(i * tile, tile), tile), :] = ob.T.astype(BF16)
        return carry

    lax.fori_loop(0, nk, q_block, 0)


def _diff_call(scal, q1t, q2t, k1, k2, vdt, g_col, tile):
    bsz, _, nq, _, _ = q1t.shape
    seq = nq * tile
    qspec = pl.BlockSpec((None, None, nq, LANE, tile), lambda b, h, c: (b, h, 0, 0, 0))
    kspec = pl.BlockSpec((None, seq, LANE), lambda b, h, c: (b, 0, h))
    score = pltpu.VMEM((tile, tile), F32)
    grid_spec = pltpu.PrefetchScalarGridSpec(
        num_scalar_prefetch=1,
        grid=(bsz, DIFF_HEADS),
        in_specs=[qspec, qspec, kspec, kspec,
                  pl.BlockSpec((None, None, nq, DIFF_V_ROWS, tile), lambda b, h, c: (b, h, 0, 0, 0)),
                  pl.BlockSpec((DIFF_V_DIM, 1), lambda b, h, c: (0, 0))],
        out_specs=pl.BlockSpec((None, seq, LANE), lambda b, h, c: (b, 0, h)),
        scratch_shapes=[score, score, score, score,
                        pltpu.VMEM((1, tile), F32), pltpu.VMEM((DIFF_V_ROWS, tile), F32),
                        pltpu.VMEM((1, tile), F32), pltpu.VMEM((DIFF_V_ROWS, tile), F32),
                        score],
    )
    return pl.pallas_call(
        functools.partial(_diff_kernel, tile=tile, nk=nq),
        grid_spec=grid_spec,
        out_shape=jax.ShapeDtypeStruct((bsz, seq, DIFF_HEADS * LANE), BF16),
        compiler_params=pltpu.CompilerParams(
            dimension_semantics=("parallel", "parallel"),
            vmem_limit_bytes=VMEM_LIMIT),
        name="diff_attn",
    )(scal, q1t, q2t, k1, k2, vdt, g_col)


def _merge_kernel(x_ref, oa_ref, ob_ref, gmix_ref, wg_ref, bg_ref, wpa_ref, wpb_ref, wo_ref,
                  gmoe_ref, wrh_ref, wrl_ref, br_ref,
                  x1_ref, h2_ref, mi_ref, mf_ref, cnt_ref, run_ref, *, bm):
    i = pl.program_id(0)

    @pl.when(i == 0)
    def _():
        run_ref[...] = jnp.zeros_like(run_ref)

    x = x_ref[...]
    h = _rms(x, gmix_ref[...]).astype(BF16)
    gates = jnp.dot(h, wg_ref[...], preferred_element_type=F32) + bg_ref[...]
    gates = 1.0 / (1.0 + jnp.exp(-gates))
    ya = jnp.dot(oa_ref[...], wpa_ref[...], preferred_element_type=F32)
    yb = jnp.dot(ob_ref[...], wpb_ref[...], preferred_element_type=F32)
    y = gates[:, :D_MODEL] * ya + gates[:, D_MODEL:] * yb
    x1 = x + jnp.dot(y.astype(BF16), wo_ref[...], preferred_element_type=F32)
    x1_ref[...] = x1
    h2 = _rms(x1, gmoe_ref[...])
    h2_ref[...] = h2

    h2h = h2.astype(BF16)
    h2l = (h2 - h2h.astype(F32)).astype(BF16)
    r = (jnp.dot(h2h, wrh_ref[...], preferred_element_type=F32)
         + jnp.dot(h2l, wrh_ref[...], preferred_element_type=F32)
         + jnp.dot(h2h, wrl_ref[...], preferred_element_type=F32)) + br_ref[...]
    el_all = r[:, :LANE]
    gl = r[:, LANE:]

    lane_i = lax.broadcasted_iota(jnp.int32, (bm, LANE), 1)
    lane = lane_i.astype(F32)
    big = float(LANE)

    def first_argmax(vals):
        vmax = jnp.max(vals, axis=1, keepdims=True)
        idx = jnp.min(jnp.where(vals == vmax, lane, big), axis=1, keepdims=True)
        return vmax, idx

    glm = jnp.where(lane_i < N_GROUPS, gl, NEG)
    gmax, g_idx = first_argmax(glm)
    g_w = 1.0 / jnp.sum(jnp.exp(glm - gmax), axis=1, keepdims=True)

    in_group = ((lane_i >> 3).astype(F32) == g_idx) & (lane_i < N_EXPERTS)
    elm = jnp.where(in_group, el_all, NEG)
    e1, idx1 = first_argmax(elm)
    elm2 = jnp.where(lane == idx1, NEG, elm)
    e2, idx2 = first_argmax(elm2)
    t = jnp.exp(e2 - e1)
    w1 = g_w / (1.0 + t)
    w2 = g_w * t / (1.0 + t)

    oh1 = (lane == idx1).astype(F32)
    oh2 = (lane == idx2).astype(F32)
    oh = oh1 + oh2
    row = lax.broadcasted_iota(jnp.int32, (bm, bm), 0)
    col = lax.broadcasted_iota(jnp.int32, (bm, bm), 1)
    strict = (col < row).astype(BF16)
    before = jnp.dot(strict, oh.astype(BF16), preferred_element_type=F32) + run_ref[...]
    rank1 = jnp.sum(oh1 * before, axis=1, keepdims=True)
    rank2 = jnp.sum(oh2 * before, axis=1, keepdims=True)
    run_ref[...] = run_ref[...] + jnp.sum(oh, axis=0, keepdims=True)
    cnt_ref[...] = jnp.broadcast_to(run_ref[...], cnt_ref.shape)

    meta = jnp.where(lane_i == 0, idx1, jnp.where(lane_i == 1, idx2,
                     jnp.where(lane_i == 2, rank1, jnp.where(lane_i == 3, rank2, 0.0))))
    mi_ref[...] = meta.T[:SUBLANES].astype(jnp.int32)
    mf_ref[...] = jnp.where(lane_i == 0, w1, jnp.where(lane_i == 1, w2, 0.0))


def _merge_call(x2, oa, ob, g_mix, w_g, b_g, w_pa, w_pb, w_o, g_moe, wr_hi, wr_lo, b_r):
    n = x2.shape[0]
    bm = MERGE_ROWS
    row = lambda w: pl.BlockSpec((bm, w), lambda i: (i, 0))
    full = lambda a: pl.BlockSpec(a.shape, lambda i: (0,) * a.ndim)
    consts = (g_mix, w_g, b_g, w_pa, w_pb, w_o, g_moe, wr_hi, wr_lo, b_r)
    return pl.pallas_call(
        functools.partial(_merge_kernel, bm=bm),
        grid=(n // bm,),
        in_specs=[row(D_MODEL), row(oa.shape[1]), row(ob.shape[1])] + [full(a) for a in consts],
        out_specs=(row(D_MODEL), row(D_MODEL),
                   pl.BlockSpec((None, SUBLANES, bm), lambda i: (i, 0, 0)), row(LANE),
                   pl.BlockSpec((8, LANE), lambda i: (0, 0))),
        out_shape=(jax.ShapeDtypeStruct((n, D_MODEL), F32),
                   jax.ShapeDtypeStruct((n, D_MODEL), F32),
                   jax.ShapeDtypeStruct((n // bm, SUBLANES, bm), jnp.int32),
                   jax.ShapeDtypeStruct((n, LANE), F32),
                   jax.ShapeDtypeStruct((8, LANE), F32)),
        scratch_shapes=[pltpu.VMEM((1, LANE), F32)],
        compiler_params=pltpu.CompilerParams(dimension_semantics=("arbitrary",),
                                             vmem_limit_bytes=VMEM_LIMIT),
        name="merge_router",
    )(x2, oa, ob, *consts)


def _dest_kernel(pstart_ref, meta_ref, dest_ref):
    for k in range(2):
        expert = meta_ref[k]
        dest = meta_ref[2 + k]
        for e in range(N_EXPERTS):
            dest = dest + jnp.where(expert == e, pstart_ref[e], 0)
        dest_ref[k] = dest


def _dest_call(pstart, meta):
    _, rows, lanes = meta.shape
    grid_spec = pltpu.PrefetchScalarGridSpec(
        num_scalar_prefetch=1,
        grid=(1,),
        in_specs=[pl.BlockSpec(meta.shape, lambda i, ps: (0, 0, 0))],
        out_specs=pl.BlockSpec((2, rows, lanes), lambda i, ps: (0, 0, 0)),
    )
    return pl.pallas_call(
        _dest_kernel,
        grid_spec=grid_spec,
        out_shape=jax.ShapeDtypeStruct((2, rows, lanes), jnp.int32),
        name="dest",
    )(pstart, meta)


def _row_copy(src_ref, src_row, dst_ref, dst_row, sem):
    return pltpu.make_async_copy(src_ref.at[pl.ds(src_row, 1), :],
                                 dst_ref.at[pl.ds(dst_row, 1), :], sem)


def _dispatch_kernel(d0_ref, d1_ref, tail_ref, h_ref, xs_ref, zero_ref, sem, zsem, *, bt, blk):
    i = pl.program_id(0)

    @pl.when(i == 0)
    def _():
        zero_ref[...] = jnp.zeros_like(zero_ref)

        def fill(e):
            start = pl.multiple_of(tail_ref[e] * blk, blk)
            return pltpu.make_async_copy(zero_ref, xs_ref.at[pl.ds(start, blk), :], zsem)

        for e in range(2 * N_EXPERTS):
            @pl.when(tail_ref[e] >= 0)
            def _():
                fill(e).start()

        for e in range(2 * N_EXPERTS):
            @pl.when(tail_ref[e] >= 0)
            def _():
                fill(e).wait()

    def issue(g, carry):
        base = pl.multiple_of(g * SUBLANES, SUBLANES)
        rows = h_ref.at[pl.ds(base, SUBLANES), :]
        for u in range(SUBLANES):
            t = i * bt + base + u
            _row_copy(rows, u, xs_ref, d0_ref[t], sem).start()
            _row_copy(rows, u, xs_ref, d1_ref[t], sem).start()
        return carry

    lax.fori_loop(0, bt // SUBLANES, issue, 0)

    for _ in range(2):
        pltpu.make_async_copy(h_ref, xs_ref.at[pl.ds(0, bt), :], sem).wait()


def _dispatch_call(dest0, dest1, tail_blk, h2, p_rows):
    n, d = h2.shape
    bt = MOVE_ROWS
    blk = MOE_ROWS
    grid_spec = pltpu.PrefetchScalarGridSpec(
        num_scalar_prefetch=3,
        grid=(n // bt,),
        in_specs=[pl.BlockSpec((bt, d), lambda i, d0, d1, tb: (i, 0))],
        out_specs=pl.BlockSpec(memory_space=pl.ANY),
        scratch_shapes=[pltpu.VMEM((blk, d), h2.dtype), pltpu.SemaphoreType.DMA(()),
                        pltpu.SemaphoreType.DMA(())],
    )
    return pl.pallas_call(
        functools.partial(_dispatch_kernel, bt=bt, blk=blk),
        grid_spec=grid_spec,
        out_shape=jax.ShapeDtypeStruct((p_rows, d), h2.dtype),
        compiler_params=pltpu.CompilerParams(dimension_semantics=("arbitrary",),
                                             vmem_limit_bytes=VMEM_LIMIT),
        name="dispatch",
    )(dest0, dest1, tail_blk, h2)


def _expert_kernel(blk_e_ref, used_ref, x_ref, w1_ref, w3_ref, w2_ref, y_ref, w1b, w3b, w2b):
    i = pl.program_id(0)
    live = i < used_ref[0]
    fresh = (i == 0) | (blk_e_ref[i] != blk_e_ref[jnp.maximum(i - 1, 0)])

    @pl.when(live & fresh)
    def _():
        w1b[...] = w1_ref[...].astype(BF16)
        w3b[...] = w3_ref[...].astype(BF16)
        w2b[...] = w2_ref[...].astype(BF16)

    @pl.when(live)
    def _():
        xb = x_ref[...].astype(BF16)
        a = jnp.dot(xb, w1b[...], preferred_element_type=F32)
        g = jnp.dot(xb, w3b[...], preferred_element_type=F32)
        mid = (a / (1.0 + jnp.exp(-a))) * g
        y_ref[...] = jnp.dot(mid.astype(BF16), w2b[...], preferred_element_type=F32)

    @pl.when(i >= used_ref[0])
    def _():
        y_ref[...] = jnp.zeros_like(y_ref)


def _expert_call(blk_e, n_used, xs, w1, w3, w2):
    p_rows, d = xs.shape
    blk = MOE_ROWS
    x_idx = lambda i, be, nu: (jnp.minimum(i, nu[0] - 1), 0)
    grid_spec = pltpu.PrefetchScalarGridSpec(
        num_scalar_prefetch=2,
        grid=(p_rows // blk,),
        in_specs=[pl.BlockSpec((blk, d), x_idx),
                  pl.BlockSpec((None, d, D_EXPERT), lambda i, be, nu: (be[i], 0, 0)),
                  pl.BlockSpec((None, d, D_EXPERT), lambda i, be, nu: (be[i], 0, 0)),
                  pl.BlockSpec((None, D_EXPERT, d), lambda i, be, nu: (be[i], 0, 0))],
        out_specs=pl.BlockSpec((blk, d), lambda i, be, nu: (i, 0)),
        scratch_shapes=[pltpu.VMEM((d, D_EXPERT), BF16), pltpu.VMEM((d, D_EXPERT), BF16),
                        pltpu.VMEM((D_EXPERT, d), BF16)],
    )
    return pl.pallas_call(
        _expert_kernel,
        grid_spec=grid_spec,
        out_shape=jax.ShapeDtypeStruct((p_rows, d), F32),
        compiler_params=pltpu.CompilerParams(dimension_semantics=("arbitrary",),
                                             vmem_limit_bytes=VMEM_LIMIT),
        name="experts",
    )(blk_e, n_used, xs, w1, w3, w2)


def _combine_kernel(d0_ref, d1_ref, x1_ref, mf_ref, g_ref, yb_ref, o_ref, buf0, buf1, sem, *, bt, steps):
    i = pl.program_id(0)
    slot = i % 2

    def gather(step, s):
        def issue(g, carry):
            base = pl.multiple_of(g * SUBLANES, SUBLANES)
            rows0 = buf0.at[s, pl.ds(base, SUBLANES), :]
            rows1 = buf1.at[s, pl.ds(base, SUBLANES), :]
            for u in range(SUBLANES):
                t = step * bt + base + u
                _row_copy(yb_ref, d0_ref[t], rows0, u, sem.at[s]).start()
                _row_copy(yb_ref, d1_ref[t], rows1, u, sem.at[s]).start()
            return carry

        lax.fori_loop(0, bt // SUBLANES, issue, 0)

    @pl.when(i == 0)
    def _():
        gather(0, 0)

    @pl.when(i + 1 < steps)
    def _():
        gather(i + 1, 1 - slot)

    for buf in (buf0, buf1):
        pltpu.make_async_copy(yb_ref.at[pl.ds(0, bt), :], buf.at[slot], sem.at[slot]).wait()

    mf = mf_ref[...]
    x2 = x1_ref[...] + mf[:, 0:1] * buf0[slot] + mf[:, 1:2] * buf1[slot]
    o_ref[...] = _rms(x2, g_ref[...])


def _combine_call(dest0, dest1, x1, mf, g_final, yb):
    n, d = x1.shape
    bt = MOVE_ROWS
    grid_spec = pltpu.PrefetchScalarGridSpec(
        num_scalar_prefetch=2,
        grid=(n // bt,),
        in_specs=[pl.BlockSpec((bt, d), lambda i, d0, d1: (i, 0)),
                  pl.BlockSpec((bt, LANE), lambda i, d0, d1: (i, 0)),
                  pl.BlockSpec((1, d), lambda i, d0, d1: (0, 0)),
                  pl.BlockSpec(memory_space=pl.ANY)],
        out_specs=pl.BlockSpec((bt, d), lambda i, d0, d1: (i, 0)),
        scratch_shapes=[pltpu.VMEM((2, bt, d), F32), pltpu.VMEM((2, bt, d), F32),
                        pltpu.SemaphoreType.DMA((2,))],
    )
    return pl.pallas_call(
        functools.partial(_combine_kernel, bt=bt, steps=n // bt),
        grid_spec=grid_spec,
        out_shape=jax.ShapeDtypeStruct((n, d), F32),
        compiler_params=pltpu.CompilerParams(dimension_semantics=("arbitrary",),
                                             vmem_limit_bytes=VMEM_LIMIT),
        name="combine",
    )(dest0, dest1, x1, mf, g_final, yb)


def _pad_heads(w, heads, width, padded, scale=1.0):
    d = w.shape[0]
    w = (w * scale).reshape(d, heads, width)
    return jnp.pad(w, ((0, 0), (0, 0), (0, padded - width))).reshape(d, heads * padded)


def _build_proj_params(w_in, b_fgate):
    fw, dw = FOX_HEADS * FOX_HEAD_DIM, DIFF_HEADS * DIFF_V_DIM
    o = 0
    fq, fk, fv = (w_in[:, o + k * fw:o + (k + 1) * fw] for k in range(3))
    o += 3 * fw
    ff = w_in[:, o:o + FOX_HEADS]
    o += FOX_HEADS
    dq, dk, dv = (w_in[:, o + k * dw:o + (k + 1) * dw] for k in range(3))
    d = w_in.shape[0]
    dq = dq.reshape(d, DIFF_HEADS, 2, DIFF_HEAD_DIM)
    dk = dk.reshape(d, DIFF_HEADS, 2, DIFF_HEAD_DIM)
    fscale = FOX_HEAD_DIM ** -0.5 * LOG2E
    dscale = DIFF_HEAD_DIM ** -0.5 * LOG2E
    half = lambda t, k: t[:, :, k, :].reshape(d, DIFF_HEADS * DIFF_HEAD_DIM)
    w_t = jnp.concatenate([
        _pad_heads(fq, FOX_HEADS, FOX_HEAD_DIM, LANE, fscale),
        _pad_heads(fv, FOX_HEADS, FOX_HEAD_DIM, FOX_V_ROWS),
        _pad_heads(half(dq, 0), DIFF_HEADS, DIFF_HEAD_DIM, LANE, dscale),
        _pad_heads(half(dq, 1), DIFF_HEADS, DIFF_HEAD_DIM, LANE, dscale),
        _pad_heads(dv, DIFF_HEADS, DIFF_V_DIM, DIFF_V_ROWS),
    ], axis=1).T.astype(BF16)
    w_row = jnp.concatenate([
        _pad_heads(fk, FOX_HEADS, FOX_HEAD_DIM, LANE),
        jnp.pad(ff, ((0, 0), (0, LANE - FOX_HEADS))),
        _pad_heads(half(dk, 0), DIFF_HEADS, DIFF_HEAD_DIM, LANE),
        _pad_heads(half(dk, 1), DIFF_HEADS, DIFF_HEAD_DIM, LANE),
    ], axis=1).astype(BF16)

    slopes = [2.0 ** (-8.0 / DIFF_HEADS * (hh + 1)) for hh in range(DIFF_HEADS)]
    b = jnp.zeros((_T_ROWS,), F32)
    for hh in range(FOX_HEADS):
        b = b.at[_TQF + hh * LANE + FOX_HEAD_DIM:_TQF + hh * LANE + FOX_HEAD_DIM + 3].set(-1.0)
        b = b.at[_TVF + hh * FOX_V_ROWS + FOX_HEAD_DIM].set(1.0)
    for hh in range(DIFF_HEADS):
        for base in (_TQ1, _TQ2):
            b = b.at[base + hh * LANE + DIFF_HEAD_DIM:base + hh * LANE + DIFF_HEAD_DIM + 3].set(slopes[hh])
        b = b.at[_TVD + hh * DIFF_V_ROWS + DIFF_V_DIM].set(1.0)
    bf_pad = jnp.pad(b_fgate, (0, LANE - FOX_HEADS))[None, :]
    return w_t, b[:, None], w_row, bf_pad, slopes


def kernel(x, g_mix, w_in, b_fgate, b_gate, lam_q1, lam_k1, lam_q2, lam_k2, g_subln, w_pa, w_pb, w_o,
           g_moe, w_group, b_group, w_expert, b_expert, w1, w3, w2, g_final):
    bsz, seq, d = x.shape
    n = bsz * seq
    tile = ATT_TILE
    nk = seq // tile
    x2 = x.reshape(n, d)

    w_t, b_t, w_row, bf_pad, slopes = _build_proj_params(w_in[0], b_fgate[0])
    qft, vft, q1t, q2t, vdt, kf, k1, k2, fcol, cblk = _proj_call(x2, g_mix, w_t, b_t, w_row, bf_pad, bsz, seq)
    shp = lambda a: a.reshape(bsz, seq, a.shape[1])
    c_flat = cblk[:, 0, :FOX_HEADS].reshape(bsz, nk, FOX_HEADS).transpose(0, 2, 1).reshape(-1)
    f_rows = (fcol[:, :FOX_HEADS].reshape(bsz, nk, tile, FOX_HEADS).transpose(0, 3, 1, 2)
              .reshape(bsz, FOX_HEADS, nk, 1, tile))

    oa = _fox_call(c_flat, qft, shp(kf), vft, f_rows, tile)
    lam = (jnp.exp(jnp.sum(lam_q1[0] * lam_k1[0])) - jnp.exp(jnp.sum(lam_q2[0] * lam_k2[0])) + LAM_INIT)
    scal = jnp.concatenate([jnp.asarray(slopes, F32), lam[None].astype(F32)])
    ob = _diff_call(scal, q1t, q2t, shp(k1), shp(k2), vdt, g_subln[0][:, None], tile)

    o = 3 * FOX_HEADS * FOX_HEAD_DIM + FOX_HEADS + 3 * DIFF_HEADS * DIFF_V_DIM
    w_g = w_in[0][:, o:].astype(BF16)
    w_pa_pad = jnp.pad(w_pa[0].reshape(FOX_HEADS, FOX_HEAD_DIM, d),
                       ((0, 0), (0, LANE - FOX_HEAD_DIM), (0, 0))).reshape(FOX_HEADS * LANE, d).astype(BF16)
    w_r = jnp.zeros((d, 2 * LANE), F32)
    w_r = w_r.at[:, :N_EXPERTS].set(w_expert[0]).at[:, LANE:LANE + N_GROUPS].set(w_group[0])
    wr_hi = w_r.astype(BF16)
    wr_lo = (w_r - wr_hi.astype(F32)).astype(BF16)
    b_r = jnp.zeros((1, 2 * LANE), F32)
    b_r = b_r.at[0, :N_EXPERTS].set(b_expert[0].reshape(-1)).at[0, LANE:LANE + N_GROUPS].set(b_group[0])
    x1, h2, mi, mf, cnt = _merge_call(
        x2, oa.reshape(n, -1), ob.reshape(n, -1), g_mix, w_g, b_gate, w_pa_pad,
        w_pb[0].astype(BF16), w_o[0].astype(BF16), g_moe, wr_hi, wr_lo, b_r)

    blk = MOE_ROWS
    counts = cnt[0, :N_EXPERTS].astype(jnp.int32)
    padded = (counts + blk - 1) // blk * blk
    pend = jnp.cumsum(padded)
    pstart = pend - padded
    meta = mi.transpose(1, 0, 2).reshape(SUBLANES, n // LANE, LANE)
    dest = _dest_call(pstart.astype(jnp.int32), meta).reshape(2, n)
    dest0, dest1 = dest[0], dest[1]
    p_rows = n * 2 + N_EXPERTS * blk
    n_blk = p_rows // blk
    blk_pos = jnp.arange(n_blk, dtype=jnp.int32) * blk
    blk_e = jnp.minimum(jnp.sum(pend[None, :] <= blk_pos[:, None], axis=1), N_EXPERTS - 1).astype(jnp.int32)
    n_used = (pend[-1:] // blk).astype(jnp.int32)

    spare = n_used[0] + jnp.arange(N_EXPERTS, dtype=jnp.int32)
    tail_blk = jnp.concatenate([jnp.where(padded > 0, pend // blk - 1, -1),
                                jnp.where(spare < n_blk, spare, -1)]).astype(jnp.int32)
    xs = _dispatch_call(dest0, dest1, tail_blk, h2, p_rows)
    yb = _expert_call(blk_e, n_used, xs, w1[0], w3[0], w2[0])
    out = _combine_call(dest0, dest1, x1, mf, g_final[None, :], yb)
    return out.reshape(bsz, seq, d)
```

```python
import functools

import jax
import jax.numpy as jnp
from jax import lax
from jax.experimental import pallas as pl
from jax.experimental.pallas import tpu as pltpu

F32 = jnp.float32
BF16 = jnp.bfloat16

D_MODEL = 1024
FOX_HEADS = 8
FOX_HEAD_DIM = 64
DIFF_HEADS = 4
DIFF_HEAD_DIM = 64
DIFF_V_DIM = 128
CHUNK = 64
N_GROUPS = 4
EXPERTS_PER_GROUP = 8
N_EXPERTS = 32
D_EXPERT = 512
EPS = 1e-6
LAM_INIT = 0.8 - 0.6 * 1.0

LANE = 128
SUBLANES = 8
NEG = -1e30
LOG2E = 1.4426950408889634
ATT_TILE = 512
MERGE_ROWS = 512
MOE_ROWS = 512
MOVE_ROWS = 1024
VMEM_LIMIT = 56 * 1024 * 1024

FOX_V_ROWS = 80
DIFF_V_ROWS = 144

_TQF, _TVF, _TQ1 = 0, 1024, 1024 + FOX_HEADS * FOX_V_ROWS
_TQ2, _TVD = _TQ1 + 512, _TQ1 + 1024
_T_ROWS = _TVD + DIFF_HEADS * DIFF_V_ROWS
_KF, _FF, _K1, _K2, _ROW_COLS = 0, 1024, 1152, 1664, 2176


def _rms(x, g):
    return x * lax.rsqrt(jnp.mean(x * x, axis=-1, keepdims=True) + EPS) * g


def _split3(r):
    r0 = r.astype(BF16).astype(F32)
    r1 = (r - r0).astype(BF16).astype(F32)
    r2 = (r - r0 - r1).astype(BF16).astype(F32)
    return r0, r1, r2


_NT = (((1,), (1,)), ((), ()))


def _proj_kernel(x_ref, g_ref, wt_ref, bt_ref, w_ref, bf_ref,
                 qft_ref, vft_ref, q1t_ref, q2t_ref, vdt_ref, kf_ref, k1_ref, k2_ref,
                 fcol_ref, cblk_ref, carry_ref, *, bm, steps_per_seq):
    i = pl.program_id(0)
    h = _rms(x_ref[...], g_ref[...]).astype(BF16)

    zt = lax.dot_general(wt_ref[...], h, _NT, preferred_element_type=F32) + bt_ref[...]

    def heads(lo, n_heads, rows):
        return zt[lo:lo + n_heads * rows].reshape(n_heads, rows, bm).astype(BF16)

    qft_ref[...] = heads(_TQF, FOX_HEADS, LANE)
    vft_ref[...] = heads(_TVF, FOX_HEADS, FOX_V_ROWS)
    q1t_ref[...] = heads(_TQ1, DIFF_HEADS, LANE)
    q2t_ref[...] = heads(_TQ2, DIFF_HEADS, LANE)
    vdt_ref[...] = heads(_TVD, DIFF_HEADS, DIFF_V_ROWS)

    z = jnp.dot(h, w_ref[...], preferred_element_type=F32)

    zf = z[:, _FF:_K1] + bf_ref[...]
    logf = jnp.minimum(zf, 0.0) - jnp.log1p(jnp.exp(-jnp.abs(zf)))

    @pl.when(i % steps_per_seq == 0)
    def _():
        carry_ref[...] = jnp.zeros_like(carry_ref)

    c = carry_ref[...]
    row = lax.broadcasted_iota(jnp.int32, (bm, bm), 0)
    col = lax.broadcasted_iota(jnp.int32, (bm, bm), 1)
    tri = (col <= row).astype(BF16)
    l0, l1, l2 = _split3(logf)
    rel = (jnp.dot(tri, l0.astype(BF16), preferred_element_type=F32)
           + jnp.dot(tri, l1.astype(BF16), preferred_element_type=F32)
           + jnp.dot(tri, l2.astype(BF16), preferred_element_type=F32))
    fcum = rel + c
    carry_ref[...] = fcum[bm - 1:bm, :]
    fcol_ref[...] = fcum
    cblk_ref[...] = jnp.broadcast_to(c, cblk_ref.shape)

    lane = lax.broadcasted_iota(jnp.int32, (bm, LANE), 1)
    rel2 = rel * LOG2E
    for hh in range(FOX_HEADS):
        r0, r1, r2 = _split3(rel2[:, hh:hh + 1])
        aug = jnp.where(lane == FOX_HEAD_DIM, r0,
                        jnp.where(lane == FOX_HEAD_DIM + 1, r1,
                                  jnp.where(lane == FOX_HEAD_DIM + 2, r2, 0.0)))
        lo = _KF + hh * LANE
        kf_ref[:, hh * LANE:(hh + 1) * LANE] = (z[:, lo:lo + LANE] + aug).astype(BF16)

    width = DIFF_HEADS * LANE
    jrel = lax.broadcasted_iota(jnp.int32, (bm, width), 0)
    lane4 = lax.broadcasted_iota(jnp.int32, (bm, width), 1) & (LANE - 1)
    j0, j1, j2 = _split3(jrel.astype(F32) * LOG2E)
    augk = jnp.where(lane4 == DIFF_HEAD_DIM, j0,
                     jnp.where(lane4 == DIFF_HEAD_DIM + 1, j1,
                               jnp.where(lane4 == DIFF_HEAD_DIM + 2, j2, 0.0)))
    k1_ref[...] = (z[:, _K1:_K2] + augk).astype(BF16)
    k2_ref[...] = (z[:, _K2:_ROW_COLS] + augk).astype(BF16)


def _proj_call(x2, g_mix, w_t, b_t, w_row, bf_pad, bsz, seq):
    n = x2.shape[0]
    bm = ATT_TILE
    steps = n // bm
    spp = seq // bm
    row = lambda w: pl.BlockSpec((bm, w), lambda i: (i, 0))
    once = lambda a: pl.BlockSpec(a.shape, lambda i: (0,) * a.ndim, pipeline_mode=pl.Buffered(1))
    tshape = lambda heads, rows: jax.ShapeDtypeStruct((bsz, heads, spp, rows, bm), BF16)
    tspec = lambda heads, rows: pl.BlockSpec((None, heads, None, rows, bm),
                                             lambda i: (i // spp, 0, i % spp, 0, 0))
    out_shape = (
        tshape(FOX_HEADS, LANE), tshape(FOX_HEADS, FOX_V_ROWS),
        tshape(DIFF_HEADS, LANE), tshape(DIFF_HEADS, LANE), tshape(DIFF_HEADS, DIFF_V_ROWS),
        jax.ShapeDtypeStruct((n, FOX_HEADS * LANE), BF16),
        jax.ShapeDtypeStruct((n, DIFF_HEADS * LANE), BF16),
        jax.ShapeDtypeStruct((n, DIFF_HEADS * LANE), BF16),
        jax.ShapeDtypeStruct((n, LANE), F32),
        jax.ShapeDtypeStruct((steps, 8, LANE), F32),
    )
    out_specs = (
        tspec(FOX_HEADS, LANE), tspec(FOX_HEADS, FOX_V_ROWS),
        tspec(DIFF_HEADS, LANE), tspec(DIFF_HEADS, LANE), tspec(DIFF_HEADS, DIFF_V_ROWS),
        row(FOX_HEADS * LANE), row(DIFF_HEADS * LANE), row(DIFF_HEADS * LANE), row(LANE),
        pl.BlockSpec((None, 8, LANE), lambda i: (i, 0, 0)),
    )
    return pl.pallas_call(
        functools.partial(_proj_kernel, bm=bm, steps_per_seq=spp),
        grid=(steps,),
        in_specs=[row(D_MODEL), once(g_mix), once(w_t), once(b_t), once(w_row), once(bf_pad)],
        out_specs=out_specs,
        out_shape=out_shape,
        scratch_shapes=[pltpu.VMEM((1, LANE), F32)],
        compiler_params=pltpu.CompilerParams(dimension_semantics=("arbitrary",),
                                             vmem_limit_bytes=VMEM_LIMIT),
        name="proj",
    )(x2, g_mix, w_t, b_t, w_row, bf_pad)


def _softmax_step(s_ref, m_ref, acc_ref, vt, d, tile, fix=None):
    half = tile // 2
    for c0 in (0, half):
        cols = slice(c0, c0 + half)
        s = s_ref[:, cols]
        if fix is not None:
            s = fix(s, c0)
        dc = d[:, cols]
        m_old = m_ref[:, cols]
        m_new = jnp.maximum(m_old, jnp.max(s, axis=0, keepdims=True) + dc)
        p = jnp.exp2(s - (m_new - dc))
        acc_ref[:, cols] = (jnp.exp2(m_old - m_new) * acc_ref[:, cols]
                            + jnp.dot(vt, p.astype(BF16), preferred_element_type=F32))
        m_ref[:, cols] = m_new


def _tile_loop(i, fill, step, last):
    fill(0, 0)

    def run(first, trips, width):
        def body(q, carry):
            j = first + width * q
            for u in range(width):
                fill((u + 1) % 2, j + u + 1)
                step(u % 2, j + u)
            return carry

        lax.fori_loop(0, trips, body, 0)

    run(0, i // 8, 8)
    run((i // 8) * 8, (i % 8) // 4, 4)
    run((i // 4) * 4, (i % 4) // 2, 2)

    @pl.when(i % 2 == 1)
    def _():
        fill(1, i)
        step(0, i - 1)
        last(1)

    @pl.when(i % 2 == 0)
    def _():
        last(0)


def _fox_kernel(c_ref, qt_ref, k_ref, vt_ref, f_ref, o_ref, sa_ref, sb_ref, m_ref, acc_ref, *, tile, nk):
    b, hh = pl.program_id(0), pl.program_id(1)
    base = (b * FOX_HEADS + hh) * nk
    bufs = (sa_ref, sb_ref)

    def q_block(i, carry):
        fi = f_ref[i] * LOG2E
        m_ref[...] = jnp.full_like(m_ref, NEG)
        acc_ref[...] = jnp.zeros_like(acc_ref)

        def fill(slot, j):
            off = pl.multiple_of(j * tile, tile)
            bufs[slot][...] = jnp.dot(k_ref[pl.ds(off, tile), :], qt_ref[i], preferred_element_type=F32)

        def step(slot, j):
            _softmax_step(bufs[slot], m_ref, acc_ref, vt_ref[j], fi - c_ref[base + j] * LOG2E, tile)

        def last(slot):
            key = lax.broadcasted_iota(jnp.int32, (tile, tile // 2), 0)
            qry = lax.broadcasted_iota(jnp.int32, (tile, tile // 2), 1)
            causal = lambda s, c0: jnp.where(key <= qry + c0, s, NEG)
            _softmax_step(bufs[slot], m_ref, acc_ref, vt_ref[i], fi - c_ref[base + i] * LOG2E, tile, causal)

        _tile_loop(i, fill, step, last)

        acc = acc_ref[...]
        o = acc[:FOX_HEAD_DIM] / acc[FOX_HEAD_DIM:FOX_HEAD_DIM + 1]
        o = jnp.concatenate([o, jnp.zeros((LANE - FOX_HEAD_DIM, tile), F32)], axis=0)
        o_ref[pl.ds(pl.multiple_of(i * tile, tile), tile), :] = o.T.astype(BF16)
        return carry

    lax.fori_loop(0, nk, q_block, 0)


def _fox_call(c_flat, qft, kf, vft, f_rows, tile):
    bsz, _, nq, _, _ = qft.shape
    seq = nq * tile
    grid_spec = pltpu.PrefetchScalarGridSpec(
        num_scalar_prefetch=1,
        grid=(bsz, FOX_HEADS),
        in_specs=[
            pl.BlockSpec((None, None, nq, LANE, tile), lambda b, h, c: (b, h, 0, 0, 0)),
            pl.BlockSpec((None, seq, LANE), lambda b, h, c: (b, 0, h)),
            pl.BlockSpec((None, None, nq, FOX_V_ROWS, tile), lambda b, h, c: (b, h, 0, 0, 0)),
            pl.BlockSpec((None, None, nq, 1, tile), lambda b, h, c: (b, h, 0, 0, 0)),
        ],
        out_specs=pl.BlockSpec((None, seq, LANE), lambda b, h, c: (b, 0, h)),
        scratch_shapes=[pltpu.VMEM((tile, tile), F32), pltpu.VMEM((tile, tile), F32),
                        pltpu.VMEM((1, tile), F32), pltpu.VMEM((FOX_V_ROWS, tile), F32)],
    )
    return pl.pallas_call(
        functools.partial(_fox_kernel, tile=tile, nk=nq),
        grid_spec=grid_spec,
        out_shape=jax.ShapeDtypeStruct((bsz, seq, FOX_HEADS * LANE), BF16),
        compiler_params=pltpu.CompilerParams(
            dimension_semantics=("parallel", "parallel"),
            vmem_limit_bytes=VMEM_LIMIT),
        name="fox_attn",
    )(c_flat, qft, kf, vft, f_rows)


def _diff_kernel(sc_ref, q1t_ref, q2t_ref, k1_ref, k2_ref, vt_ref, g_ref, o_ref,
                 s1a_ref, s1b_ref, s2a_ref, s2b_ref, m1_ref, a1_ref, m2_ref, a2_ref, bias_ref, *, tile, nk):
    hh = pl.program_id(1)
    slope = sc_ref[hh] * LOG2E
    lam = sc_ref[DIFF_HEADS]

    key = lax.broadcasted_iota(jnp.int32, (tile, tile), 0)
    qry = lax.broadcasted_iota(jnp.int32, (tile, tile), 1)
    bias = -slope * (key + jnp.abs(qry - key)).astype(F32)
    bias_ref[...] = jnp.where((key // CHUNK) <= (qry // CHUNK), bias, NEG)

    bufs1 = (s1a_ref, s1b_ref)
    bufs2 = (s2a_ref, s2b_ref)

    def q_block(i, carry):
        for m_ref, a_ref in ((m1_ref, a1_ref), (m2_ref, a2_ref)):
            m_ref[...] = jnp.full_like(m_ref, NEG)
            a_ref[...] = jnp.zeros_like(a_ref)
        qrel = lax.broadcasted_iota(jnp.int32, (1, tile), 1)

        def fill(slot, j):
            off = pl.multiple_of(j * tile, tile)
            bufs1[slot][...] = jnp.dot(k1_ref[pl.ds(off, tile), :], q1t_ref[i], preferred_element_type=F32)
            bufs2[slot][...] = jnp.dot(k2_ref[pl.ds(off, tile), :], q2t_ref[i], preferred_element_type=F32)

        def step(slot, j):
            d = slope * ((j - i) * tile - qrel).astype(F32)
            vt = vt_ref[j]
            _softmax_step(bufs1[slot], m1_ref, a1_ref, vt, d, tile)
            _softmax_step(bufs2[slot], m2_ref, a2_ref, vt, d, tile)

        def last(slot):
            half = tile // 2
            chunk_bias = lambda s, c0: s + bias_ref[:, c0:c0 + half]
            zero = jnp.zeros((1, tile), F32)
            vt = vt_ref[i]
            _softmax_step(bufs1[slot], m1_ref, a1_ref, vt, zero, tile, chunk_bias)
            _softmax_step(bufs2[slot], m2_ref, a2_ref, vt, zero, tile, chunk_bias)

        _tile_loop(i, fill, step, last)

        a1 = a1_ref[...]
        a2 = a2_ref[...]
        o1 = a1[:DIFF_V_DIM] / a1[DIFF_V_DIM:DIFF_V_DIM + 1]
        o2 = a2[:DIFF_V_DIM] / a2[DIFF_V_DIM:DIFF_V_DIM + 1]
        ob = o1 - lam * o2
        ob = ob * lax.rsqrt(jnp.mean(ob * ob, axis=0, keepdims=True) + EPS) * g_ref[...] * (1.0 - LAM_INIT)
        o_ref[pl.ds(pl.multiple_of(i * tile, tile), tile), :] = ob.T.astype(BF16)
        return carry

    lax.fori_loop(0, nk, q_block, 0)


def _diff_call(scal, q1t, q2t, k1, k2, vdt, g_col, tile):
    bsz, _, nq, _, _ = q1t.shape
    seq = nq * tile
    qspec = pl.BlockSpec((None, None, nq, LANE, tile), lambda b, h, c: (b, h, 0, 0, 0))
    kspec = pl.BlockSpec((None, seq, LANE), lambda b, h, c: (b, 0, h))
    score = pltpu.VMEM((tile, tile), F32)
    grid_spec = pltpu.PrefetchScalarGridSpec(
        num_scalar_prefetch=1,
        grid=(bsz, DIFF_HEADS),
        in_specs=[qspec, qspec, kspec, kspec,
                  pl.BlockSpec((None, None, nq, DIFF_V_ROWS, tile), lambda b, h, c: (b, h, 0, 0, 0)),
                  pl.BlockSpec((DIFF_V_DIM, 1), lambda b, h, c: (0, 0))],
        out_specs=pl.BlockSpec((None, seq, LANE), lambda b, h, c: (b, 0, h)),
        scratch_shapes=[score, score, score, score,
                        pltpu.VMEM((1, tile), F32), pltpu.VMEM((DIFF_V_ROWS, tile), F32),
                        pltpu.VMEM((1, tile), F32), pltpu.VMEM((DIFF_V_ROWS, tile), F32),
                        score],
    )
    return pl.pallas_call(
        functools.partial(_diff_kernel, tile=tile, nk=nq),
        grid_spec=grid_spec,
        out_shape=jax.ShapeDtypeStruct((bsz, seq, DIFF_HEADS * LANE), BF16),
        compiler_params=pltpu.CompilerParams(
            dimension_semantics=("parallel", "parallel"),
            vmem_limit_bytes=VMEM_LIMIT),
        name="diff_attn",
    )(scal, q1t, q2t, k1, k2, vdt, g_col)


def _merge_kernel(x_ref, oa_ref, ob_ref, gmix_ref, wg_ref, bg_ref, wpa_ref, wpb_ref, wo_ref,
                  gmoe_ref, wrh_ref, wrl_ref, br_ref,
                  x1_ref, h2_ref, mi_ref, mf_ref, cnt_ref, run_ref, *, bm):
    i = pl.program_id(0)

    @pl.when(i == 0)
    def _():
        run_ref[...] = jnp.zeros_like(run_ref)

    x = x_ref[...]
    h = _rms(x, gmix_ref[...]).astype(BF16)
    gates = jnp.dot(h, wg_ref[...], preferred_element_type=F32) + bg_ref[...]
    gates = 1.0 / (1.0 + jnp.exp(-gates))
    ya = jnp.dot(oa_ref[...], wpa_ref[...], preferred_element_type=F32)
    yb = jnp.dot(ob_ref[...], wpb_ref[...], preferred_element_type=F32)
    y = gates[:, :D_MODEL] * ya + gates[:, D_MODEL:] * yb
    x1 = x + jnp.dot(y.astype(BF16), wo_ref[...], preferred_element_type=F32)
    x1_ref[...] = x1
    h2 = _rms(x1, gmoe_ref[...])
    h2_ref[...] = h2

    h2h = h2.astype(BF16)
    h2l = (h2 - h2h.astype(F32)).astype(BF16)
    r = (jnp.dot(h2h, wrh_ref[...], preferred_element_type=F32)
         + jnp.dot(h2l, wrh_ref[...], preferred_element_type=F32)
         + jnp.dot(h2h, wrl_ref[...], preferred_element_type=F32)) + br_ref[...]
    el_all = r[:, :LANE]
    gl = r[:, LANE:]

    lane_i = lax.broadcasted_iota(jnp.int32, (bm, LANE), 1)
    lane = lane_i.astype(F32)
    big = float(LANE)

    def first_argmax(vals):
        vmax = jnp.max(vals, axis=1, keepdims=True)
        idx = jnp.min(jnp.where(vals == vmax, lane, big), axis=1, keepdims=True)
        return vmax, idx

    glm = jnp.where(lane_i < N_GROUPS, gl, NEG)
    gmax, g_idx = first_argmax(glm)
    g_w = 1.0 / jnp.sum(jnp.exp(glm - gmax), axis=1, keepdims=True)

    in_group = ((lane_i >> 3).astype(F32) == g_idx) & (lane_i < N_EXPERTS)
    elm = jnp.where(in_group, el_all, NEG)
    e1, idx1 = first_argmax(elm)
    elm2 = jnp.where(lane == idx1, NEG, elm)
    e2, idx2 = first_argmax(elm2)
    t = jnp.exp(e2 - e1)
    w1 = g_w / (1.0 + t)
    w2 = g_w * t / (1.0 + t)

    oh1 = (lane == idx1).astype(F32)
    oh2 = (lane == idx2).astype(F32)
    oh = oh1 + oh2
    row = lax.broadcasted_iota(jnp.int32, (bm, bm), 0)
    col = lax.broadcasted_iota(jnp.int32, (bm, bm), 1)
    strict = (col < row).astype(BF16)
    before = jnp.dot(strict, oh.astype(BF16), preferred_element_type=F32) + run_ref[...]
    rank1 = jnp.sum(oh1 * before, axis=1, keepdims=True)
    rank2 = jnp.sum(oh2 * before, axis=1, keepdims=True)
    run_ref[...] = run_ref[...] + jnp.sum(oh, axis=0, keepdims=True)
    cnt_ref[...] = jnp.broadcast_to(run_ref[...], cnt_ref.shape)

    meta = jnp.where(lane_i == 0, idx1, jnp.where(lane_i == 1, idx2,
                     jnp.where(lane_i == 2, rank1, jnp.where(lane_i == 3, rank2, 0.0))))
    mi_ref[...] = meta.T[:SUBLANES].astype(jnp.int32)
    mf_ref[...] = jnp.where(lane_i == 0, w1, jnp.where(lane_i == 1, w2, 0.0))


def _merge_call(x2, oa, ob, g_mix, w_g, b_g, w_pa, w_pb, w_o, g_moe, wr_hi, wr_lo, b_r):
    n = x2.shape[0]
    bm = MERGE_ROWS
    row = lambda w: pl.BlockSpec((bm, w), lambda i: (i, 0))
    full = lambda a: pl.BlockSpec(a.shape, lambda i: (0,) * a.ndim)
    consts = (g_mix, w_g, b_g, w_pa, w_pb, w_o, g_moe, wr_hi, wr_lo, b_r)
    return pl.pallas_call(
        functools.partial(_merge_kernel, bm=bm),
        grid=(n // bm,),
        in_specs=[row(D_MODEL), row(oa.shape[1]), row(ob.shape[1])] + [full(a) for a in consts],
        out_specs=(row(D_MODEL), row(D_MODEL),
                   pl.BlockSpec((None, SUBLANES, bm), lambda i: (i, 0, 0)), row(LANE),
                   pl.BlockSpec((8, LANE), lambda i: (0, 0))),
        out_shape=(jax.ShapeDtypeStruct((n, D_MODEL), F32),
                   jax.ShapeDtypeStruct((n, D_MODEL), F32),
                   jax.ShapeDtypeStruct((n // bm, SUBLANES, bm), jnp.int32),
                   jax.ShapeDtypeStruct((n, LANE), F32),
                   jax.ShapeDtypeStruct((8, LANE), F32)),
        scratch_shapes=[pltpu.VMEM((1, LANE), F32)],
        compiler_params=pltpu.CompilerParams(dimension_semantics=("arbitrary",),
                                             vmem_limit_bytes=VMEM_LIMIT),
        name="merge_router",
    )(x2, oa, ob, *consts)


def _dest_kernel(pstart_ref, meta_ref, dest_ref):
    for k in range(2):
        expert = meta_ref[k]
        dest = meta_ref[2 + k]
        for e in range(N_EXPERTS):
            dest = dest + jnp.where(expert == e, pstart_ref[e], 0)
        dest_ref[k] = dest


def _dest_call(pstart, meta):
    _, rows, lanes = meta.shape
    grid_spec = pltpu.PrefetchScalarGridSpec(
        num_scalar_prefetch=1,
        grid=(1,),
        in_specs=[pl.BlockSpec(meta.shape, lambda i, ps: (0, 0, 0))],
        out_specs=pl.BlockSpec((2, rows, lanes), lambda i, ps: (0, 0, 0)),
    )
    return pl.pallas_call(
        _dest_kernel,
        grid_spec=grid_spec,
        out_shape=jax.ShapeDtypeStruct((2, rows, lanes), jnp.int32),
        name="dest",
    )(pstart, meta)


def _row_copy(src_ref, src_row, dst_ref, dst_row, sem):
    return pltpu.make_async_copy(src_ref.at[pl.ds(src_row, 1), :],
                                 dst_ref.at[pl.ds(dst_row, 1), :], sem)


def _dispatch_kernel(d0_ref, d1_ref, tail_ref, h_ref, xs_ref, zero_ref, sem, zsem, *, bt, blk):
    i = pl.program_id(0)

    @pl.when(i == 0)
    def _():
        zero_ref[...] = jnp.zeros_like(zero_ref)

        def fill(e):
            start = pl.multiple_of(tail_ref[e] * blk, blk)
            return pltpu.make_async_copy(zero_ref, xs_ref.at[pl.ds(start, blk), :], zsem)

        for e in range(2 * N_EXPERTS):
            @pl.when(tail_ref[e] >= 0)
            def _():
                fill(e).start()

        for e in range(2 * N_EXPERTS):
            @pl.when(tail_ref[e] >= 0)
            def _():
                fill(e).wait()

    def issue(g, carry):
        base = pl.multiple_of(g * SUBLANES, SUBLANES)
        rows = h_ref.at[pl.ds(base, SUBLANES), :]
        for u in range(SUBLANES):
            t = i * bt + base + u
            _row_copy(rows, u, xs_ref, d0_ref[t], sem).start()
            _row_copy(rows, u, xs_ref, d1_ref[t], sem).start()
        return carry

    lax.fori_loop(0, bt // SUBLANES, issue, 0)

    for _ in range(2):
        pltpu.make_async_copy(h_ref, xs_ref.at[pl.ds(0, bt), :], sem).wait()


def _dispatch_call(dest0, dest1, tail_blk, h2, p_rows):
    n, d = h2.shape
    bt = MOVE_ROWS
    blk = MOE_ROWS
    grid_spec = pltpu.PrefetchScalarGridSpec(
        num_scalar_prefetch=3,
        grid=(n // bt,),
        in_specs=[pl.BlockSpec((bt, d), lambda i, d0, d1, tb: (i, 0))],
        out_specs=pl.BlockSpec(memory_space=pl.ANY),
        scratch_shapes=[pltpu.VMEM((blk, d), h2.dtype), pltpu.SemaphoreType.DMA(()),
                        pltpu.SemaphoreType.DMA(())],
    )
    return pl.pallas_call(
        functools.partial(_dispatch_kernel, bt=bt, blk=blk),
        grid_spec=grid_spec,
        out_shape=jax.ShapeDtypeStruct((p_rows, d), h2.dtype),
        compiler_params=pltpu.CompilerParams(dimension_semantics=("arbitrary",),
                                             vmem_limit_bytes=VMEM_LIMIT),
        name="dispatch",
    )(dest0, dest1, tail_blk, h2)


def _expert_kernel(blk_e_ref, used_ref, nxt_ref, slot_ref, x_ref, w1_ref, w3_ref, w2_ref, y_ref,
                   w1b, w3b, w2b, w1f, w3f, w2f, sem):
    i = pl.program_id(0)
    live = i < used_ref[0]
    fresh = (i == 0) | (blk_e_ref[i] != blk_e_ref[jnp.maximum(i - 1, 0)])
    slot = slot_ref[i]

    def fetch(e, s):
        return [pltpu.make_async_copy(src.at[e], dst.at[s], sem.at[s])
                for src, dst in ((w1_ref, w1f), (w3_ref, w3f), (w2_ref, w2f))]

    @pl.when(live & (i == 0))
    def _():
        for c in fetch(blk_e_ref[0], slot):
            c.start()

    @pl.when(live & fresh)
    def _():
        for c in fetch(blk_e_ref[i], slot):
            c.wait()
        w1b[...] = w1f[slot].astype(BF16)
        w3b[...] = w3f[slot].astype(BF16)
        w2b[...] = w2f[slot].astype(BF16)

        @pl.when(nxt_ref[i] >= 0)
        def _():
            for c in fetch(nxt_ref[i], 1 - slot):
                c.start()

    @pl.when(live)
    def _():
        xb = x_ref[...].astype(BF16)
        a = jnp.dot(xb, w1b[...], preferred_element_type=F32)
        g = jnp.dot(xb, w3b[...], preferred_element_type=F32)
        mid = (a / (1.0 + jnp.exp(-a))) * g
        y_ref[...] = jnp.dot(mid.astype(BF16), w2b[...], preferred_element_type=F32)

    @pl.when(i >= used_ref[0])
    def _():
        y_ref[...] = jnp.zeros_like(y_ref)


def _expert_call(blk_e, n_used, nxt, wslot, xs, w1, w3, w2):
    p_rows, d = xs.shape
    blk = MOE_ROWS
    x_idx = lambda i, be, nu, nx, sl: (jnp.minimum(i, nu[0] - 1), 0)
    hbm = pl.BlockSpec(memory_space=pl.ANY)
    grid_spec = pltpu.PrefetchScalarGridSpec(
        num_scalar_prefetch=4,
        grid=(p_rows // blk,),
        in_specs=[pl.BlockSpec((blk, d), x_idx), hbm, hbm, hbm],
        out_specs=pl.BlockSpec((blk, d), lambda i, be, nu, nx, sl: (i, 0)),
        scratch_shapes=[pltpu.VMEM((d, D_EXPERT), BF16), pltpu.VMEM((d, D_EXPERT), BF16),
                        pltpu.VMEM((D_EXPERT, d), BF16),
                        pltpu.VMEM((2, d, D_EXPERT), F32), pltpu.VMEM((2, d, D_EXPERT), F32),
                        pltpu.VMEM((2, D_EXPERT, d), F32), pltpu.SemaphoreType.DMA((2,))],
    )
    return pl.pallas_call(
        _expert_kernel,
        grid_spec=grid_spec,
        out_shape=jax.ShapeDtypeStruct((p_rows, d), F32),
        compiler_params=pltpu.CompilerParams(dimension_semantics=("arbitrary",),
                                             vmem_limit_bytes=VMEM_LIMIT),
        name="experts",
    )(blk_e, n_used, nxt, wslot, xs, w1, w3, w2)


def _combine_kernel(d0_ref, d1_ref, x1_ref, mf_ref, g_ref, yb_ref, o_ref, buf0, buf1, sem, *, bt, steps):
    i = pl.program_id(0)
    slot = i % 2

    def gather(step, s):
        def issue(g, carry):
            base = pl.multiple_of(g * SUBLANES, SUBLANES)
            rows0 = buf0.at[s, pl.ds(base, SUBLANES), :]
            rows1 = buf1.at[s, pl.ds(base, SUBLANES), :]
            for u in range(SUBLANES):
                t = step * bt + base + u
                _row_copy(yb_ref, d0_ref[t], rows0, u, sem.at[s]).start()
                _row_copy(yb_ref, d1_ref[t], rows1, u, sem.at[s]).start()
            return carry

        lax.fori_loop(0, bt // SUBLANES, issue, 0)

    @pl.when(i == 0)
    def _():
        gather(0, 0)

    @pl.when(i + 1 < steps)
    def _():
        gather(i + 1, 1 - slot)

    for buf in (buf0, buf1):
        pltpu.make_async_copy(yb_ref.at[pl.ds(0, bt), :], buf.at[slot], sem.at[slot]).wait()

    mf = mf_ref[...]
    x2 = x1_ref[...] + mf[:, 0:1] * buf0[slot] + mf[:, 1:2] * buf1[slot]
    o_ref[...] = _rms(x2, g_ref[...])


def _combine_call(dest0, dest1, x1, mf, g_final, yb):
    n, d = x1.shape
    bt = MOVE_ROWS
    grid_spec = pltpu.PrefetchScalarGridSpec(
        num_scalar_prefetch=2,
        grid=(n // bt,),
        in_specs=[pl.BlockSpec((bt, d), lambda i, d0, d1: (i, 0)),
                  pl.BlockSpec((bt, LANE), lambda i, d0, d1: (i, 0)),
                  pl.BlockSpec((1, d), lambda i, d0, d1: (0, 0)),
                  pl.BlockSpec(memory_space=pl.ANY)],
        out_specs=pl.BlockSpec((bt, d), lambda i, d0, d1: (i, 0)),
        scratch_shapes=[pltpu.VMEM((2, bt, d), F32), pltpu.VMEM((2, bt, d), F32),
                        pltpu.SemaphoreType.DMA((2,))],
    )
    return pl.pallas_call(
        functools.partial(_combine_kernel, bt=bt, steps=n // bt),
        grid_spec=grid_spec,
        out_shape=jax.ShapeDtypeStruct((n, d), F32),
        compiler_params=pltpu.CompilerParams(dimension_semantics=("arbitrary",),
                                             vmem_limit_bytes=VMEM_LIMIT),
        name="combine",
    )(dest0, dest1, x1, mf, g_final, yb)


def _pad_heads(w, heads, width, padded, scale=1.0):
    d = w.shape[0]
    w = (w * scale).reshape(d, heads, width)
    return jnp.pad(w, ((0, 0), (0, 0), (0, padded - width))).reshape(d, heads * padded)


def _build_proj_params(w_in, b_fgate):
    fw, dw = FOX_HEADS * FOX_HEAD_DIM, DIFF_HEADS * DIFF_V_DIM
    o = 0
    fq, fk, fv = (w_in[:, o + k * fw:o + (k + 1) * fw] for k in range(3))
    o += 3 * fw
    ff = w_in[:, o:o + FOX_HEADS]
    o += FOX_HEADS
    dq, dk, dv = (w_in[:, o + k * dw:o + (k + 1) * dw] for k in range(3))
    d = w_in.shape[0]
    dq = dq.reshape(d, DIFF_HEADS, 2, DIFF_HEAD_DIM)
    dk = dk.reshape(d, DIFF_HEADS, 2, DIFF_HEAD_DIM)
    fscale = FOX_HEAD_DIM ** -0.5 * LOG2E
    dscale = DIFF_HEAD_DIM ** -0.5 * LOG2E
    half = lambda t, k: t[:, :, k, :].reshape(d, DIFF_HEADS * DIFF_HEAD_DIM)
    w_t = jnp.concatenate([
        _pad_heads(fq, FOX_HEADS, FOX_HEAD_DIM, LANE, fscale),
        _pad_heads(fv, FOX_HEADS, FOX_HEAD_DIM, FOX_V_ROWS),
        _pad_heads(half(dq, 0), DIFF_HEADS, DIFF_HEAD_DIM, LANE, dscale),
        _pad_heads(half(dq, 1), DIFF_HEADS, DIFF_HEAD_DIM, LANE, dscale),
        _pad_heads(dv, DIFF_HEADS, DIFF_V_DIM, DIFF_V_ROWS),
    ], axis=1).T.astype(BF16)
    w_row = jnp.concatenate([
        _pad_heads(fk, FOX_HEADS, FOX_HEAD_DIM, LANE),
        jnp.pad(ff, ((0, 0), (0, LANE - FOX_HEADS))),
        _pad_heads(half(dk, 0), DIFF_HEADS, DIFF_HEAD_DIM, LANE),
        _pad_heads(half(dk, 1), DIFF_HEADS, DIFF_HEAD_DIM, LANE),
    ], axis=1).astype(BF16)

    slopes = [2.0 ** (-8.0 / DIFF_HEADS * (hh + 1)) for hh in range(DIFF_HEADS)]
    b = jnp.zeros((_T_ROWS,), F32)
    for hh in range(FOX_HEADS):
        b = b.at[_TQF + hh * LANE + FOX_HEAD_DIM:_TQF + hh * LANE + FOX_HEAD_DIM + 3].set(-1.0)
        b = b.at[_TVF + hh * FOX_V_ROWS + FOX_HEAD_DIM].set(1.0)
    for hh in range(DIFF_HEADS):
        for base in (_TQ1, _TQ2):
            b = b.at[base + hh * LANE + DIFF_HEAD_DIM:base + hh * LANE + DIFF_HEAD_DIM + 3].set(slopes[hh])
        b = b.at[_TVD + hh * DIFF_V_ROWS + DIFF_V_DIM].set(1.0)
    bf_pad = jnp.pad(b_fgate, (0, LANE - FOX_HEADS))[None, :]
    return w_t, b[:, None], w_row, bf_pad, slopes


def kernel(x, g_mix, w_in, b_fgate, b_gate, lam_q1, lam_k1, lam_q2, lam_k2, g_subln, w_pa, w_pb, w_o,
           g_moe, w_group, b_group, w_expert, b_expert, w1, w3, w2, g_final):
    bsz, seq, d = x.shape
    n = bsz * seq
    tile = ATT_TILE
    nk = seq // tile
    x2 = x.reshape(n, d)

    w_t, b_t, w_row, bf_pad, slopes = _build_proj_params(w_in[0], b_fgate[0])
    qft, vft, q1t, q2t, vdt, kf, k1, k2, fcol, cblk = _proj_call(x2, g_mix, w_t, b_t, w_row, bf_pad, bsz, seq)
    shp = lambda a: a.reshape(bsz, seq, a.shape[1])
    c_flat = cblk[:, 0, :FOX_HEADS].reshape(bsz, nk, FOX_HEADS).transpose(0, 2, 1).reshape(-1)
    f_rows = (fcol[:, :FOX_HEADS].reshape(bsz, nk, tile, FOX_HEADS).transpose(0, 3, 1, 2)
              .reshape(bsz, FOX_HEADS, nk, 1, tile))

    oa = _fox_call(c_flat, qft, shp(kf), vft, f_rows, tile)
    lam = (jnp.exp(jnp.sum(lam_q1[0] * lam_k1[0])) - jnp.exp(jnp.sum(lam_q2[0] * lam_k2[0])) + LAM_INIT)
    scal = jnp.concatenate([jnp.asarray(slopes, F32), lam[None].astype(F32)])
    ob = _diff_call(scal, q1t, q2t, shp(k1), shp(k2), vdt, g_subln[0][:, None], tile)

    o = 3 * FOX_HEADS * FOX_HEAD_DIM + FOX_HEADS + 3 * DIFF_HEADS * DIFF_V_DIM
    w_g = w_in[0][:, o:].astype(BF16)
    w_pa_pad = jnp.pad(w_pa[0].reshape(FOX_HEADS, FOX_HEAD_DIM, d),
                       ((0, 0), (0, LANE - FOX_HEAD_DIM), (0, 0))).reshape(FOX_HEADS * LANE, d).astype(BF16)
    w_r = jnp.zeros((d, 2 * LANE), F32)
    w_r = w_r.at[:, :N_EXPERTS].set(w_expert[0]).at[:, LANE:LANE + N_GROUPS].set(w_group[0])
    wr_hi = w_r.astype(BF16)
    wr_lo = (w_r - wr_hi.astype(F32)).astype(BF16)
    b_r = jnp.zeros((1, 2 * LANE), F32)
    b_r = b_r.at[0, :N_EXPERTS].set(b_expert[0].reshape(-1)).at[0, LANE:LANE + N_GROUPS].set(b_group[0])
    x1, h2, mi, mf, cnt = _merge_call(
        x2, oa.reshape(n, -1), ob.reshape(n, -1), g_mix, w_g, b_gate, w_pa_pad,
        w_pb[0].astype(BF16), w_o[0].astype(BF16), g_moe, wr_hi, wr_lo, b_r)

    blk = MOE_ROWS
    counts = cnt[0, :N_EXPERTS].astype(jnp.int32)
    padded = (counts + blk - 1) // blk * blk
    pend = jnp.cumsum(padded)
    pstart = pend - padded
    meta = mi.transpose(1, 0, 2).reshape(SUBLANES, n // LANE, LANE)
    dest = _dest_call(pstart.astype(jnp.int32), meta).reshape(2, n)
    dest0, dest1 = dest[0], dest[1]
    p_rows = n * 2 + N_EXPERTS * blk
    n_blk = p_rows // blk
    blk_pos = jnp.arange(n_blk, dtype=jnp.int32) * blk
    blk_e = jnp.minimum(jnp.sum(pend[None, :] <= blk_pos[:, None], axis=1), N_EXPERTS - 1).astype(jnp.int32)
    n_used = (pend[-1:] // blk).astype(jnp.int32)

    spare = n_used[0] + jnp.arange(N_EXPERTS, dtype=jnp.int32)
    tail_blk = jnp.concatenate([jnp.where(padded > 0, pend // blk - 1, -1),
                                jnp.where(spare < n_blk, spare, -1)]).astype(jnp.int32)
    xs = _dispatch_call(dest0, dest1, tail_blk, h2, p_rows)
    eid = jnp.arange(N_EXPERTS, dtype=jnp.int32)
    later = jnp.where((eid[None, :] > eid[:, None]) & (padded[None, :] > 0), eid[None, :], N_EXPERTS)
    nxt_of = jnp.min(later, axis=1)
    nxt = jnp.where(nxt_of < N_EXPERTS, nxt_of, -1)[blk_e].astype(jnp.int32)
    wslot = ((jnp.cumsum((padded > 0).astype(jnp.int32)) - 1)[blk_e] % 2).astype(jnp.int32)
    yb = _expert_call(blk_e, n_used, nxt, wslot, xs, w1[0], w3[0], w2[0])
    out = _combine_call(dest0, dest1, x1, mf, g_final[None, :], yb)
    return out.reshape(bsz, seq, d)
```

```python
import functools

import jax
import jax.numpy as jnp
from jax import lax
from jax.experimental import pallas as pl
from jax.experimental.pallas import tpu as pltpu

F32 = jnp.float32
BF16 = jnp.bfloat16

D_MODEL = 1024
FOX_HEADS = 8
FOX_HEAD_DIM = 64
DIFF_HEADS = 4
DIFF_HEAD_DIM = 64
DIFF_V_DIM = 128
CHUNK = 64
N_GROUPS = 4
EXPERTS_PER_GROUP = 8
N_EXPERTS = 32
D_EXPERT = 512
EPS = 1e-6
LAM_INIT = 0.8 - 0.6 * 1.0

LANE = 128
SUBLANES = 8
NEG = -1e30
LOG2E = 1.4426950408889634
ATT_TILE = 512
MERGE_ROWS = 512
MOE_ROWS = 512
DISPATCH_ROWS = 2048
MOVE_ROWS = 1024
VMEM_LIMIT = 56 * 1024 * 1024

FOX_V_ROWS = 80
DIFF_V_ROWS = 144

_TQF = 0
_TVF = _TQF + FOX_HEADS * LANE
_TQ1 = _TVF + FOX_HEADS * FOX_V_ROWS
_TQ2 = _TQ1 + DIFF_HEADS * LANE
_TVD = _TQ2 + DIFF_HEADS * LANE
_T_ROWS = _TVD + DIFF_HEADS * DIFF_V_ROWS
_KF = 0
_FF = _KF + FOX_HEADS * LANE
_K1 = _FF + LANE
_K2 = _K1 + DIFF_HEADS * LANE
_ROW_COLS = _K2 + DIFF_HEADS * LANE


def _rms(x, g):
    return x * lax.rsqrt(jnp.mean(x * x, axis=-1, keepdims=True) + EPS) * g


def _split3(r):
    r0 = r.astype(BF16).astype(F32)
    r1 = (r - r0).astype(BF16).astype(F32)
    r2 = (r - r0 - r1).astype(BF16).astype(F32)
    return r0, r1, r2


_NT = (((1,), (1,)), ((), ()))


def _proj_kernel(x_ref, g_ref, wt_ref, bt_ref, w_ref, bf_ref,
                 qft_ref, vft_ref, q1t_ref, q2t_ref, vdt_ref, kf_ref, k1_ref, k2_ref,
                 fcol_ref, cblk_ref, carry_ref, *, bm, steps_per_seq):
    i = pl.program_id(0)
    h = _rms(x_ref[...], g_ref[...]).astype(BF16)

    zt = lax.dot_general(wt_ref[...], h, _NT, preferred_element_type=F32) + bt_ref[...]

    def heads(lo, n_heads, rows):
        return zt[lo:lo + n_heads * rows].reshape(n_heads, rows, bm).astype(BF16)

    qft_ref[...] = heads(_TQF, FOX_HEADS, LANE)
    vft_ref[...] = heads(_TVF, FOX_HEADS, FOX_V_ROWS)
    q1t_ref[...] = heads(_TQ1, DIFF_HEADS, LANE)
    q2t_ref[...] = heads(_TQ2, DIFF_HEADS, LANE)
    vdt_ref[...] = heads(_TVD, DIFF_HEADS, DIFF_V_ROWS)

    z = jnp.dot(h, w_ref[...], preferred_element_type=F32)

    zf = z[:, _FF:_K1] + bf_ref[...]
    logf = jnp.minimum(zf, 0.0) - jnp.log1p(jnp.exp(-jnp.abs(zf)))

    @pl.when(i % steps_per_seq == 0)
    def _():
        carry_ref[...] = jnp.zeros_like(carry_ref)

    c = carry_ref[...]
    row = lax.broadcasted_iota(jnp.int32, (bm, bm), 0)
    col = lax.broadcasted_iota(jnp.int32, (bm, bm), 1)
    tri = (col <= row).astype(BF16)
    l0, l1, l2 = _split3(logf)
    rel = (jnp.dot(tri, l0.astype(BF16), preferred_element_type=F32)
           + jnp.dot(tri, l1.astype(BF16), preferred_element_type=F32)
           + jnp.dot(tri, l2.astype(BF16), preferred_element_type=F32))
    fcum = rel + c
    carry_ref[...] = fcum[bm - 1:bm, :]
    fcol_ref[...] = fcum
    cblk_ref[...] = jnp.broadcast_to(c, cblk_ref.shape)

    lane = lax.broadcasted_iota(jnp.int32, (bm, LANE), 1)
    rel2 = rel * LOG2E
    for hh in range(FOX_HEADS):
        r0, r1, r2 = _split3(rel2[:, hh:hh + 1])
        aug = jnp.where(lane == FOX_HEAD_DIM, r0,
                        jnp.where(lane == FOX_HEAD_DIM + 1, r1,
                                  jnp.where(lane == FOX_HEAD_DIM + 2, r2, 0.0)))
        lo = _KF + hh * LANE
        kf_ref[:, hh * LANE:(hh + 1) * LANE] = (z[:, lo:lo + LANE] + aug).astype(BF16)

    width = DIFF_HEADS * LANE
    jrel = lax.broadcasted_iota(jnp.int32, (bm, width), 0)
    lane4 = lax.broadcasted_iota(jnp.int32, (bm, width), 1) & (LANE - 1)
    j0, j1, j2 = _split3(jrel.astype(F32) * LOG2E)
    augk = jnp.where(lane4 == DIFF_HEAD_DIM, j0,
                     jnp.where(lane4 == DIFF_HEAD_DIM + 1, j1,
                               jnp.where(lane4 == DIFF_HEAD_DIM + 2, j2, 0.0)))
    k1_ref[...] = (z[:, _K1:_K2] + augk).astype(BF16)
    k2_ref[...] = (z[:, _K2:_ROW_COLS] + augk).astype(BF16)


def _proj_call(x2, g_mix, w_t, b_t, w_row, bf_pad, bsz, seq):
    n = x2.shape[0]
    bm = ATT_TILE
    steps = n // bm
    spp = seq // bm
    row = lambda w: pl.BlockSpec((bm, w), lambda i: (i, 0))
    once = lambda a: pl.BlockSpec(a.shape, lambda i: (0,) * a.ndim, pipeline_mode=pl.Buffered(1))
    tshape = lambda heads, rows: jax.ShapeDtypeStruct((bsz, heads, spp, rows, bm), BF16)
    tspec = lambda heads, rows: pl.BlockSpec((None, heads, None, rows, bm),
                                             lambda i: (i // spp, 0, i % spp, 0, 0))
    out_shape = (
        tshape(FOX_HEADS, LANE), tshape(FOX_HEADS, FOX_V_ROWS),
        tshape(DIFF_HEADS, LANE), tshape(DIFF_HEADS, LANE), tshape(DIFF_HEADS, DIFF_V_ROWS),
        jax.ShapeDtypeStruct((n, FOX_HEADS * LANE), BF16),
        jax.ShapeDtypeStruct((n, DIFF_HEADS * LANE), BF16),
        jax.ShapeDtypeStruct((n, DIFF_HEADS * LANE), BF16),
        jax.ShapeDtypeStruct((n, LANE), F32),
        jax.ShapeDtypeStruct((steps, SUBLANES, LANE), F32),
    )
    out_specs = (
        tspec(FOX_HEADS, LANE), tspec(FOX_HEADS, FOX_V_ROWS),
        tspec(DIFF_HEADS, LANE), tspec(DIFF_HEADS, LANE), tspec(DIFF_HEADS, DIFF_V_ROWS),
        row(FOX_HEADS * LANE), row(DIFF_HEADS * LANE), row(DIFF_HEADS * LANE), row(LANE),
        pl.BlockSpec((None, SUBLANES, LANE), lambda i: (i, 0, 0)),
    )
    return pl.pallas_call(
        functools.partial(_proj_kernel, bm=bm, steps_per_seq=spp),
        grid=(steps,),
        in_specs=[row(D_MODEL), once(g_mix), once(w_t), once(b_t), once(w_row), once(bf_pad)],
        out_specs=out_specs,
        out_shape=out_shape,
        scratch_shapes=[pltpu.VMEM((1, LANE), F32)],
        compiler_params=pltpu.CompilerParams(dimension_semantics=("arbitrary",),
                                             vmem_limit_bytes=VMEM_LIMIT),
        name="proj",
    )(x2, g_mix, w_t, b_t, w_row, bf_pad)


def _softmax_step(s_ref, m_ref, acc_ref, vt, d, tile, fix=None):
    half = tile // 2
    for c0 in (0, half):
        cols = slice(c0, c0 + half)
        s = s_ref[:, cols]
        if fix is not None:
            s = fix(s, c0)
        dc = d[:, cols]
        m_old = m_ref[:, cols]
        m_new = jnp.maximum(m_old, jnp.max(s, axis=0, keepdims=True) + dc)
        p = jnp.exp2(s - (m_new - dc))
        acc_ref[:, cols] = (jnp.exp2(m_old - m_new) * acc_ref[:, cols]
                            + jnp.dot(vt, p.astype(BF16), preferred_element_type=F32))
        m_ref[:, cols] = m_new


def _tile_loop(i, fill, step, last):
    fill(0, 0)

    def run(first, trips, width):
        def body(q, carry):
            j = first + width * q
            for u in range(width):
                fill((u + 1) % 2, j + u + 1)
                step(u % 2, j + u)
            return carry

        lax.fori_loop(0, trips, body, 0)

    run(0, i // 8, 8)
    run((i // 8) * 8, (i % 8) // 4, 4)
    run((i // 4) * 4, (i % 4) // 2, 2)

    @pl.when(i % 2 == 1)
    def _():
        fill(1, i)
        step(0, i - 1)
        last(1)

    @pl.when(i % 2 == 0)
    def _():
        last(0)


def _fox_kernel(c_ref, qt_ref, k_ref, vt_ref, f_ref, o_ref, sa_ref, sb_ref, m_ref, acc_ref, *, tile, nk):
    b, hh = pl.program_id(0), pl.program_id(1)
    base = (b * FOX_HEADS + hh) * nk
    bufs = (sa_ref, sb_ref)

    def q_block(i, carry):
        fi = f_ref[i] * LOG2E
        m_ref[...] = jnp.full_like(m_ref, NEG)
        acc_ref[...] = jnp.zeros_like(acc_ref)

        def fill(slot, j):
            off = pl.multiple_of(j * tile, tile)
            bufs[slot][...] = jnp.dot(k_ref[pl.ds(off, tile), :], qt_ref[i], preferred_element_type=F32)

        def step(slot, j):
            _softmax_step(bufs[slot], m_ref, acc_ref, vt_ref[j], fi - c_ref[base + j] * LOG2E, tile)

        def last(slot):
            key = lax.broadcasted_iota(jnp.int32, (tile, tile // 2), 0)
            qry = lax.broadcasted_iota(jnp.int32, (tile, tile // 2), 1)
            causal = lambda s, c0: jnp.where(key <= qry + c0, s, NEG)
            _softmax_step(bufs[slot], m_ref, acc_ref, vt_ref[i], fi - c_ref[base + i] * LOG2E, tile, causal)

        _tile_loop(i, fill, step, last)

        acc = acc_ref[...]
        o = acc[:FOX_HEAD_DIM] / acc[FOX_HEAD_DIM:FOX_HEAD_DIM + 1]
        o = jnp.concatenate([o, jnp.zeros((LANE - FOX_HEAD_DIM, tile), F32)], axis=0)
        o_ref[pl.ds(pl.multiple_of(i * tile, tile), tile), :] = o.T.astype(BF16)
        return carry

    lax.fori_loop(0, nk, q_block, 0)


def _fox_call(c_flat, qft, kf, vft, f_rows, tile):
    bsz, _, nq, _, _ = qft.shape
    seq = nq * tile
    grid_spec = pltpu.PrefetchScalarGridSpec(
        num_scalar_prefetch=1,
        grid=(bsz, FOX_HEADS),
        in_specs=[
            pl.BlockSpec((None, None, nq, LANE, tile), lambda b, h, c: (b, h, 0, 0, 0)),
            pl.BlockSpec((None, seq, LANE), lambda b, h, c: (b, 0, h)),
            pl.BlockSpec((None, None, nq, FOX_V_ROWS, tile), lambda b, h, c: (b, h, 0, 0, 0)),
            pl.BlockSpec((None, None, nq, 1, tile), lambda b, h, c: (b, h, 0, 0, 0)),
        ],
        out_specs=pl.BlockSpec((None, seq, LANE), lambda b, h, c: (b, 0, h)),
        scratch_shapes=[pltpu.VMEM((tile, tile), F32), pltpu.VMEM((tile, tile), F32),
                        pltpu.VMEM((1, tile), F32), pltpu.VMEM((FOX_V_ROWS, tile), F32)],
    )
    return pl.pallas_call(
        functools.partial(_fox_kernel, tile=tile, nk=nq),
        grid_spec=grid_spec,
        out_shape=jax.ShapeDtypeStruct((bsz, seq, FOX_HEADS * LANE), BF16),
        compiler_params=pltpu.CompilerParams(
            dimension_semantics=("parallel", "parallel"),
            vmem_limit_bytes=VMEM_LIMIT),
        name="fox_attn",
    )(c_flat, qft, kf, vft, f_rows)


def _diff_kernel(sc_ref, q1t_ref, q2t_ref, k1_ref, k2_ref, vt_ref, g_ref, o_ref,
                 s1a_ref, s1b_ref, s2a_ref, s2b_ref, m1_ref, a1_ref, m2_ref, a2_ref, bias_ref, *, tile, nk):
    hh = pl.program_id(1)
    slope = sc_ref[hh] * LOG2E
    lam = sc_ref[DIFF_HEADS]

    key = lax.broadcasted_iota(jnp.int32, (tile, tile), 0)
    qry = lax.broadcasted_iota(jnp.int32, (tile, tile), 1)
    bias = -slope * (key + jnp.abs(qry - key)).astype(F32)
    bias_ref[...] = jnp.where((key // CHUNK) <= (qry // CHUNK), bias, NEG)

    bufs1 = (s1a_ref, s1b_ref)
    bufs2 = (s2a_ref, s2b_ref)

    def q_block(i, carry):
        for m_ref, a_ref in ((m1_ref, a1_ref), (m2_ref, a2_ref)):
            m_ref[...] = jnp.full_like(m_ref, NEG)
            a_ref[...] = jnp.zeros_like(a_ref)
        qrel = lax.broadcasted_iota(jnp.int32, (1, tile), 1)

        def fill(slot, j):
            off = pl.multiple_of(j * tile, tile)
            bufs1[slot][...] = jnp.dot(k1_ref[pl.ds(off, tile), :], q1t_ref[i], preferred_element_type=F32)
            bufs2[slot][...] = jnp.dot(k2_ref[pl.ds(off, tile), :], q2t_ref[i], preferred_element_type=F32)

        def step(slot, j):
            d = slope * ((j - i) * tile - qrel).astype(F32)
            vt = vt_ref[j]
            _softmax_step(bufs1[slot], m1_ref, a1_ref, vt, d, tile)
            _softmax_step(bufs2[slot], m2_ref, a2_ref, vt, d, tile)

        def last(slot):
            half = tile // 2
            chunk_bias = lambda s, c0: s + bias_ref[:, c0:c0 + half]
            zero = jnp.zeros((1, tile), F32)
            vt = vt_ref[i]
            _softmax_step(bufs1[slot], m1_ref, a1_ref, vt, zero, tile, chunk_bias)
            _softmax_step(bufs2[slot], m2_ref, a2_ref, vt, zero, tile, chunk_bias)

        _tile_loop(i, fill, step, last)

        a1 = a1_ref[...]
        a2 = a2_ref[...]
        o1 = a1[:DIFF_V_DIM] / a1[DIFF_V_DIM:DIFF_V_DIM + 1]
        o2 = a2[:DIFF_V_DIM] / a2[DIFF_V_DIM:DIFF_V_DIM + 1]
        ob = o1 - lam * o2
        ob = ob * lax.rsqrt(jnp.mean(ob * ob, axis=0, keepdims=True) + EPS) * g_ref[...] * (1.0 - LAM_INIT)
        o_ref[pl.ds(pl.multiple_of(i * tile, tile), tile), :] = ob.T.astype(BF16)
        return carry

    lax.fori_loop(0, nk, q_block, 0)


def _diff_call(scal, q1t, q2t, k1, k2, vdt, g_col, tile):
    bsz, _, nq, _, _ = q1t.shape
    seq = nq * tile
    qspec = pl.BlockSpec((None, None, nq, LANE, tile), lambda b, h, c: (b, h, 0, 0, 0))
    kspec = pl.BlockSpec((None, seq, LANE), lambda b, h, c: (b, 0, h))
    score = pltpu.VMEM((tile, tile), F32)
    grid_spec = pltpu.PrefetchScalarGridSpec(
        num_scalar_prefetch=1,
        grid=(bsz, DIFF_HEADS),
        in_specs=[qspec, qspec, kspec, kspec,
                  pl.BlockSpec((None, None, nq, DIFF_V_ROWS, tile), lambda b, h, c: (b, h, 0, 0, 0)),
                  pl.BlockSpec((DIFF_V_DIM, 1), lambda b, h, c: (0, 0))],
        out_specs=pl.BlockSpec((None, seq, LANE), lambda b, h, c: (b, 0, h)),
        scratch_shapes=[score, score, score, score,
                        pltpu.VMEM((1, tile), F32), pltpu.VMEM((DIFF_V_ROWS, tile), F32),
                        pltpu.VMEM((1, tile), F32), pltpu.VMEM((DIFF_V_ROWS, tile), F32),
                        score],
    )
    return pl.pallas_call(
        functools.partial(_diff_kernel, tile=tile, nk=nq),
        grid_spec=grid_spec,
        out_shape=jax.ShapeDtypeStruct((bsz, seq, DIFF_HEADS * LANE), BF16),
        compiler_params=pltpu.CompilerParams(
            dimension_semantics=("parallel", "parallel"),
            vmem_limit_bytes=VMEM_LIMIT),
        name="diff_attn",
    )(scal, q1t, q2t, k1, k2, vdt, g_col)


def _merge_kernel(x_ref, oa_ref, ob_ref, gmix_ref, wg_ref, bg_ref, wpa_ref, wpb_ref, wo_ref,
                  gmoe_ref, wrh_ref, wrl_ref, br_ref,
                  x1_ref, h2_ref, mi_ref, mf_ref, cnt_ref, run_ref, *, bm):
    i = pl.program_id(0)

    @pl.when(i == 0)
    def _():
        run_ref[...] = jnp.zeros_like(run_ref)

    x = x_ref[...]
    h = _rms(x, gmix_ref[...]).astype(BF16)
    gates = jnp.dot(h, wg_ref[...], preferred_element_type=F32) + bg_ref[...]
    gates = 1.0 / (1.0 + jnp.exp(-gates))
    ya = jnp.dot(oa_ref[...], wpa_ref[...], preferred_element_type=F32)
    yb = jnp.dot(ob_ref[...], wpb_ref[...], preferred_element_type=F32)
    y = gates[:, :D_MODEL] * ya + gates[:, D_MODEL:] * yb
    x1 = x + jnp.dot(y.astype(BF16), wo_ref[...], preferred_element_type=F32)
    x1_ref[...] = x1
    h2 = _rms(x1, gmoe_ref[...])
    h2_ref[...] = h2

    h2h = h2.astype(BF16)
    h2l = (h2 - h2h.astype(F32)).astype(BF16)
    r = (jnp.dot(h2h, wrh_ref[...], preferred_element_type=F32)
         + jnp.dot(h2l, wrh_ref[...], preferred_element_type=F32)
         + jnp.dot(h2h, wrl_ref[...], preferred_element_type=F32)) + br_ref[...]
    el_all = r[:, :LANE]
    gl = r[:, LANE:]

    lane_i = lax.broadcasted_iota(jnp.int32, (bm, LANE), 1)
    lane = lane_i.astype(F32)
    big = float(LANE)

    def first_argmax(vals):
        vmax = jnp.max(vals, axis=1, keepdims=True)
        idx = jnp.min(jnp.where(vals == vmax, lane, big), axis=1, keepdims=True)
        return vmax, idx

    glm = jnp.where(lane_i < N_GROUPS, gl, NEG)
    gmax, g_idx = first_argmax(glm)
    g_w = 1.0 / jnp.sum(jnp.exp(glm - gmax), axis=1, keepdims=True)

    in_group = ((lane_i // EXPERTS_PER_GROUP).astype(F32) == g_idx) & (lane_i < N_EXPERTS)
    elm = jnp.where(in_group, el_all, NEG)
    e1, idx1 = first_argmax(elm)
    elm2 = jnp.where(lane == idx1, NEG, elm)
    e2, idx2 = first_argmax(elm2)
    t = jnp.exp(e2 - e1)
    w1 = g_w / (1.0 + t)
    w2 = g_w * t / (1.0 + t)

    oh1 = (lane == idx1).astype(F32)
    oh2 = (lane == idx2).astype(F32)
    oh = oh1 + oh2
    row = lax.broadcasted_iota(jnp.int32, (bm, bm), 0)
    col = lax.broadcasted_iota(jnp.int32, (bm, bm), 1)
    strict = (col < row).astype(BF16)
    before = jnp.dot(strict, oh.astype(BF16), preferred_element_type=F32) + run_ref[...]
    rank1 = jnp.sum(oh1 * before, axis=1, keepdims=True)
    rank2 = jnp.sum(oh2 * before, axis=1, keepdims=True)
    run_ref[...] = run_ref[...] + jnp.sum(oh, axis=0, keepdims=True)
    cnt_ref[...] = jnp.broadcast_to(run_ref[...], cnt_ref.shape)

    meta = jnp.where(lane_i == 0, idx1, jnp.where(lane_i == 1, idx2,
                     jnp.where(lane_i == 2, rank1, jnp.where(lane_i == 3, rank2, 0.0))))
    mi_ref[...] = meta.T[:SUBLANES].astype(jnp.int32)
    mf_ref[...] = jnp.where(lane_i == 0, w1, jnp.where(lane_i == 1, w2, 0.0))


def _merge_call(x2, oa, ob, g_mix, w_g, b_g, w_pa, w_pb, w_o, g_moe, wr_hi, wr_lo, b_r):
    n = x2.shape[0]
    bm = MERGE_ROWS
    row = lambda w: pl.BlockSpec((bm, w), lambda i: (i, 0))
    full = lambda a: pl.BlockSpec(a.shape, lambda i: (0,) * a.ndim)
    consts = (g_mix, w_g, b_g, w_pa, w_pb, w_o, g_moe, wr_hi, wr_lo, b_r)
    return pl.pallas_call(
        functools.partial(_merge_kernel, bm=bm),
        grid=(n // bm,),
        in_specs=[row(D_MODEL), row(oa.shape[1]), row(ob.shape[1])] + [full(a) for a in consts],
        out_specs=(row(D_MODEL), row(D_MODEL),
                   pl.BlockSpec((None, SUBLANES, bm), lambda i: (i, 0, 0)), row(LANE),
                   pl.BlockSpec((SUBLANES, LANE), lambda i: (0, 0))),
        out_shape=(jax.ShapeDtypeStruct((n, D_MODEL), F32),
                   jax.ShapeDtypeStruct((n, D_MODEL), F32),
                   jax.ShapeDtypeStruct((n // bm, SUBLANES, bm), jnp.int32),
                   jax.ShapeDtypeStruct((n, LANE), F32),
                   jax.ShapeDtypeStruct((SUBLANES, LANE), F32)),
        scratch_shapes=[pltpu.VMEM((1, LANE), F32)],
        compiler_params=pltpu.CompilerParams(dimension_semantics=("arbitrary",),
                                             vmem_limit_bytes=VMEM_LIMIT),
        name="merge_router",
    )(x2, oa, ob, *consts)


def _dest_kernel(pstart_ref, meta_ref, dest_ref):
    for k in range(2):
        expert = meta_ref[k]
        dest = meta_ref[2 + k]
        for e in range(N_EXPERTS):
            dest = dest + jnp.where(expert == e, pstart_ref[e], 0)
        dest_ref[k] = dest


def _dest_call(pstart, meta):
    _, rows, lanes = meta.shape
    grid_spec = pltpu.PrefetchScalarGridSpec(
        num_scalar_prefetch=1,
        grid=(1,),
        in_specs=[pl.BlockSpec(meta.shape, lambda i, ps: (0, 0, 0))],
        out_specs=pl.BlockSpec((2, rows, lanes), lambda i, ps: (0, 0, 0)),
    )
    return pl.pallas_call(
        _dest_kernel,
        grid_spec=grid_spec,
        out_shape=jax.ShapeDtypeStruct((2, rows, lanes), jnp.int32),
        name="dest",
    )(pstart, meta)


def _row_copy(src_ref, src_row, dst_ref, dst_row, sem):
    return pltpu.make_async_copy(src_ref.at[pl.ds(src_row, 1), :],
                                 dst_ref.at[pl.ds(dst_row, 1), :], sem)


def _dispatch_kernel(d0_ref, d1_ref, tail_ref, h_ref, xs_ref, zero_ref, sem, zsem, *, bt, blk):
    i = pl.program_id(0)

    @pl.when(i == 0)
    def _():
        zero_ref[...] = jnp.zeros_like(zero_ref)

        def fill(e):
            start = pl.multiple_of(tail_ref[e] * blk, blk)
            return pltpu.make_async_copy(zero_ref, xs_ref.at[pl.ds(start, blk), :], zsem)

        for e in range(2 * N_EXPERTS):
            @pl.when(tail_ref[e] >= 0)
            def _():
                fill(e).start()

        for e in range(2 * N_EXPERTS):
            @pl.when(tail_ref[e] >= 0)
            def _():
                fill(e).wait()

    def issue(g, carry):
        base = pl.multiple_of(g * SUBLANES, SUBLANES)
        rows = h_ref.at[pl.ds(base, SUBLANES), :]
        for u in range(SUBLANES):
            t = i * bt + base + u
            _row_copy(rows, u, xs_ref, d0_ref[t], sem).start()
            _row_copy(rows, u, xs_ref, d1_ref[t], sem).start()
        return carry

    lax.fori_loop(0, bt // SUBLANES, issue, 0)

    for _ in range(2):
        pltpu.make_async_copy(h_ref, xs_ref.at[pl.ds(0, bt), :], sem).wait()


def _dispatch_call(dest0, dest1, tail_blk, h2, p_rows):
    n, d = h2.shape
    bt = min(DISPATCH_ROWS, n)
    blk = MOE_ROWS
    grid_spec = pltpu.PrefetchScalarGridSpec(
        num_scalar_prefetch=3,
        grid=(n // bt,),
        in_specs=[pl.BlockSpec((bt, d), lambda i, d0, d1, tb: (i, 0))],
        out_specs=pl.BlockSpec(memory_space=pl.ANY),
        scratch_shapes=[pltpu.VMEM((blk, d), h2.dtype), pltpu.SemaphoreType.DMA(()),
                        pltpu.SemaphoreType.DMA(())],
    )
    return pl.pallas_call(
        functools.partial(_dispatch_kernel, bt=bt, blk=blk),
        grid_spec=grid_spec,
        out_shape=jax.ShapeDtypeStruct((p_rows, d), h2.dtype),
        compiler_params=pltpu.CompilerParams(dimension_semantics=("arbitrary",),
                                             vmem_limit_bytes=VMEM_LIMIT),
        name="dispatch",
    )(dest0, dest1, tail_blk, h2)


def _expert_kernel(blk_e_ref, used_ref, nxt_ref, slot_ref, x_ref, w1_ref, w3_ref, w2_ref, y_ref,
                   w1b, w3b, w2b, w1f, w3f, w2f, sem):
    i = pl.program_id(0)
    live = i < used_ref[0]
    fresh = (i == 0) | (blk_e_ref[i] != blk_e_ref[jnp.maximum(i - 1, 0)])
    slot = slot_ref[i]

    def fetch(e, s):
        return [pltpu.make_async_copy(src.at[e], dst.at[s], sem.at[s])
                for src, dst in ((w1_ref, w1f), (w3_ref, w3f), (w2_ref, w2f))]

    @pl.when(live & (i == 0))
    def _():
        for c in fetch(blk_e_ref[0], slot):
            c.start()

    @pl.when(live & fresh)
    def _():
        for c in fetch(blk_e_ref[i], slot):
            c.wait()
        w1b[...] = w1f[slot].astype(BF16)
        w3b[...] = w3f[slot].astype(BF16)
        w2b[...] = w2f[slot].astype(BF16)

        @pl.when(nxt_ref[i] >= 0)
        def _():
            for c in fetch(nxt_ref[i], 1 - slot):
                c.start()

    @pl.when(live)
    def _():
        xb = x_ref[...].astype(BF16)
        a = jnp.dot(xb, w1b[...], preferred_element_type=F32)
        g = jnp.dot(xb, w3b[...], preferred_element_type=F32)
        mid = (a / (1.0 + jnp.exp(-a))) * g
        y_ref[...] = jnp.dot(mid.astype(BF16), w2b[...], preferred_element_type=F32)

    @pl.when(i >= used_ref[0])
    def _():
        y_ref[...] = jnp.zeros_like(y_ref)


def _expert_call(blk_e, n_used, nxt, wslot, xs, w1, w3, w2):
    p_rows, d = xs.shape
    blk = MOE_ROWS
    x_idx = lambda i, be, nu, nx, sl: (jnp.minimum(i, nu[0] - 1), 0)
    hbm = pl.BlockSpec(memory_space=pl.ANY)
    grid_spec = pltpu.PrefetchScalarGridSpec(
        num_scalar_prefetch=4,
        grid=(p_rows // blk,),
        in_specs=[pl.BlockSpec((blk, d), x_idx), hbm, hbm, hbm],
        out_specs=pl.BlockSpec((blk, d), lambda i, be, nu, nx, sl: (i, 0)),
        scratch_shapes=[pltpu.VMEM((d, D_EXPERT), BF16), pltpu.VMEM((d, D_EXPERT), BF16),
                        pltpu.VMEM((D_EXPERT, d), BF16),
                        pltpu.VMEM((2, d, D_EXPERT), F32), pltpu.VMEM((2, d, D_EXPERT), F32),
                        pltpu.VMEM((2, D_EXPERT, d), F32), pltpu.SemaphoreType.DMA((2,))],
    )
    return pl.pallas_call(
        _expert_kernel,
        grid_spec=grid_spec,
        out_shape=jax.ShapeDtypeStruct((p_rows, d), F32),
        compiler_params=pltpu.CompilerParams(dimension_semantics=("arbitrary",),
                                             vmem_limit_bytes=VMEM_LIMIT),
        name="experts",
    )(blk_e, n_used, nxt, wslot, xs, w1, w3, w2)


def _combine_kernel(d0_ref, d1_ref, x1_ref, mf_ref, g_ref, yb_ref, o_ref, buf0, buf1, sem, *, bt, steps):
    i = pl.program_id(0)
    slot = i % 2

    def gather(step, s):
        def issue(g, carry):
            base = pl.multiple_of(g * SUBLANES, SUBLANES)
            rows0 = buf0.at[s, pl.ds(base, SUBLANES), :]
            rows1 = buf1.at[s, pl.ds(base, SUBLANES), :]
            for u in range(SUBLANES):
                t = step * bt + base + u
                _row_copy(yb_ref, d0_ref[t], rows0, u, sem.at[s]).start()
                _row_copy(yb_ref, d1_ref[t], rows1, u, sem.at[s]).start()
            return carry

        lax.fori_loop(0, bt // SUBLANES, issue, 0)

    @pl.when(i == 0)
    def _():
        gather(0, 0)

    @pl.when(i + 1 < steps)
    def _():
        gather(i + 1, 1 - slot)

    for buf in (buf0, buf1):
        pltpu.make_async_copy(yb_ref.at[pl.ds(0, bt), :], buf.at[slot], sem.at[slot]).wait()

    mf = mf_ref[...]
    x2 = x1_ref[...] + mf[:, 0:1] * buf0[slot] + mf[:, 1:2] * buf1[slot]
    o_ref[...] = _rms(x2, g_ref[...])


def _combine_call(dest0, dest1, x1, mf, g_final, yb):
    n, d = x1.shape
    bt = MOVE_ROWS
    grid_spec = pltpu.PrefetchScalarGridSpec(
        num_scalar_prefetch=2,
        grid=(n // bt,),
        in_specs=[pl.BlockSpec((bt, d), lambda i, d0, d1: (i, 0)),
                  pl.BlockSpec((bt, LANE), lambda i, d0, d1: (i, 0)),
                  pl.BlockSpec((1, d), lambda i, d0, d1: (0, 0)),
                  pl.BlockSpec(memory_space=pl.ANY)],
        out_specs=pl.BlockSpec((bt, d), lambda i, d0, d1: (i, 0)),
        scratch_shapes=[pltpu.VMEM((2, bt, d), F32), pltpu.VMEM((2, bt, d), F32),
                        pltpu.SemaphoreType.DMA((2,))],
    )
    return pl.pallas_call(
        functools.partial(_combine_kernel, bt=bt, steps=n // bt),
        grid_spec=grid_spec,
        out_shape=jax.ShapeDtypeStruct((n, d), F32),
        compiler_params=pltpu.CompilerParams(dimension_semantics=("arbitrary",),
                                             vmem_limit_bytes=VMEM_LIMIT),
        name="combine",
    )(dest0, dest1, x1, mf, g_final, yb)


def _pad_heads(w, heads, width, padded, scale=1.0):
    d = w.shape[0]
    w = (w * scale).reshape(d, heads, width)
    return jnp.pad(w, ((0, 0), (0, 0), (0, padded - width))).reshape(d, heads * padded)


def _build_proj_params(w_in, b_fgate):
    fw, dw = FOX_HEADS * FOX_HEAD_DIM, DIFF_HEADS * DIFF_V_DIM
    o = 0
    fq, fk, fv = (w_in[:, o + k * fw:o + (k + 1) * fw] for k in range(3))
    o += 3 * fw
    ff = w_in[:, o:o + FOX_HEADS]
    o += FOX_HEADS
    dq, dk, dv = (w_in[:, o + k * dw:o + (k + 1) * dw] for k in range(3))
    d = w_in.shape[0]
    dq = dq.reshape(d, DIFF_HEADS, 2, DIFF_HEAD_DIM)
    dk = dk.reshape(d, DIFF_HEADS, 2, DIFF_HEAD_DIM)
    fscale = FOX_HEAD_DIM ** -0.5 * LOG2E
    dscale = DIFF_HEAD_DIM ** -0.5 * LOG2E
    half = lambda t, k: t[:, :, k, :].reshape(d, DIFF_HEADS * DIFF_HEAD_DIM)
    w_t = jnp.concatenate([
        _pad_heads(fq, FOX_HEADS, FOX_HEAD_DIM, LANE, fscale),
        _pad_heads(fv, FOX_HEADS, FOX_HEAD_DIM, FOX_V_ROWS),
        _pad_heads(half(dq, 0), DIFF_HEADS, DIFF_HEAD_DIM, LANE, dscale),
        _pad_heads(half(dq, 1), DIFF_HEADS, DIFF_HEAD_DIM, LANE, dscale),
        _pad_heads(dv, DIFF_HEADS, DIFF_V_DIM, DIFF_V_ROWS),
    ], axis=1).T.astype(BF16)
    w_row = jnp.concatenate([
        _pad_heads(fk, FOX_HEADS, FOX_HEAD_DIM, LANE),
        jnp.pad(ff, ((0, 0), (0, LANE - FOX_HEADS))),
        _pad_heads(half(dk, 0), DIFF_HEADS, DIFF_HEAD_DIM, LANE),
        _pad_heads(half(dk, 1), DIFF_HEADS, DIFF_HEAD_DIM, LANE),
    ], axis=1).astype(BF16)

    slopes = [2.0 ** (-8.0 / DIFF_HEADS * (hh + 1)) for hh in range(DIFF_HEADS)]
    b = jnp.zeros((_T_ROWS,), F32)
    for hh in range(FOX_HEADS):
        b = b.at[_TQF + hh * LANE + FOX_HEAD_DIM:_TQF + hh * LANE + FOX_HEAD_DIM + 3].set(-1.0)
        b = b.at[_TVF + hh * FOX_V_ROWS + FOX_HEAD_DIM].set(1.0)
    for hh in range(DIFF_HEADS):
        for base in (_TQ1, _TQ2):
            b = b.at[base + hh * LANE + DIFF_HEAD_DIM:base + hh * LANE + DIFF_HEAD_DIM + 3].set(slopes[hh])
        b = b.at[_TVD + hh * DIFF_V_ROWS + DIFF_V_DIM].set(1.0)
    bf_pad = jnp.pad(b_fgate, (0, LANE - FOX_HEADS))[None, :]
    return w_t, b[:, None], w_row, bf_pad, slopes


def kernel(x, g_mix, w_in, b_fgate, b_gate, lam_q1, lam_k1, lam_q2, lam_k2, g_subln, w_pa, w_pb, w_o,
           g_moe, w_group, b_group, w_expert, b_expert, w1, w3, w2, g_final):
    bsz, seq, d = x.shape
    n = bsz * seq
    tile = ATT_TILE
    nk = seq // tile
    x2 = x.reshape(n, d)

    w_t, b_t, w_row, bf_pad, slopes = _build_proj_params(w_in[0], b_fgate[0])
    qft, vft, q1t, q2t, vdt, kf, k1, k2, fcol, cblk = _proj_call(x2, g_mix, w_t, b_t, w_row, bf_pad, bsz, seq)
    shp = lambda a: a.reshape(bsz, seq, a.shape[1])
    c_flat = cblk[:, 0, :FOX_HEADS].reshape(bsz, nk, FOX_HEADS).transpose(0, 2, 1).reshape(-1)
    f_rows = (fcol[:, :FOX_HEADS].reshape(bsz, nk, tile, FOX_HEADS).transpose(0, 3, 1, 2)
              .reshape(bsz, FOX_HEADS, nk, 1, tile))

    oa = _fox_call(c_flat, qft, shp(kf), vft, f_rows, tile)
    lam = (jnp.exp(jnp.sum(lam_q1[0] * lam_k1[0])) - jnp.exp(jnp.sum(lam_q2[0] * lam_k2[0])) + LAM_INIT)
    scal = jnp.concatenate([jnp.asarray(slopes, F32), lam[None].astype(F32)])
    ob = _diff_call(scal, q1t, q2t, shp(k1), shp(k2), vdt, g_subln[0][:, None], tile)

    o = 3 * FOX_HEADS * FOX_HEAD_DIM + FOX_HEADS + 3 * DIFF_HEADS * DIFF_V_DIM
    w_g = w_in[0][:, o:].astype(BF16)
    w_pa_pad = jnp.pad(w_pa[0].reshape(FOX_HEADS, FOX_HEAD_DIM, d),
                       ((0, 0), (0, LANE - FOX_HEAD_DIM), (0, 0))).reshape(FOX_HEADS * LANE, d).astype(BF16)
    w_r = jnp.zeros((d, 2 * LANE), F32)
    w_r = w_r.at[:, :N_EXPERTS].set(w_expert[0]).at[:, LANE:LANE + N_GROUPS].set(w_group[0])
    wr_hi = w_r.astype(BF16)
    wr_lo = (w_r - wr_hi.astype(F32)).astype(BF16)
    b_r = jnp.zeros((1, 2 * LANE), F32)
    b_r = b_r.at[0, :N_EXPERTS].set(b_expert[0].reshape(-1)).at[0, LANE:LANE + N_GROUPS].set(b_group[0])
    x1, h2, mi, mf, cnt = _merge_call(
        x2, oa.reshape(n, -1), ob.reshape(n, -1), g_mix, w_g, b_gate, w_pa_pad,
        w_pb[0].astype(BF16), w_o[0].astype(BF16), g_moe, wr_hi, wr_lo, b_r)

    blk = MOE_ROWS
    counts = cnt[0, :N_EXPERTS].astype(jnp.int32)
    padded = (counts + blk - 1) // blk * blk
    pend = jnp.cumsum(padded)
    pstart = pend - padded
    meta = mi.transpose(1, 0, 2).reshape(SUBLANES, n // LANE, LANE)
    dest = _dest_call(pstart.astype(jnp.int32), meta).reshape(2, n)
    dest0, dest1 = dest[0], dest[1]
    p_rows = n * 2 + N_EXPERTS * blk
    n_blk = p_rows // blk
    blk_pos = jnp.arange(n_blk, dtype=jnp.int32) * blk
    blk_e = jnp.minimum(jnp.sum(pend[None, :] <= blk_pos[:, None], axis=1), N_EXPERTS - 1).astype(jnp.int32)
    n_used = (pend[-1:] // blk).astype(jnp.int32)

    spare = n_used[0] + jnp.arange(N_EXPERTS, dtype=jnp.int32)
    tail_blk = jnp.concatenate([jnp.where(padded > 0, pend // blk - 1, -1),
                                jnp.where(spare < n_blk, spare, -1)]).astype(jnp.int32)
    xs = _dispatch_call(dest0, dest1, tail_blk, h2, p_rows)
    eid = jnp.arange(N_EXPERTS, dtype=jnp.int32)
    later = jnp.where((eid[None, :] > eid[:, None]) & (padded[None, :] > 0), eid[None, :], N_EXPERTS)
    nxt_of = jnp.min(later, axis=1)
    nxt = jnp.where(nxt_of < N_EXPERTS, nxt_of, -1)[blk_e].astype(jnp.int32)
    wslot = ((jnp.cumsum((padded > 0).astype(jnp.int32)) - 1)[blk_e] % 2).astype(jnp.int32)
    yb = _expert_call(blk_e, n_used, nxt, wslot, xs, w1[0], w3[0], w2[0])
    out = _combine_call(dest0, dest1, x1, mf, g_final[None, :], yb)
    return out.reshape(bsz, seq, d)
```
